```python
import math
import functools
import jax
import jax.numpy as jnp
from jax import lax
import numpy as np

D_MODEL = 1024
BATCH = 2
SEQ = 8192
DEPTH = 4
DEC_BATCH = 32
DEC_SEQ = 1
PAST_LEN = 8192
PAGE_SIZE = 128

N_MIXERS = 4
GROUP_WIDTH = D_MODEL // N_MIXERS
HEAD_DIM = 64
N_HEADS = GROUP_WIDTH // HEAD_DIM
D_FF = 2816
FFN_RES = 0.5
NORM_EPS = 1e-6
RWKV_W_LORA = 32
RWKV_A_LORA = 32
RWKV_G_LORA = 64
RWKV_LN_EPS = 64e-5
RET_CHUNK = 64
RET_THETA = 10000.0
GDN_CONV = 4
GDN_CHUNK = 64
MOBA_BLOCK = 256
MOBA_TOPK = 3
MOBA_QBLOCK = 128
A_COLS = 3 * GROUP_WIDTH + RWKV_W_LORA + RWKV_A_LORA + RWKV_G_LORA
B_COLS = 4 * GROUP_WIDTH
C_QKV = 3 * GROUP_WIDTH
C_COLS = C_QKV + GROUP_WIDTH + 2 * N_HEADS
D_COLS = 3 * GROUP_WIDTH
IN_COLS = A_COLS + B_COLS + C_COLS + D_COLS

kernel_name = 'hybrid_rwkv_ret_gdn_moba_step'


def rms_norm(x, g):
    xf = x.astype(jnp.float32)
    y = xf * lax.rsqrt(jnp.mean(xf * xf, axis=-1, keepdims=True) + NORM_EPS)
    return (y * g.astype(jnp.float32)).astype(x.dtype)


def heads(t):
    return t.reshape(t.shape[:-1] + (N_HEADS, HEAD_DIM))


def l2norm(t):
    return t * lax.rsqrt(jnp.sum(t * t, axis=-1, keepdims=True) + 1e-6)


def pad_axis(t, pad, axis):
    widths = [(0, 0)] * t.ndim
    widths[axis] = (0, pad)
    return jnp.pad(t, widths)


def modulation(c, w_mod_l, b_mod_l):
    m = jax.nn.silu(c) @ w_mod_l + b_mod_l
    return jnp.split(m[:, None, :], 9, axis=-1)


def adaln_pre(x, g, shift, scale):
    return rms_norm(x, g) * (1 + scale) + shift


def swiglu(h, w_gate, w_up, w_down):
    return (jax.nn.silu(h @ w_gate) * (h @ w_up)) @ w_down


def chunk_decay_attention(q, k, v, logd, s0, chunk):
    B, H, L, _ = q.shape
    c = min(chunk, L)
    n = -(-L // c)
    pad = n * c - L
    q, k, v, logd = (pad_axis(t, pad, 2) for t in (q, k, v, logd))
    q = q.reshape(B, H, n, c, -1)
    k = k.reshape(B, H, n, c, -1)
    v = v.reshape(B, H, n, c, -1)
    G = jnp.cumsum(logd.reshape(B, H, n, c), axis=-1)
    causal = jnp.tril(jnp.ones((c, c), bool))
    dmask = jnp.exp(jnp.where(causal, G[..., :, None] - G[..., None, :], -jnp.inf))
    intra = jnp.einsum('bhnij,bhnjd->bhnid', jnp.einsum('bhnid,bhnjd->bhnij', q, k) * dmask, v)

    def step(S, inp):
        qn, kn, vn, Gn = inp
        o = jnp.einsum('bhcd,bhde->bhce', qn * jnp.exp(Gn)[..., None], S)
        g_last = Gn[..., -1]
        S = jnp.exp(g_last)[..., None, None] * S + jnp.einsum(
            'bhcd,bhce->bhde', kn * jnp.exp(g_last[..., None] - Gn)[..., None], vn)
        return S, o

    s_new, inter = lax.scan(step, s0, tuple(jnp.moveaxis(t, 2, 0) for t in (q, k, v, G)))
    o = intra + jnp.moveaxis(inter, 0, 2)
    return o.reshape(B, H, n * c, -1)[:, :, :L], s_new


def chunk_gated_delta(q, k, v, beta, logd, s0, chunk):
    f32 = jnp.float32
    B, H, L, _ = q.shape
    c = min(chunk, L)
    n = -(-L // c)
    pad = n * c - L
    q, k, v, beta, logd = (pad_axis(t, pad, 2) for t in (q, k, v, beta, logd))
    q = q.reshape(B, H, n, c, -1)
    k = k.reshape(B, H, n, c, -1)
    v = v.reshape(B, H, n, c, -1)
    beta = beta.reshape(B, H, n, c)
    G = jnp.cumsum(logd.reshape(B, H, n, c), axis=-1)
    causal = jnp.tril(jnp.ones((c, c), bool))
    strict = jnp.tril(jnp.ones((c, c), f32), -1)
    dmask = jnp.exp(jnp.where(causal, G[..., :, None] - G[..., None, :], -jnp.inf))
    kb = k * beta[..., None]
    m = jnp.einsum('bhnid,bhnjd->bhnij', kb, k) * dmask * strict
    eye = jnp.eye(c, dtype=f32)
    tmat = lax.linalg.triangular_solve(eye + m, jnp.broadcast_to(eye, m.shape), left_side=True, lower=True)
    u = tmat @ (v * beta[..., None])
    w = tmat @ (kb * jnp.exp(G)[..., None])
    aqk = jnp.einsum('bhnid,bhnjd->bhnij', q, k) * dmask

    def step(S, inp):
        qn, kn, un, wn, an, Gn = inp
        vnew = un - jnp.einsum('bhck,bhkv->bhcv', wn, S)
        o = jnp.einsum('bhck,bhkv->bhcv', qn * jnp.exp(Gn)[..., None], S) + jnp.einsum('bhij,bhjv->bhiv', an, vnew)
        g_last = Gn[..., -1]
        S = jnp.exp(g_last)[..., None, None] * S + jnp.einsum(
            'bhck,bhcv->bhkv', kn * jnp.exp(g_last[..., None] - Gn)[..., None], vnew)
        return S, o

    s_new, o = lax.scan(step, s0, tuple(jnp.moveaxis(t, 2, 0) for t in (q, k, u, w, aqk, G)))
    o = jnp.moveaxis(o, 0, 2)
    return o.reshape(B, H, n * c, -1)[:, :, :L], s_new


def rwkv7_mixer(p, shift_prev, s0, mu, w0, w2, a0, a2, g2, k_k, k_a, r_k, ln_w, ln_b):
    f32 = jnp.float32
    B, L, _ = p.shape
    gw = GROUP_WIDTH
    p = p.astype(f32)
    prev = jnp.concatenate([shift_prev[:, None, :].astype(f32), p[:, :-1]], axis=1)
    xs = p + (prev - p) * mu
    r, k, v, wl, al, gl = jnp.split(
        xs, [gw, 2 * gw, 3 * gw, 3 * gw + RWKV_W_LORA, 3 * gw + RWKV_W_LORA + RWKV_A_LORA], axis=-1)
    w_log = -jax.nn.softplus(-(w0 + jnp.tanh(wl) @ w2)) - 0.5
    decay = jnp.exp(-jnp.exp(w_log))
    a = jax.nn.sigmoid(a0 + al @ a2)
    g = jax.nn.sigmoid(gl) @ g2
    kk = l2norm(heads(k * k_k))
    k = k * (1 + (a - 1) * k_a)
    rh, kh, vh, ah, wh = heads(r), heads(k), heads(v), heads(a), heads(decay)

    def step(S, inp):
        r_t, w_t, k_t, v_t, a_t, b_t = inp
        sa = jnp.einsum('bhij,bhj->bhi', S, a_t)
        S = S * w_t[:, :, None, :] + sa[..., None] * b_t[:, :, None, :] + v_t[..., None] * k_t[:, :, None, :]
        return S, jnp.einsum('bhij,bhj->bhi', S, r_t)

    seq = tuple(jnp.swapaxes(t, 0, 1) for t in (rh, wh, kh, vh, -kk, kk * ah))
    s_new, y = lax.scan(step, s0.astype(f32), seq)
    y = jnp.swapaxes(y, 0, 1)
    mean = jnp.mean(y, axis=-1, keepdims=True)
    var = jnp.mean(jnp.square(y - mean), axis=-1, keepdims=True)
    y = ((y - mean) * lax.rsqrt(var + RWKV_LN_EPS)).reshape(B, L, gw) * ln_w + ln_b
    bonus = (jnp.sum(rh * kh * r_k, axis=-1, keepdims=True) * vh).reshape(B, L, gw)
    return (y + bonus) * g, p[:, -1], s_new


def xpos_rotate(x, pos):
    d = x.shape[-1]
    inv = 1.0 / (RET_THETA ** jnp.linspace(0.0, 1.0, d // 2, dtype=jnp.float32))
    ang = pos.astype(jnp.float32)[:, None] * inv[None, :]
    sin = jnp.sin(ang)[None, :, None, :]
    cos = jnp.cos(ang)[None, :, None, :]
    x1, x2 = x[..., 0::2], x[..., 1::2]
    return jnp.stack([x1 * cos - x2 * sin, x1 * sin + x2 * cos], axis=-1).reshape(x.shape)


def retention_mixer(p, s0, pos0):
    f32 = jnp.float32
    B, L, _ = p.shape
    q, k, v, gt = jnp.split(p.astype(f32), 4, axis=-1)
    pos = pos0 + jnp.arange(L)
    q = xpos_rotate(heads(q), pos)
    k = xpos_rotate(heads(k), pos) * HEAD_DIM ** -0.5
    log_gamma = jnp.log(1.0 - 2.0 ** (-5.0 - jnp.arange(N_HEADS, dtype=f32)))
    logd = jnp.broadcast_to(log_gamma[None, :, None], (B, N_HEADS, L))
    tr = lambda t: jnp.swapaxes(t, 1, 2)
    o, s_new = chunk_decay_attention(tr(q), tr(k), tr(heads(v)), logd, s0.astype(f32), RET_CHUNK)
    o = tr(o)
    o = o * lax.rsqrt(jnp.mean(o * o, axis=-1, keepdims=True) + NORM_EPS)
    return jax.nn.silu(gt) * o.reshape(B, L, GROUP_WIDTH), s_new


def gdn_mixer(p, conv_prev, s0, conv_w, a_log, dt_bias, norm_w):
    f32 = jnp.float32
    B, L, _ = p.shape
    p = p.astype(f32)
    qkv, gate, a_in, b_in = jnp.split(p, [C_QKV, C_QKV + GROUP_WIDTH, C_QKV + GROUP_WIDTH + N_HEADS], axis=-1)
    ext = jnp.concatenate([conv_prev.astype(f32), qkv], axis=1)
    conv = sum(ext[:, j:j + L] * conv_w[j] for j in range(GDN_CONV))
    q, k, v = jnp.split(jax.nn.silu(conv), 3, axis=-1)
    q = l2norm(heads(q)) * HEAD_DIM ** -0.5
    k = l2norm(heads(k))
    v = heads(v)
    beta = jax.nn.sigmoid(b_in)
    logd = -jnp.exp(a_log.astype(f32)) * jax.nn.softplus(a_in + dt_bias)
    tr = lambda t: jnp.swapaxes(t, 1, 2)
    o, s_new = chunk_gated_delta(tr(q), tr(k), tr(v), tr(beta), tr(logd), s0.astype(f32), GDN_CHUNK)
    o = tr(o)
    o = o * lax.rsqrt(jnp.mean(o * o, axis=-1, keepdims=True) + NORM_EPS) * norm_w * jax.nn.silu(heads(gate))
    return o.reshape(B, L, GROUP_WIDTH), ext[:, -(GDN_CONV - 1):], s_new


def moba_past_blocks(q, k_blocks, v_blocks, means, own_blk):
    B, H, Lq, _ = q.shape
    nb = k_blocks.shape[2]
    topk = min(MOBA_TOPK, nb)
    gate = jnp.einsum('bhqd,bhnd->bhqn', q.astype(jnp.float32), means)
    past = jnp.arange(nb)[None, :] < own_blk[:, None]
    gate = jnp.where(past[None, None], gate, -jnp.inf)
    _, idx = lax.top_k(gate, topk)
    take = jax.vmap(jax.vmap(lambda blk, i: blk[i]))
    k_sel = take(k_blocks, idx)
    v_sel = take(v_blocks, idx)
    valid = jnp.arange(topk)[None, :] < own_blk[:, None]
    s = jnp.einsum('bhqd,bhqjkd->bhqjk', q, k_sel)
    s = jnp.where(valid[None, None, :, :, None], s, -jnp.inf)
    return (s.reshape(B, H, Lq, topk * MOBA_BLOCK),
            v_sel.reshape(B, H, Lq, topk * MOBA_BLOCK, HEAD_DIM))


def softmax_merge(s_sel, s_own):
    n = s_sel.shape[-1]
    prob = jax.nn.softmax(jnp.concatenate([s_sel, s_own], axis=-1).astype(jnp.float32), axis=-1)
    return prob[..., :n], prob[..., n:]


def moba_prompt(q, k, v):
    B, L = q.shape[:2]
    qh = jnp.swapaxes(q, 1, 2) * HEAD_DIM ** -0.5
    nb = -(-L // MOBA_BLOCK)
    pad = nb * MOBA_BLOCK - L
    kh = pad_axis(jnp.swapaxes(k, 1, 2), pad, 2)
    vh = pad_axis(jnp.swapaxes(v, 1, 2), pad, 2)
    kb = kh.reshape(B, N_HEADS, nb, MOBA_BLOCK, HEAD_DIM)
    vb = vh.reshape(B, N_HEADS, nb, MOBA_BLOCK, HEAD_DIM)
    means = jnp.mean(kb.astype(jnp.float32), axis=3)

    def query_block(start):
        qc = lax.dynamic_slice_in_dim(qh, start, MOBA_QBLOCK, axis=2)
        qpos = start + jnp.arange(MOBA_QBLOCK)
        own = qpos // MOBA_BLOCK
        s_sel, v_sel = moba_past_blocks(qc, kb, vb, means, own)
        ostart = (start // MOBA_BLOCK) * MOBA_BLOCK
        k_own = lax.dynamic_slice_in_dim(kh, ostart, MOBA_BLOCK, axis=2)
        v_own = lax.dynamic_slice_in_dim(vh, ostart, MOBA_BLOCK, axis=2)
        kpos = ostart + jnp.arange(MOBA_BLOCK)
        s_own = jnp.einsum('bhqd,bhkd->bhqk', qc, k_own)
        s_own = jnp.where((kpos[None, :] <= qpos[:, None])[None, None], s_own, -jnp.inf)
        p_sel, p_own = softmax_merge(s_sel, s_own)
        return (jnp.einsum('bhqk,bhqkd->bhqd', p_sel, v_sel)
                + jnp.einsum('bhqk,bhkd->bhqd', p_own, v_own))

    o = lax.map(query_block, jnp.arange(0, L, MOBA_QBLOCK))
    o = jnp.moveaxis(o, 0, 2).reshape(B, N_HEADS, L, HEAD_DIM)
    return jnp.swapaxes(o, 1, 2).reshape(B, L, GROUP_WIDTH)


def moba_sample(q, k, v, k_past, v_past, past_len):
    B, L = q.shape[:2]
    qh = jnp.swapaxes(q, 1, 2) * HEAD_DIM ** -0.5
    total = past_len + L
    nb = -(-total // MOBA_BLOCK)
    pad = nb * MOBA_BLOCK - total
    k_all = pad_axis(jnp.swapaxes(jnp.concatenate([k_past.astype(k.dtype), k], axis=1), 1, 2), pad, 2)
    v_all = pad_axis(jnp.swapaxes(jnp.concatenate([v_past.astype(v.dtype), v], axis=1), 1, 2), pad, 2)
    kb = k_all.reshape(B, N_HEADS, nb, MOBA_BLOCK, HEAD_DIM)
    vb = v_all.reshape(B, N_HEADS, nb, MOBA_BLOCK, HEAD_DIM)
    means = jnp.mean(kb.astype(jnp.float32), axis=3)
    qpos = past_len + jnp.arange(L)
    own = qpos // MOBA_BLOCK
    s_sel, v_sel = moba_past_blocks(qh, kb, vb, means, own)
    kidx = own[:, None] * MOBA_BLOCK + jnp.arange(MOBA_BLOCK)[None, :]
    k_own = k_all[:, :, kidx]
    v_own = v_all[:, :, kidx]
    s_own = jnp.einsum('bhqd,bhqkd->bhqk', qh, k_own)
    s_own = jnp.where((kidx <= qpos[:, None])[None, None], s_own, -jnp.inf)
    p_sel, p_own = softmax_merge(s_sel, s_own)
    o = jnp.einsum('bhqk,bhqkd->bhqd', p_sel, v_sel) + jnp.einsum('bhqk,bhqkd->bhqd', p_own, v_own)
    return jnp.swapaxes(o, 1, 2).reshape(B, L, GROUP_WIDTH)


def hybrid_mixer(h, pos0, attend, s_rwkv, s_shift, s_ret, s_gdn, s_conv, w_in_l, w_out_l,
                 mu, w0, w2, a0, a2, g2, k_k, k_a, r_k, ln_w, ln_b, conv_w, a_log, dt_bias, norm_w):
    p = h @ w_in_l
    pa, pb, pc, pd = jnp.split(p, [A_COLS, A_COLS + B_COLS, A_COLS + B_COLS + C_COLS], axis=-1)
    ya, shift_new, rwkv_new = rwkv7_mixer(pa, s_shift, s_rwkv, mu, w0, w2, a0, a2, g2, k_k, k_a, r_k, ln_w, ln_b)
    yb, ret_new = retention_mixer(pb, s_ret, pos0)
    yc, conv_new, gdn_new = gdn_mixer(pc, s_conv, s_gdn, conv_w, a_log, dt_bias, norm_w)
    qd, kd, vd = (heads(t) for t in jnp.split(pd, 3, axis=-1))
    yd = attend(qd, kd, vd)
    y = jnp.concatenate([ya, yb, yc, yd], axis=-1).astype(h.dtype) @ w_out_l
    return y, (kd, vd, rwkv_new, shift_new, ret_new, gdn_new, conv_new)


def trunk_layer(x, c, pos0, attend, s_rwkv, s_shift, s_ret, s_gdn, s_conv,
                w_mod_l, b_mod_l, norm_pre_l, norm_post_l, wg_l, wu_l, wd_l, mix_w):
    sh1, sc1, g1, sh2, sc2, g2, sh3, sc3, g3 = modulation(c, w_mod_l, b_mod_l)
    h = adaln_pre(x, norm_pre_l[0], sh1, sc1)
    x = x + FFN_RES * g1 * rms_norm(swiglu(h, wg_l[0], wu_l[0], wd_l[0]), norm_post_l[0])
    h = adaln_pre(x, norm_pre_l[1], sh2, sc2)
    y, new_states = hybrid_mixer(h, pos0, attend, s_rwkv, s_shift, s_ret, s_gdn, s_conv, *mix_w)
    x = x + g2 * rms_norm(y, norm_post_l[1])
    h = adaln_pre(x, norm_pre_l[2], sh3, sc3)
    x = x + FFN_RES * g3 * rms_norm(swiglu(h, wg_l[1], wu_l[1], wd_l[1]), norm_post_l[2])
    return x, new_states


def setup_inputs(seed: int = 0) -> dict:
    key = jax.random.key(seed)
    keys = iter(jax.random.split(key, 64))
    f32 = jnp.float32

    def normal(shape, scale=1.0):
        return jax.random.normal(next(keys), shape, f32) * scale

    def uniform(shape, lo, hi):
        return jax.random.uniform(next(keys), shape, f32, lo, hi)

    n_pages = PAST_LEN // PAGE_SIZE
    n_used = DEC_BATCH * n_pages
    n_phys = (5 * n_used + 3) // 4
    page_table = jax.random.permutation(next(keys), n_phys)[:n_used].reshape(DEC_BATCH, n_pages).astype(jnp.int32)
    dt = jnp.exp(uniform((DEPTH, N_HEADS), math.log(1e-3), math.log(1e-1)))
    mat = (N_HEADS, HEAD_DIM, HEAD_DIM)
    return {
        'x_prompt': normal((BATCH, SEQ, D_MODEL)),
        'x_sample': normal((DEC_BATCH, DEC_SEQ, D_MODEL)),
        'c_prompt': normal((BATCH, D_MODEL)),
        'c_sample': normal((DEC_BATCH, D_MODEL)),
        'cache_k': normal((DEPTH, n_phys, PAGE_SIZE, N_HEADS, HEAD_DIM)),
        'cache_v': normal((DEPTH, n_phys, PAGE_SIZE, N_HEADS, HEAD_DIM)),
        'page_table': page_table,
        'state_rwkv': normal((DEPTH, DEC_BATCH) + mat, 0.5),
        'state_rwkv_shift': normal((DEPTH, DEC_BATCH, A_COLS)),
        'state_ret': normal((DEPTH, DEC_BATCH) + mat, 0.5),
        'state_gdn': normal((DEPTH, DEC_BATCH) + mat, 0.1),
        'state_gdn_conv': normal((DEPTH, DEC_BATCH, GDN_CONV - 1, C_QKV)),
        'w_mod': normal((DEPTH, D_MODEL, 9 * D_MODEL), D_MODEL ** -0.5),
        'b_mod': normal((DEPTH, 9 * D_MODEL), 0.01),
        'norm_pre': 1.0 + normal((DEPTH, 3, D_MODEL), 0.02),
        'norm_post': 1.0 + normal((DEPTH, 3, D_MODEL), 0.02),
        'ffn_w_gate': normal((DEPTH, 2, D_MODEL, D_FF), D_MODEL ** -0.5),
        'ffn_w_up': normal((DEPTH, 2, D_MODEL, D_FF), D_MODEL ** -0.5),
        'ffn_w_down': normal((DEPTH, 2, D_FF, D_MODEL), D_FF ** -0.5),
        'w_in': normal((DEPTH, D_MODEL, IN_COLS), D_MODEL ** -0.5),
        'w_out': normal((DEPTH, D_MODEL, D_MODEL), D_MODEL ** -0.5),
        'rwkv_mu': uniform((DEPTH, A_COLS), 0.0, 1.0),
        'rwkv_w0': uniform((DEPTH, GROUP_WIDTH), -6.0, -1.0),
        'rwkv_w2': normal((DEPTH, RWKV_W_LORA, GROUP_WIDTH), 0.1 * RWKV_W_LORA ** -0.5),
        'rwkv_a0': normal((DEPTH, GROUP_WIDTH), 0.1),
        'rwkv_a2': normal((DEPTH, RWKV_A_LORA, GROUP_WIDTH), 0.1 * RWKV_A_LORA ** -0.5),
        'rwkv_g2': normal((DEPTH, RWKV_G_LORA, GROUP_WIDTH), RWKV_G_LORA ** -0.5),
        'rwkv_k_k': 0.85 + normal((DEPTH, GROUP_WIDTH), 0.02),
        'rwkv_k_a': 1.0 + normal((DEPTH, GROUP_WIDTH), 0.02),
        'rwkv_r_k': normal((DEPTH, N_HEADS, HEAD_DIM), 0.1),
        'rwkv_ln_w': 1.0 + normal((DEPTH, GROUP_WIDTH), 0.02),
        'rwkv_ln_b': normal((DEPTH, GROUP_WIDTH), 0.01),
        'gdn_conv_w': normal((DEPTH, GDN_CONV, C_QKV), GDN_CONV ** -0.5),
        'gdn_a_log': jnp.log(uniform((DEPTH, N_HEADS), 1.0, 16.0)),
        'gdn_dt_bias': dt + jnp.log(-jnp.expm1(-dt)),
        'gdn_norm_w': 1.0 + normal((DEPTH, HEAD_DIM), 0.02),
    }


def reference(x_prompt, x_sample, c_prompt, c_sample, cache_k, cache_v, page_table,
              state_rwkv, state_rwkv_shift, state_ret, state_gdn, state_gdn_conv,
              w_mod, b_mod, norm_pre, norm_post, ffn_w_gate, ffn_w_up, ffn_w_down,
              w_in, w_out, rwkv_mu, rwkv_w0, rwkv_w2, rwkv_a0, rwkv_a2, rwkv_g2,
              rwkv_k_k, rwkv_k_a, rwkv_r_k, rwkv_ln_w, rwkv_ln_b,
              gdn_conv_w, gdn_a_log, gdn_dt_bias, gdn_norm_w):
    f32 = jnp.float32
    bp, bs = x_prompt.shape[0], x_sample.shape[0]
    past_len = page_table.shape[1] * PAGE_SIZE
    mat = (N_HEADS, HEAD_DIM, HEAD_DIM)
    zero_states = (jnp.zeros((bp,) + mat, f32), jnp.zeros((bp, A_COLS), f32),
                   jnp.zeros((bp,) + mat, f32), jnp.zeros((bp,) + mat, f32),
                   jnp.zeros((bp, GDN_CONV - 1, C_QKV), f32))
    xp, xs = x_prompt, x_sample
    news_p, news_s = [], []
    for l in range(DEPTH):
        layer_w = (w_mod[l], b_mod[l], norm_pre[l], norm_post[l], ffn_w_gate[l], ffn_w_up[l], ffn_w_down[l])
        mix_w = (w_in[l], w_out[l], rwkv_mu[l], rwkv_w0[l], rwkv_w2[l], rwkv_a0[l], rwkv_a2[l], rwkv_g2[l],
                 rwkv_k_k[l], rwkv_k_a[l], rwkv_r_k[l], rwkv_ln_w[l], rwkv_ln_b[l],
                 gdn_conv_w[l], gdn_a_log[l], gdn_dt_bias[l], gdn_norm_w[l])
        xp, new_p = trunk_layer(xp, c_prompt, 0, moba_prompt, *zero_states, *layer_w, mix_w)
        k_past = cache_k[l][page_table].reshape(bs, past_len, N_HEADS, HEAD_DIM)
        v_past = cache_v[l][page_table].reshape(bs, past_len, N_HEADS, HEAD_DIM)
        attend_s = functools.partial(moba_sample, k_past=k_past, v_past=v_past, past_len=past_len)
        xs, new_s = trunk_layer(xs, c_sample, past_len, attend_s, state_rwkv[l], state_rwkv_shift[l],
                                state_ret[l], state_gdn[l], state_gdn_conv[l], *layer_w, mix_w)
        news_p.append(new_p)
        news_s.append(new_s)
    kp, vp, rwp, shp, rtp, gdp, cvp = [jnp.stack(t).astype(x_prompt.dtype) for t in zip(*news_p)]
    ks, vs, rws, shs, rts, gds, cvs = [jnp.stack(t).astype(x_sample.dtype) for t in zip(*news_s)]
    return (xp, xs, kp, vp, ks, vs, rwp, rws, shp, shs, rtp, rts, gdp, gds, cvp, cvs)
```

```python
import functools
import math

import numpy as np
import jax
import jax.numpy as jnp
from jax import lax
from jax.experimental import pallas as pl
from jax.experimental.pallas import tpu as pltpu

f32 = jnp.float32
bf16 = jnp.bfloat16
HI = lax.Precision.HIGHEST

GROUP_WIDTH = 256
HEAD_DIM = 64
N_HEADS = 4
FFN_RES = 0.5
NORM_EPS = 1e-6
RWKV_W_LORA = 32
RWKV_A_LORA = 32
RWKV_G_LORA = 64
RWKV_LN_EPS = 64e-5
RET_THETA = 10000.0
GDN_CONV = 4
MOBA_BLOCK = 256
MOBA_TOPK = 3
MOBA_QBLOCK = 128
PAGE_SIZE = 128
CHUNK = 64
INV_BLOCK = 16
LORA_W = RWKV_W_LORA + RWKV_A_LORA + RWKV_G_LORA
A_COLS = 3 * GROUP_WIDTH + LORA_W
C_QKV = 3 * GROUP_WIDTH
NEG = -1e30
VMEM_LIMIT = 56 * 1024 * 1024


def _cparams(*sem):
    return pltpu.CompilerParams(dimension_semantics=sem, vmem_limit_bytes=VMEM_LIMIT)


def _dot(a, b, precision=None):
    return jnp.dot(a, b, preferred_element_type=f32, precision=precision)


def _dot_nt(a, b, precision=None):
    return lax.dot_general(a, b, (((1,), (1,)), ((), ())), preferred_element_type=f32, precision=precision)


def _dot_tn(a, b, precision=None):
    return lax.dot_general(a, b, (((0,), (0,)), ((), ())), preferred_element_type=f32, precision=precision)


def _rms(x, g):
    return x * lax.rsqrt(jnp.mean(x * x, axis=-1, keepdims=True) + NORM_EPS) * g


def _silu(x):
    return x * jax.nn.sigmoid(x)


def _softplus(x):
    return jnp.maximum(x, 0.0) + jnp.log(1.0 + jnp.exp(-jnp.abs(x)))


def _iota2(shape, axis):
    return lax.broadcasted_iota(jnp.int32, shape, axis)


def _head_seg():
    return (_iota2((GROUP_WIDTH, GROUP_WIDTH), 0) // HEAD_DIM == _iota2((GROUP_WIDTH, GROUP_WIDTH), 1) // HEAD_DIM).astype(f32)


def _segsum(x, seg):
    return _dot(x, seg, precision=HI)


def _eye(n):
    return (_iota2((n, n), 0) == _iota2((n, n), 1)).astype(f32)


def _nilpotent_inverse(n_mat, eye, index):
    t = eye + n_mat
    p = n_mat
    for _ in range(int(math.log2(index)) - 1):
        p = _dot(p, p, precision=HI)
        t = t + _dot(p, t, precision=HI)
    return t


def _unit_lower_inverse(n_mat, c):
    eye = _eye(c)
    same = _iota2((c, c), 0) // INV_BLOCK == _iota2((c, c), 1) // INV_BLOCK
    t_diag = _nilpotent_inverse(jnp.where(same, n_mat, 0.0), eye, INV_BLOCK)
    p = _dot(t_diag, jnp.where(same, 0.0, n_mat), precision=HI)
    return _dot(_nilpotent_inverse(p, eye, c // INV_BLOCK), t_diag, precision=HI)


def _mod_body(c_ref, w_ref, b_ref, o_ref):
    s = _silu(c_ref[...]).astype(bf16)
    o_ref[0] = _dot(s, w_ref[0].astype(bf16)) + b_ref[0]


def _modulation(c_all, w_mod, b_mod):
    depth, d, n9 = w_mod.shape
    rows = c_all.shape[0]
    tn = 1152
    return pl.pallas_call(
        _mod_body,
        grid=(depth, n9 // tn),
        in_specs=[pl.BlockSpec((rows, d), lambda l, j: (0, 0)),
                  pl.BlockSpec((1, d, tn), lambda l, j: (l, 0, j)),
                  pl.BlockSpec((1, 1, tn), lambda l, j: (l, 0, j))],
        out_specs=pl.BlockSpec((1, rows, tn), lambda l, j: (l, 0, j)),
        out_shape=jax.ShapeDtypeStruct((depth, rows, n9), f32),
        compiler_params=_cparams("arbitrary", "arbitrary"),
        name="modulation",
    )(c_all, w_mod, b_mod.reshape(depth, 1, n9))


def _ffn_body(x_ref, sh_ref, sc_ref, g_ref, npre_ref, npost_ref, wg_ref, wu_ref, wd_ref, o_ref, h_scr, acc_scr, *, nf):
    f = pl.program_id(2)

    @pl.when(f == 0)
    def _():
        h = _rms(x_ref[0], npre_ref[...]) * (1.0 + sc_ref[0]) + sh_ref[0]
        h_scr[...] = h.astype(bf16)

    h = h_scr[...]
    act = (_silu(_dot(h, wg_ref[...])) * _dot(h, wu_ref[...])).astype(bf16)
    part = _dot(act, wd_ref[...])

    @pl.when(f == 0)
    def _():
        acc_scr[...] = part

    @pl.when(f > 0)
    def _():
        acc_scr[...] += part

    @pl.when(f == nf - 1)
    def _():
        o_ref[0] = x_ref[0] + FFN_RES * g_ref[0] * _rms(acc_scr[...], npost_ref[...])


def _ffn(x, sh, sc, g, npre, npost, wg, wu, wd, l, i, tm, tf):
    b, seq, d = x.shape
    r = sh.shape[1]
    dff = wg.shape[-1]
    nf = dff // tf
    mod_spec = pl.BlockSpec((1, r, d), lambda bi, t, f: (bi, 0 if r == 1 else t, 0))
    vec_spec = pl.BlockSpec((1, d), lambda bi, t, f: (0, 0))
    return pl.pallas_call(
        functools.partial(_ffn_body, nf=nf),
        grid=(b, seq // tm, nf),
        in_specs=[pl.BlockSpec((1, tm, d), lambda bi, t, f: (bi, t, 0)),
                  mod_spec, mod_spec, mod_spec, vec_spec, vec_spec,
                  pl.BlockSpec((None, None, d, tf), lambda bi, t, f: (l, i, 0, f)),
                  pl.BlockSpec((None, None, d, tf), lambda bi, t, f: (l, i, 0, f)),
                  pl.BlockSpec((None, None, tf, d), lambda bi, t, f: (l, i, f, 0))],
        out_specs=pl.BlockSpec((1, tm, d), lambda bi, t, f: (bi, t, 0)),
        out_shape=jax.ShapeDtypeStruct(x.shape, f32),
        scratch_shapes=[pltpu.VMEM((tm, d), bf16), pltpu.VMEM((tm, d), f32)],
        compiler_params=_cparams("arbitrary", "arbitrary", "arbitrary"),
        name="ffn_halfstep",
    )(x, sh, sc, g, npre, npost, wg, wu, wd)


def _inproj_body(x_ref, sh_ref, sc_ref, npre_ref, *refs):
    n = len(refs) // 2
    h = (_rms(x_ref[0], npre_ref[...]) * (1.0 + sc_ref[0]) + sh_ref[0]).astype(bf16)
    for w_ref, o_ref in zip(refs[:n], refs[n:]):
        o_ref[0] = _dot(h, w_ref[...])


def _inproj(x, sh, sc, npre, ws, tm):
    b, seq, d = x.shape
    r = sh.shape[1]
    mod_spec = pl.BlockSpec((1, r, d), lambda bi, t: (bi, 0 if r == 1 else t, 0))
    return pl.pallas_call(
        _inproj_body,
        grid=(b, seq // tm),
        in_specs=[pl.BlockSpec((1, tm, d), lambda bi, t: (bi, t, 0)), mod_spec, mod_spec,
                  pl.BlockSpec((1, d), lambda bi, t: (0, 0))]
                 + [pl.BlockSpec(w.shape, lambda bi, t: (0, 0)) for w in ws],
        out_specs=[pl.BlockSpec((1, tm, w.shape[1]), lambda bi, t: (bi, t, 0)) for w in ws],
        out_shape=[jax.ShapeDtypeStruct((b, seq, w.shape[1]), f32) for w in ws],
        compiler_params=_cparams("arbitrary", "arbitrary"),
        name="mixer_inproj",
    )(x, sh, sc, npre, *ws)


def _outproj_body(x_ref, ya_ref, yb_ref, yc_ref, yd_ref, g_ref, npost_ref, w_ref, o_ref):
    y = jnp.concatenate([ya_ref[0], yb_ref[0], yc_ref[0], yd_ref[0]], axis=-1).astype(bf16)
    o_ref[0] = x_ref[0] + g_ref[0] * _rms(_dot(y, w_ref[...]), npost_ref[...])


def _outproj(x, ys, g, npost, w, tm):
    b, seq, d = x.shape
    r = g.shape[1]
    y_spec = pl.BlockSpec((1, tm, GROUP_WIDTH), lambda bi, t: (bi, t, 0))
    return pl.pallas_call(
        _outproj_body,
        grid=(b, seq // tm),
        in_specs=[pl.BlockSpec((1, tm, d), lambda bi, t: (bi, t, 0)), y_spec, y_spec, y_spec, y_spec,
                  pl.BlockSpec((1, r, d), lambda bi, t: (bi, 0 if r == 1 else t, 0)),
                  pl.BlockSpec((1, d), lambda bi, t: (0, 0)),
                  pl.BlockSpec(w.shape, lambda bi, t: (0, 0))],
        out_specs=pl.BlockSpec((1, tm, d), lambda bi, t: (bi, t, 0)),
        out_shape=jax.ShapeDtypeStruct(x.shape, f32),
        compiler_params=_cparams("arbitrary", "arbitrary"),
        name="mixer_outproj",
    )(x, *ys, g, npost, w)


def _rwkv_pre(p, prev, mu, w0, w2p, a0, a2p, g2p, k_k, k_a, seg):
    xs = p + (prev - p) * mu
    r = xs[:, 0:GROUP_WIDTH]
    k = xs[:, GROUP_WIDTH:2 * GROUP_WIDTH]
    v = xs[:, 2 * GROUP_WIDTH:3 * GROUP_WIDTH]
    lo = xs[:, 3 * GROUP_WIDTH:]
    w_log = -_softplus(-(w0 + _dot(jnp.tanh(lo), w2p))) - 0.5
    logw = -jnp.exp(w_log)
    a = jax.nn.sigmoid(a0 + _dot(lo, a2p))
    g = _dot(jax.nn.sigmoid(lo), g2p)
    kk = k * k_k
    kk = kk * lax.rsqrt(_segsum(kk * kk, seg) + 1e-6)
    k2 = k * (1.0 + (a - 1.0) * k_a)
    return r, logw, k2, v, kk, a, g


def _rwkv_post(y, r, k2, v, g, r_k, ln_w, ln_b, seg):
    mean = _segsum(y, seg) * (1.0 / HEAD_DIM)
    d = y - mean
    var = _segsum(d * d, seg) * (1.0 / HEAD_DIM)
    yn = d * lax.rsqrt(var + RWKV_LN_EPS) * ln_w + ln_b
    bonus = _segsum(r * k2 * r_k, seg) * v
    return (yn + bonus) * g


def _rwkv_body(p_ref, mu_ref, w0_ref, w2_ref, a0_ref, a2_ref, g2_ref, kk_ref, ka_ref, rk_ref, lnw_ref, lnb_ref,
               y_ref, s_out_ref, s_scr, carry_scr, *, tt):
    c = CHUNK
    t = pl.program_id(1)

    @pl.when(t == 0)
    def _():
        s_scr[...] = jnp.zeros_like(s_scr)
        carry_scr[...] = jnp.zeros_like(carry_scr)

    seg = _head_seg()
    p = p_ref[0]
    row = _iota2((tt, 1), 0)
    prev = jnp.where(row == 0, carry_scr[7:8, :], pltpu.roll(p, 1, axis=0))
    carry_scr[...] = p[tt - 8:tt, :]
    r, logw, k2, v, kk, a, g = _rwkv_pre(p, prev, mu_ref[...], w0_ref[...], w2_ref[...], a0_ref[...], a2_ref[...],
                                         g2_ref[...], kk_ref[...], ka_ref[...], seg)
    kka = kk * a
    ri, ci = _iota2((c, c), 0), _iota2((c, c), 1)
    incl, strict = ri >= ci, ri > ci
    lmat = incl.astype(f32)

    y_chunks = []
    for ch in range(tt // c):
        sl = slice(ch * c, (ch + 1) * c)
        lw = logw[sl]
        gcum = _dot(lmat, lw, precision=HI)
        g_last = gcum[c - 1:c, :]
        e_g = jnp.exp(gcum)
        e_gx = jnp.exp(gcum - lw)
        e_ng = jnp.exp(-gcum)
        e_rem = jnp.exp(g_last - gcum)
        e_last = jnp.exp(g_last)
        rt = r[sl] * e_g
        at = -kk[sl] * e_gx
        kt = k2[sl] * e_ng
        bt = kka[sl] * e_ng
        kh = k2[sl] * e_rem
        bh = kka[sl] * e_rem
        y_heads = []
        for h in range(N_HEADS):
            hs = slice(h * HEAD_DIM, (h + 1) * HEAD_DIM)
            amat = _dot_nt(jnp.concatenate([at[:, hs], rt[:, hs]], axis=0),
                           jnp.concatenate([kt[:, hs], bt[:, hs]], axis=0))
            a_ak = jnp.where(strict, amat[:c, :c], 0.0)
            a_ab = jnp.where(strict, amat[:c, c:], 0.0)
            a_rk = jnp.where(incl, amat[c:, :c], 0.0)
            a_rb = jnp.where(incl, amat[c:, c:], 0.0)
            tmat = _unit_lower_inverse(a_ab, c)
            vh = v[sl, hs]
            tx = _dot(tmat, jnp.concatenate([at[:, hs], _dot(a_ak, vh)], axis=1))
            s_old = s_scr[h]
            u = _dot_nt(tx[:, :HEAD_DIM], s_old) + tx[:, HEAD_DIM:]
            y_heads.append(_dot_nt(rt[:, hs], s_old) + _dot(a_rk, vh) + _dot(a_rb, u))
            s_scr[h] = s_old * e_last[:, hs] + _dot_tn(jnp.concatenate([vh, u], axis=0),
                                                       jnp.concatenate([kh[:, hs], bh[:, hs]], axis=0))
        y_chunks.append(jnp.concatenate(y_heads, axis=1))
    y = jnp.concatenate(y_chunks, axis=0)
    y_ref[0] = _rwkv_post(y, r, k2, v, g, rk_ref[...], lnw_ref[...], lnb_ref[...], seg)
    s_out_ref[0] = s_scr[...]


def _rwkv_prompt(pa, prm, tt):
    b, seq, _ = pa.shape
    vec = lambda a: pl.BlockSpec(a.shape, lambda bi, t: (0, 0))
    return pl.pallas_call(
        functools.partial(_rwkv_body, tt=tt),
        grid=(b, seq // tt),
        in_specs=[pl.BlockSpec((1, tt, A_COLS), lambda bi, t: (bi, t, 0))] + [vec(a) for a in prm],
        out_specs=[pl.BlockSpec((1, tt, GROUP_WIDTH), lambda bi, t: (bi, t, 0)),
                   pl.BlockSpec((1, N_HEADS, HEAD_DIM, HEAD_DIM), lambda bi, t: (bi, 0, 0, 0))],
        out_shape=[jax.ShapeDtypeStruct((b, seq, GROUP_WIDTH), f32),
                   jax.ShapeDtypeStruct((b, N_HEADS, HEAD_DIM, HEAD_DIM), f32)],
        scratch_shapes=[pltpu.VMEM((N_HEADS, HEAD_DIM, HEAD_DIM), f32), pltpu.VMEM((8, A_COLS), f32)],
        compiler_params=_cparams("arbitrary", "arbitrary"),
        name="rwkv7_prompt",
    )(pa, *prm)


def _xpos(x, cos, sin_signed):
    n = x.shape[-1]
    even = (_iota2(x.shape, 1) % 2) == 0
    partner = jnp.where(even, pltpu.roll(x, n - 1, axis=1), pltpu.roll(x, 1, axis=1))
    return x * cos + partner * sin_signed


def _ret_body(p_ref, cos_ref, sin_ref, dmask_ref, eg_ref, erem_ref, elast_ref, y_ref, s_out_ref, s_scr, *, tt):
    c = CHUNK
    t = pl.program_id(1)

    @pl.when(t == 0)
    def _():
        s_scr[...] = jnp.zeros_like(s_scr)

    seg = _head_seg()
    p = p_ref[0]
    q = _xpos(p[:, 0:GROUP_WIDTH], cos_ref[...], sin_ref[...])
    k = _xpos(p[:, GROUP_WIDTH:2 * GROUP_WIDTH], cos_ref[...], sin_ref[...]) * HEAD_DIM ** -0.5
    v = p[:, 2 * GROUP_WIDTH:3 * GROUP_WIDTH]
    gt = p[:, 3 * GROUP_WIDTH:]
    e_last = elast_ref[...]
    o_chunks = []
    for ch in range(tt // c):
        sl = slice(ch * c, (ch + 1) * c)
        qg = q[sl] * eg_ref[...]
        kr = k[sl] * erem_ref[...]
        o_heads = []
        for h in range(N_HEADS):
            hs = slice(h * HEAD_DIM, (h + 1) * HEAD_DIM)
            s_old = s_scr[h]
            vh = v[sl, hs]
            intra = _dot(_dot_nt(q[sl, hs], k[sl, hs]) * dmask_ref[h], vh)
            o_heads.append(intra + _dot(qg[:, hs], s_old))
            s_scr[h] = s_old * e_last[:, hs] + _dot_tn(kr[:, hs], vh)
        o_chunks.append(jnp.concatenate(o_heads, axis=1))
    o = jnp.concatenate(o_chunks, axis=0)
    o = o * lax.rsqrt(_segsum(o * o, seg) * (1.0 / HEAD_DIM) + NORM_EPS)
    y_ref[0] = _silu(gt) * o
    s_out_ref[0] = s_scr[...]


def _ret_tables(seq, pos0):
    inv = 1.0 / (RET_THETA ** jnp.linspace(0.0, 1.0, HEAD_DIM // 2, dtype=f32))
    ang = (pos0 + jnp.arange(seq)).astype(f32)[:, None] * inv[None, :]
    cos = jnp.tile(jnp.repeat(jnp.cos(ang), 2, axis=1), (1, N_HEADS))
    sgn = jnp.tile(jnp.array([-1.0, 1.0], f32), HEAD_DIM // 2)
    sin = jnp.tile(jnp.repeat(jnp.sin(ang), 2, axis=1) * sgn[None, :], (1, N_HEADS))
    lg = np.log(1.0 - 2.0 ** (-5.0 - np.arange(N_HEADS, dtype=np.float64)))
    i = np.arange(CHUNK)
    diff = i[:, None] - i[None, :]
    dmask = np.where(diff[None] >= 0, np.exp(lg[:, None, None] * np.maximum(diff, 0)[None]), 0.0)
    lane = np.repeat(lg, HEAD_DIM)[None, :]
    e_g = np.exp((i[:, None] + 1) * lane)
    e_rem = np.exp((CHUNK - 1 - i[:, None]) * lane)
    e_last = np.exp(CHUNK * lane)
    e_one = np.exp(lane)
    return cos, sin, tuple(jnp.asarray(a, f32) for a in (dmask, e_g, e_rem, e_last, e_one))


def _ret_prompt(pb, cos, sin, tabs, tt):
    b, seq, _ = pb.shape
    dmask, e_g, e_rem, e_last, _ = tabs
    full = lambda a: pl.BlockSpec(a.shape, lambda bi, t: (0,) * a.ndim)
    return pl.pallas_call(
        functools.partial(_ret_body, tt=tt),
        grid=(b, seq // tt),
        in_specs=[pl.BlockSpec((1, tt, 4 * GROUP_WIDTH), lambda bi, t: (bi, t, 0)),
                  pl.BlockSpec((tt, GROUP_WIDTH), lambda bi, t: (t, 0)),
                  pl.BlockSpec((tt, GROUP_WIDTH), lambda bi, t: (t, 0)),
                  full(dmask), full(e_g), full(e_rem), full(e_last)],
        out_specs=[pl.BlockSpec((1, tt, GROUP_WIDTH), lambda bi, t: (bi, t, 0)),
                   pl.BlockSpec((1, N_HEADS, HEAD_DIM, HEAD_DIM), lambda bi, t: (bi, 0, 0, 0))],
        out_shape=[jax.ShapeDtypeStruct((b, seq, GROUP_WIDTH), f32),
                   jax.ShapeDtypeStruct((b, N_HEADS, HEAD_DIM, HEAD_DIM), f32)],
        scratch_shapes=[pltpu.VMEM((N_HEADS, HEAD_DIM, HEAD_DIM), f32)],
        compiler_params=_cparams("arbitrary", "arbitrary"),
        name="retention_prompt",
    )(pb, cos, sin, dmask, e_g, e_rem, e_last)


def _gdn_scalars(pab, alog_p, dtb_p, sel_a, sel_b):
    logd = -jnp.exp(alog_p) * _softplus(pab + dtb_p)
    beta = jax.nn.sigmoid(pab)
    return _dot(logd, sel_a, precision=HI), _dot(beta, sel_b, precision=HI)


def _gdn_qkv(conv, seg):
    cs = _silu(conv)
    q = cs[:, 0:GROUP_WIDTH]
    k = cs[:, GROUP_WIDTH:2 * GROUP_WIDTH]
    v = cs[:, 2 * GROUP_WIDTH:]
    q = q * lax.rsqrt(_segsum(q * q, seg) + 1e-6) * HEAD_DIM ** -0.5
    k = k * lax.rsqrt(_segsum(k * k, seg) + 1e-6)
    return q, k, v


def _gdn_body(pc_ref, pab_ref, cw_ref, alog_ref, dtb_ref, sela_ref, selb_ref, nw_ref,
              y_ref, s_out_ref, s_scr, ext_scr, *, tt):
    c = CHUNK
    t = pl.program_id(1)

    @pl.when(t == 0)
    def _():
        s_scr[...] = jnp.zeros_like(s_scr)
        ext_scr[pl.ds(0, 8), :] = jnp.zeros((8, C_QKV), f32)

    seg = _head_seg()
    pc = pc_ref[0]
    gate = pc[:, C_QKV:]
    ext_scr[pl.ds(8, tt), :] = pc[:, :C_QKV]
    cw = cw_ref[...]
    conv = sum(ext_scr[pl.ds(8 - (GDN_CONV - 1) + j, tt), :] * cw[j:j + 1, :] for j in range(GDN_CONV))
    ext_scr[pl.ds(0, 8), :] = pc[tt - 8:tt, :C_QKV]
    q, k, v = _gdn_qkv(conv, seg)
    logd, beta = _gdn_scalars(pab_ref[0], alog_ref[...], dtb_ref[...], sela_ref[...], selb_ref[...])
    kb = k * beta
    vb = v * beta

    ri, ci = _iota2((c, c), 0), _iota2((c, c), 1)
    incl, strict = ri >= ci, ri > ci
    lmat = incl.astype(f32)
    ones = jnp.ones((c, c), f32)
    eye4 = (_iota2((c, GROUP_WIDTH), 0) == _iota2((c, GROUP_WIDTH), 1) % HEAD_DIM).astype(f32)
    incl4 = _iota2((c, GROUP_WIDTH), 0) >= _iota2((c, GROUP_WIDTH), 1) % HEAD_DIM

    o_chunks = []
    for ch in range(tt // c):
        sl = slice(ch * c, (ch + 1) * c)
        gcum = _dot(lmat, logd[sl], precision=HI)
        gcum_t = _dot(ones, gcum * eye4, precision=HI)
        g_last = gcum[c - 1:c, :]
        e_g = jnp.exp(gcum)
        e_rem = jnp.exp(g_last - gcum)
        e_last = jnp.exp(g_last)
        dmask = jnp.exp(jnp.where(incl4, gcum - gcum_t, -jnp.inf))
        qg = q[sl] * e_g
        kbg = kb[sl] * e_g
        kr = k[sl] * e_rem
        o_heads = []
        for h in range(N_HEADS):
            hs = slice(h * HEAD_DIM, (h + 1) * HEAD_DIM)
            kq = _dot_nt(jnp.concatenate([kb[sl, hs], q[sl, hs]], axis=0), k[sl, hs])
            dm = dmask[:, hs]
            m = jnp.where(strict, kq[:c] * dm, 0.0)
            aqk = kq[c:] * dm
            tmat = _unit_lower_inverse(-m, c)
            uw = _dot(tmat, jnp.concatenate([vb[sl, hs], kbg[:, hs]], axis=1))
            s_old = s_scr[h]
            vnew = uw[:, :HEAD_DIM] - _dot(uw[:, HEAD_DIM:], s_old)
            o_heads.append(_dot(qg[:, hs], s_old) + _dot(aqk, vnew))
            s_scr[h] = s_old * e_last[:, hs] + _dot_tn(kr[:, hs], vnew)
        o_chunks.append(jnp.concatenate(o_heads, axis=1))
    o = jnp.concatenate(o_chunks, axis=0)
    o = o * lax.rsqrt(_segsum(o * o, seg) * (1.0 / HEAD_DIM) + NORM_EPS) * nw_ref[...] * _silu(gate)
    y_ref[0] = o
    s_out_ref[0] = s_scr[...]


def _gdn_prompt(pc, pab, prm, tt):
    b, seq, _ = pc.shape
    vec = lambda a: pl.BlockSpec(a.shape, lambda bi, t: (0, 0))
    return pl.pallas_call(
        functools.partial(_gdn_body, tt=tt),
        grid=(b, seq // tt),
        in_specs=[pl.BlockSpec((1, tt, 4 * GROUP_WIDTH), lambda bi, t: (bi, t, 0)),
                  pl.BlockSpec((1, tt, 128), lambda bi, t: (bi, t, 0))] + [vec(a) for a in prm],
        out_specs=[pl.BlockSpec((1, tt, GROUP_WIDTH), lambda bi, t: (bi, t, 0)),
                   pl.BlockSpec((1, N_HEADS, HEAD_DIM, HEAD_DIM), lambda bi, t: (bi, 0, 0, 0))],
        out_shape=[jax.ShapeDtypeStruct((b, seq, GROUP_WIDTH), f32),
                   jax.ShapeDtypeStruct((b, N_HEADS, HEAD_DIM, HEAD_DIM), f32)],
        scratch_shapes=[pltpu.VMEM((N_HEADS, HEAD_DIM, HEAD_DIM), f32), pltpu.VMEM((tt + 8, C_QKV), f32)],
        compiler_params=_cparams("arbitrary", "arbitrary"),
        name="gdn_prompt",
    )(pc, pab, *prm)


def _block_means_body(k_ref, o_ref, *, nb):
    k = k_ref[0]
    o_ref[0] = jnp.sum(k.reshape(nb, MOBA_BLOCK, GROUP_WIDTH), axis=1) * (1.0 / MOBA_BLOCK)


def _block_means(k):
    b, seq, _ = k.shape
    nb = seq // MOBA_BLOCK
    return pl.pallas_call(
        functools.partial(_block_means_body, nb=nb),
        grid=(b,),
        in_specs=[pl.BlockSpec((1, seq, GROUP_WIDTH), lambda bi: (bi, 0, 0))],
        out_specs=pl.BlockSpec((1, nb, GROUP_WIDTH), lambda bi: (bi, 0, 0)),
        out_shape=jax.ShapeDtypeStruct((b, nb, GROUP_WIDTH), f32),
        compiler_params=_cparams("arbitrary"),
        name="moba_block_means",
    )(k)


def _top_blocks(gate, past, nb):
    blk = _iota2(gate.shape, 1).astype(f32)
    g = jnp.where(past, gate, -jnp.inf)
    picks = []
    for _ in range(MOBA_TOPK):
        m = jnp.max(g, axis=1, keepdims=True)
        idx = jnp.min(jnp.where(g == m, blk, float(nb)), axis=1, keepdims=True)
        picks.append(jnp.where(m > -jnp.inf, idx, -1.0))
        g = jnp.where(blk == idx, -jnp.inf, g)
    return picks


def _moba_body(q_ref, k_ref, v_ref, mean_ref, o_ref, *, nb):
    i = pl.program_id(1)
    tq = MOBA_QBLOCK
    own = (i * tq) // MOBA_BLOCK
    qs = q_ref[0] * HEAD_DIM ** -0.5
    qb = qs.astype(bf16)
    means = mean_ref[0]
    blk = _iota2((tq, nb), 1)
    past = blk < own
    hsl = [slice(h * HEAD_DIM, (h + 1) * HEAD_DIM) for h in range(N_HEADS)]
    picks = [_top_blocks(_dot_nt(qs[:, hs], means[:, hs], precision=HI), past, nb) for hs in hsl]

    k_own = k_ref[0, pl.ds(pl.multiple_of(own * MOBA_BLOCK, MOBA_BLOCK), MOBA_BLOCK), :]
    v_own = v_ref[0, pl.ds(pl.multiple_of(own * MOBA_BLOCK, MOBA_BLOCK), MOBA_BLOCK), :]
    qpos = i * tq + _iota2((tq, MOBA_BLOCK), 0)
    kpos = own * MOBA_BLOCK + _iota2((tq, MOBA_BLOCK), 1)
    causal = kpos <= qpos
    carry = []
    for hs in hsl:
        s = jnp.where(causal, _dot_nt(qb[:, hs], k_own[:, hs]), NEG)
        m = jnp.max(s, axis=1, keepdims=True)
        p = jnp.exp(s - m)
        carry += [m, jnp.sum(p, axis=1, keepdims=True), _dot(p.astype(bf16), v_own[:, hs])]

    def step(j, carry):
        start = pl.multiple_of(j * MOBA_BLOCK, MOBA_BLOCK)
        kj = k_ref[0, pl.ds(start, MOBA_BLOCK), :]
        vj = v_ref[0, pl.ds(start, MOBA_BLOCK), :]
        jf = j.astype(f32)
        out = []
        for h, hs in enumerate(hsl):
            m_old, l_old, acc = carry[3 * h:3 * h + 3]
            chosen = functools.reduce(jnp.logical_or, [pk == jf for pk in picks[h]])
            s = jnp.where(chosen, _dot_nt(qb[:, hs], kj[:, hs]), NEG)
            m_new = jnp.maximum(m_old, jnp.max(s, axis=1, keepdims=True))
            alpha = jnp.exp(m_old - m_new)
            p = jnp.exp(s - m_new)
            out += [m_new, alpha * l_old + jnp.sum(p, axis=1, keepdims=True),
                    alpha * acc + _dot(p.astype(bf16), vj[:, hs])]
        return tuple(out)

    carry = lax.fori_loop(0, own, step, tuple(carry))
    o_ref[0] = jnp.concatenate([carry[3 * h + 2] / carry[3 * h + 1] for h in range(N_HEADS)], axis=1)


def _moba_prompt(q, k, v):
    b, seq, _ = q.shape
    nb = seq // MOBA_BLOCK
    assert seq % MOBA_BLOCK == 0 and nb <= 32
    means = _block_means(k)
    kv_spec = pl.BlockSpec((1, seq, GROUP_WIDTH), lambda bi, i: (bi, 0, 0))
    return pl.pallas_call(
        functools.partial(_moba_body, nb=nb),
        grid=(b, seq // MOBA_QBLOCK),
        in_specs=[pl.BlockSpec((1, MOBA_QBLOCK, GROUP_WIDTH), lambda bi, i: (bi, i, 0)), kv_spec, kv_spec,
                  pl.BlockSpec((1, nb, GROUP_WIDTH), lambda bi, i: (bi, 0, 0))],
        out_specs=pl.BlockSpec((1, MOBA_QBLOCK, GROUP_WIDTH), lambda bi, i: (bi, i, 0)),
        out_shape=jax.ShapeDtypeStruct((b, seq, GROUP_WIDTH), f32),
        compiler_params=_cparams("arbitrary", "arbitrary"),
        name="moba_prompt",
    )(q, k.astype(bf16), v.astype(bf16), means)


def _row_to_col(row, eye):
    return jnp.sum(eye * row, axis=1, keepdims=True)


def _col_to_row(col, eye):
    return jnp.sum(eye * col, axis=0, keepdims=True)


def _sample_mix_body(pa_ref, shift_ref, pb_ref, pc_ref, pab_ref, conv_ref, srw_ref, srt_ref, sgd_ref,
                     mu_ref, w0_ref, w2_ref, a0_ref, a2_ref, g2_ref, kk_ref, ka_ref, rk_ref, lnw_ref, lnb_ref,
                     cos_ref, sin_ref, eone_ref, cw_ref, alog_ref, dtb_ref, sela_ref, selb_ref, nw_ref,
                     ya_ref, yb_ref, yc_ref, srw_out, srt_out, sgd_out):
    seg = _head_seg()
    eye = _eye(HEAD_DIM)
    hsl = [slice(h * HEAD_DIM, (h + 1) * HEAD_DIM) for h in range(N_HEADS)]

    r, logw, k2, v, kk, a, g = _rwkv_pre(pa_ref[0], shift_ref[0], mu_ref[...], w0_ref[...], w2_ref[...], a0_ref[...],
                                         a2_ref[...], g2_ref[...], kk_ref[...], ka_ref[...], seg)
    w = jnp.exp(logw)
    kka = kk * a
    ys = []
    for h, hs in enumerate(hsl):
        s_old = srw_ref[0, h]
        sa = jnp.sum(s_old * (-kk[:, hs]), axis=1, keepdims=True)
        s_new = s_old * w[:, hs] + sa * kka[:, hs] + _row_to_col(v[:, hs], eye) * k2[:, hs]
        srw_out[0, h] = s_new
        ys.append(_col_to_row(jnp.sum(s_new * r[:, hs], axis=1, keepdims=True), eye))
    ya_ref[0] = _rwkv_post(jnp.concatenate(ys, axis=1), r, k2, v, g, rk_ref[...], lnw_ref[...], lnb_ref[...], seg)

    pb = pb_ref[0]
    q = _xpos(pb[:, 0:GROUP_WIDTH], cos_ref[...], sin_ref[...])
    k = _xpos(pb[:, GROUP_WIDTH:2 * GROUP_WIDTH], cos_ref[...], sin_ref[...]) * HEAD_DIM ** -0.5
    v = pb[:, 2 * GROUP_WIDTH:3 * GROUP_WIDTH]
    e_one = eone_ref[...]
    os_ = []
    for h, hs in enumerate(hsl):
        s_new = srt_ref[0, h] * e_one[:, hs] + _row_to_col(k[:, hs], eye) * v[:, hs]
        srt_out[0, h] = s_new
        os_.append(jnp.sum(_row_to_col(q[:, hs], eye) * s_new, axis=0, keepdims=True))
    o = jnp.concatenate(os_, axis=1)
    o = o * lax.rsqrt(_segsum(o * o, seg) * (1.0 / HEAD_DIM) + NORM_EPS)
    yb_ref[0] = _silu(pb[:, 3 * GROUP_WIDTH:]) * o

    pc = pc_ref[0]
    cw = cw_ref[...]
    prev = conv_ref[0]
    conv = pc[:, :C_QKV] * cw[GDN_CONV - 1:GDN_CONV, :]
    for j in range(GDN_CONV - 1):
        conv = conv + prev[j:j + 1, :] * cw[j:j + 1, :]
    q, k, v = _gdn_qkv(conv, seg)
    logd, beta = _gdn_scalars(pab_ref[0], alog_ref[...], dtb_ref[...], sela_ref[...], selb_ref[...])
    alpha = jnp.exp(logd)
    os_ = []
    for h, hs in enumerate(hsl):
        s_old = sgd_ref[0, h]
        k_col = _row_to_col(k[:, hs], eye)
        ks = jnp.sum(k_col * s_old, axis=0, keepdims=True)
        s_new = alpha[:, hs] * s_old + k_col * (beta[:, hs] * (v[:, hs] - alpha[:, hs] * ks))
        sgd_out[0, h] = s_new
        os_.append(jnp.sum(_row_to_col(q[:, hs], eye) * s_new, axis=0, keepdims=True))
    o = jnp.concatenate(os_, axis=1)
    yc_ref[0] = o * lax.rsqrt(_segsum(o * o, seg) * (1.0 / HEAD_DIM) + NORM_EPS) * nw_ref[...] * _silu(pc[:, C_QKV:])


def _sample_mixers(pa, shift, pb, pc, pab, conv_prev, s_rwkv, s_ret, s_gdn, rwkv_prm, ret_prm, gdn_prm):
    nseq = pa.shape[0]
    row = lambda a: pl.BlockSpec((1, 1, a.shape[-1]), lambda b: (b, 0, 0))
    state = pl.BlockSpec((1, N_HEADS, HEAD_DIM, HEAD_DIM), lambda b: (b, 0, 0, 0))
    full = lambda a: pl.BlockSpec(a.shape, lambda b: (0,) * a.ndim)
    r3 = lambda a: a.reshape(nseq, 1, a.shape[-1])
    prm = tuple(rwkv_prm) + tuple(ret_prm) + tuple(gdn_prm)
    outs = pl.pallas_call(
        _sample_mix_body,
        grid=(nseq,),
        in_specs=[row(pa), row(shift), row(pb), row(pc), row(pab),
                  pl.BlockSpec((1, GDN_CONV - 1, C_QKV), lambda b: (b, 0, 0)), state, state, state]
                 + [full(a) for a in prm],
        out_specs=[row(pa[..., :GROUP_WIDTH])] * 3 + [state] * 3,
        out_shape=[jax.ShapeDtypeStruct((nseq, 1, GROUP_WIDTH), f32)] * 3
                  + [jax.ShapeDtypeStruct((nseq, N_HEADS, HEAD_DIM, HEAD_DIM), f32)] * 3,
        compiler_params=_cparams("arbitrary"),
        name="sample_mixers",
    )(r3(pa), r3(shift), r3(pb), r3(pc), r3(pab), conv_prev, s_rwkv, s_ret, s_gdn, *prm)
    return outs


def _moba_sample_body(pt_ref, q_ref, kn_ref, vn_ref, k0_ref, k1_ref, v0_ref, v1_ref, o_ref,
                      gate_scr, m_scr, l_scr, acc_scr, *, nb):
    del pt_ref
    j = pl.program_id(1)
    hmask = (_iota2((8, GROUP_WIDTH), 1) // HEAD_DIM == _iota2((8, GROUP_WIDTH), 0)).astype(f32)
    q8 = hmask * (q_ref[0] * HEAD_DIM ** -0.5)
    kb = jnp.concatenate([k0_ref[...], k1_ref[...]], axis=0)
    vb = jnp.concatenate([v0_ref[...], v1_ref[...]], axis=0)
    mean = jnp.sum(kb, axis=0, keepdims=True) * (1.0 / MOBA_BLOCK)
    gate = jnp.sum(q8 * mean, axis=1, keepdims=True)
    s = _dot_nt(q8.astype(bf16), kb.astype(bf16))
    m = jnp.max(s, axis=1, keepdims=True)
    p = jnp.exp(s - m)
    gate_scr[j] = jnp.broadcast_to(gate, (8, 128))
    m_scr[j] = jnp.broadcast_to(m, (8, 128))
    l_scr[j] = jnp.broadcast_to(jnp.sum(p, axis=1, keepdims=True), (8, 128))
    acc_scr[j] = _dot(p.astype(bf16), vb.astype(bf16))

    @pl.when(j == nb - 1)
    def _():
        g = gate_scr[...]
        blk = _iota2(g.shape, 0).astype(f32)
        sel = jnp.zeros(g.shape, jnp.bool_)
        for _ in range(min(MOBA_TOPK, nb)):
            gm = jnp.max(g, axis=0, keepdims=True)
            idx = jnp.min(jnp.where(g == gm, blk, float(nb)), axis=0, keepdims=True)
            pick = blk == idx
            sel = jnp.logical_or(sel, pick)
            g = jnp.where(pick, -jnp.inf, g)
        s_own = jnp.broadcast_to(jnp.sum(q8 * kn_ref[0], axis=1, keepdims=True), (8, 128))
        mb = jnp.where(sel, m_scr[...], NEG)
        m_all = jnp.maximum(jnp.max(mb, axis=0), s_own)
        wgt = jnp.where(sel, jnp.exp(mb - m_all[None]), 0.0)
        w_own = jnp.exp(s_own - m_all)
        denom = jnp.sum(wgt * l_scr[...], axis=0) + w_own
        wide = lambda a: jnp.concatenate([a, a], axis=-1)
        num = jnp.sum(wide(wgt) * acc_scr[...], axis=0) + wide(w_own) * vn_ref[0]
        o_ref[0] = jnp.sum(hmask * (num / wide(denom)), axis=0, keepdims=True)


def _moba_sample(q, k_new, v_new, cache_k, cache_v, page_table, layer):
    nseq, n_pages = page_table.shape
    pages_per_block = MOBA_BLOCK // PAGE_SIZE
    assert pages_per_block == 2 and n_pages % pages_per_block == 0
    nb = n_pages // pages_per_block
    row = pl.BlockSpec((1, 1, GROUP_WIDTH), lambda b, j, pt: (b, 0, 0))
    page = lambda e: pl.BlockSpec((None, None, PAGE_SIZE, GROUP_WIDTH),
                                  lambda b, j, pt: (layer, pt[b, pages_per_block * j + e], 0, 0))
    r3 = lambda a: a.reshape(nseq, 1, GROUP_WIDTH)
    return pl.pallas_call(
        functools.partial(_moba_sample_body, nb=nb),
        grid_spec=pltpu.PrefetchScalarGridSpec(
            num_scalar_prefetch=1,
            grid=(nseq, nb),
            in_specs=[row, row, row, page(0), page(1), page(0), page(1)],
            out_specs=row,
            scratch_shapes=[pltpu.VMEM((nb, 8, 128), f32)] * 3 + [pltpu.VMEM((nb, 8, GROUP_WIDTH), f32)]),
        out_shape=jax.ShapeDtypeStruct((nseq, 1, GROUP_WIDTH), f32),
        compiler_params=_cparams("arbitrary", "arbitrary"),
        name="moba_sample",
    )(page_table, r3(q), r3(k_new), r3(v_new), cache_k, cache_k, cache_v, cache_v).reshape(nseq, GROUP_WIDTH)


def _pad_rows(w, start, total):
    return jnp.zeros((total, w.shape[1]), w.dtype).at[start:start + w.shape[0]].set(w)


def _mixer_params(l, rwkv_mu, rwkv_w0, rwkv_w2, rwkv_a0, rwkv_a2, rwkv_g2, rwkv_k_k, rwkv_k_a, rwkv_r_k, rwkv_ln_w,
                  rwkv_ln_b, gdn_conv_w, gdn_a_log, gdn_dt_bias, gdn_norm_w):
    row = lambda a: a.reshape(1, -1)
    rwkv = (row(rwkv_mu[l]), row(rwkv_w0[l]), _pad_rows(rwkv_w2[l], 0, LORA_W), row(rwkv_a0[l]),
            _pad_rows(rwkv_a2[l], RWKV_W_LORA, LORA_W), _pad_rows(rwkv_g2[l], RWKV_W_LORA + RWKV_A_LORA, LORA_W),
            row(rwkv_k_k[l]), row(rwkv_k_a[l]), row(rwkv_r_k[l]), row(rwkv_ln_w[l]), row(rwkv_ln_b[l]))
    lane = np.arange(128)[:, None]
    col_head = (np.arange(GROUP_WIDTH) // HEAD_DIM)[None, :]
    sel_a = jnp.asarray(lane == col_head, f32)
    sel_b = jnp.asarray(lane == col_head + N_HEADS, f32)
    pad128 = lambda a: jnp.zeros((1, 128), f32).at[0, :N_HEADS].set(a)
    gdn = (gdn_conv_w[l], pad128(gdn_a_log[l]), pad128(gdn_dt_bias[l]), sel_a, sel_b,
           jnp.tile(gdn_norm_w[l], N_HEADS).reshape(1, GROUP_WIDTH))
    return rwkv, gdn


def kernel(x_prompt, x_sample, c_prompt, c_sample, cache_k, cache_v, page_table, state_rwkv, state_rwkv_shift,
           state_ret, state_gdn, state_gdn_conv, w_mod, b_mod, norm_pre, norm_post, ffn_w_gate, ffn_w_up, ffn_w_down,
           w_in, w_out, rwkv_mu, rwkv_w0, rwkv_w2, rwkv_a0, rwkv_a2, rwkv_g2, rwkv_k_k, rwkv_k_a, rwkv_r_k, rwkv_ln_w,
           rwkv_ln_b, gdn_conv_w, gdn_a_log, gdn_dt_bias, gdn_norm_w):
    depth = w_in.shape[0]
    bp, seq, d = x_prompt.shape
    bs = x_sample.shape[0]
    assert x_sample.shape[1] == 1
    past_len = page_table.shape[1] * PAGE_SIZE
    assert past_len % MOBA_BLOCK == 0
    n_phys = cache_k.shape[1]
    cache_k = cache_k.reshape(depth, n_phys, PAGE_SIZE, GROUP_WIDTH)
    cache_v = cache_v.reshape(depth, n_phys, PAGE_SIZE, GROUP_WIDTH)

    n_c = bp + bs
    c_all = jnp.concatenate([c_prompt, c_sample, jnp.zeros((-n_c % 8, d), f32)], axis=0)
    mods = _modulation(c_all, w_mod, b_mod)

    wg, wu, wdn = ffn_w_gate.astype(bf16), ffn_w_up.astype(bf16), ffn_w_down.astype(bf16)
    w_in_b, w_out_b = w_in.astype(bf16), w_out.astype(bf16)
    a0, b0 = A_COLS, A_COLS + 4 * GROUP_WIDTH
    c0 = b0 + C_QKV + GROUP_WIDTH
    d0 = c0 + 2 * N_HEADS
    col_groups = [(0, a0), (a0, b0), (b0, c0), (d0, d0 + GROUP_WIDTH), (d0 + GROUP_WIDTH, d0 + 2 * GROUP_WIDTH),
                  (d0 + 2 * GROUP_WIDTH, d0 + 3 * GROUP_WIDTH)]

    cos_p, sin_p, ret_tabs = _ret_tables(seq, 0)
    cos_s, sin_s, _ = _ret_tables(1, past_len)
    tm = 512
    tt = 256

    xp, xs = x_prompt, x_sample.reshape(1, bs, d)
    news_p, news_s = [], []
    for l in range(depth):
        mp = [m.reshape(bp, 1, d) for m in jnp.split(mods[l, :bp], 9, axis=-1)]
        ms = [m.reshape(1, bs, d) for m in jnp.split(mods[l, bp:n_c], 9, axis=-1)]
        npre = [norm_pre[l, i].reshape(1, d) for i in range(3)]
        npost = [norm_post[l, i].reshape(1, d) for i in range(3)]
        ws = [w_in_b[l][:, lo:hi] for lo, hi in col_groups]
        ws.append(jnp.pad(w_in_b[l][:, c0:d0], ((0, 0), (0, 128 - 2 * N_HEADS))))
        rwkv_prm, gdn_prm = _mixer_params(l, rwkv_mu, rwkv_w0, rwkv_w2, rwkv_a0, rwkv_a2, rwkv_g2, rwkv_k_k, rwkv_k_a,
                                          rwkv_r_k, rwkv_ln_w, rwkv_ln_b, gdn_conv_w, gdn_a_log, gdn_dt_bias, gdn_norm_w)

        xp = _ffn(xp, mp[0], mp[1], mp[2], npre[0], npost[0], wg, wu, wdn, l, 0, tm, 256)
        pa, pb, pc, qd, kd, vd, pab = _inproj(xp, mp[3], mp[4], npre[1], ws, tm)
        ya, rw_new = _rwkv_prompt(pa, rwkv_prm, tt)
        yb, rt_new = _ret_prompt(pb, cos_p, sin_p, ret_tabs, tt)
        yc, gd_new = _gdn_prompt(pc, pab, gdn_prm, tt)
        yd = _moba_prompt(qd, kd, vd)
        xp = _outproj(xp, (ya, yb, yc, yd), mp[5], npost[1], w_out_b[l], tm)
        xp = _ffn(xp, mp[6], mp[7], mp[8], npre[2], npost[2], wg, wu, wdn, l, 1, tm, 256)
        hd = lambda a: a.reshape(a.shape[0], a.shape[1], N_HEADS, HEAD_DIM)
        news_p.append((hd(kd), hd(vd), rw_new, pa[:, -1], rt_new, gd_new, pc[:, seq - (GDN_CONV - 1):, :C_QKV]))

        xs = _ffn(xs, ms[0], ms[1], ms[2], npre[0], npost[0], wg, wu, wdn, l, 0, bs, 256)
        pa, pb, pc, qd, kd, vd, pab = (a[0] for a in _inproj(xs, ms[3], ms[4], npre[1], ws, bs))
        ya, yb, yc, rw_new, rt_new, gd_new = _sample_mixers(
            pa, state_rwkv_shift[l], pb, pc, pab, state_gdn_conv[l], state_rwkv[l], state_ret[l], state_gdn[l],
            rwkv_prm, (cos_s, sin_s, ret_tabs[4]), gdn_prm)
        yd = _moba_sample(qd, kd, vd, cache_k, cache_v, page_table, l)
        ys = tuple(a.reshape(1, bs, GROUP_WIDTH) for a in (ya, yb, yc, yd))
        xs = _outproj(xs, ys, ms[5], npost[1], w_out_b[l], bs)
        xs = _ffn(xs, ms[6], ms[7], ms[8], npre[2], npost[2], wg, wu, wdn, l, 1, bs, 256)
        hs_ = lambda a: a.reshape(bs, 1, N_HEADS, HEAD_DIM)
        conv_new = jnp.concatenate([state_gdn_conv[l][:, 1:], pc[:, None, :C_QKV]], axis=1)
        news_s.append((hs_(kd), hs_(vd), rw_new, pa, rt_new, gd_new, conv_new))

    kp, vp, rwp, shp, rtp, gdp, cvp = [jnp.stack(t) for t in zip(*news_p)]
    ks, vs, rws, shs, rts, gds, cvs = [jnp.stack(t) for t in zip(*news_s)]
    return (xp, xs.reshape(bs, 1, d), kp, vp, ks, vs, rwp, rws, shp, shs, rtp, rts, gdp, gds, cvp, cvs)
```

```python
import functools
import math

import numpy as np
import jax
import jax.numpy as jnp
from jax import lax
from jax.experimental import pallas as pl
from jax.experimental.pallas import tpu as pltpu

f32 = jnp.float32
bf16 = jnp.bfloat16
HI = lax.Precision.HIGHEST

GROUP_WIDTH = 256
HEAD_DIM = 64
N_HEADS = 4
FFN_RES = 0.5
NORM_EPS = 1e-6
RWKV_W_LORA = 32
RWKV_A_LORA = 32
RWKV_G_LORA = 64
RWKV_LN_EPS = 64e-5
RET_THETA = 10000.0
GDN_CONV = 4
MOBA_BLOCK = 256
MOBA_TOPK = 3
MOBA_QBLOCK = 256
PAGE_SIZE = 128
PAGES_PER_BLOCK = MOBA_BLOCK // PAGE_SIZE
CHUNK = 64
INV_BLOCK = 16
LORA_W = RWKV_W_LORA + RWKV_A_LORA + RWKV_G_LORA
A_COLS = 3 * GROUP_WIDTH + LORA_W
C_QKV = 3 * GROUP_WIDTH
NEG = -1e30
VMEM_LIMIT = 56 * 1024 * 1024


def _cparams(*sem):
    return pltpu.CompilerParams(dimension_semantics=sem, vmem_limit_bytes=VMEM_LIMIT)


def _dot(a, b, precision=None):
    return jnp.dot(a, b, preferred_element_type=f32, precision=precision)


def _dot_nt(a, b, precision=None):
    return lax.dot_general(a, b, (((1,), (1,)), ((), ())), preferred_element_type=f32, precision=precision)


def _dot_tn(a, b, precision=None):
    return lax.dot_general(a, b, (((0,), (0,)), ((), ())), preferred_element_type=f32, precision=precision)


def _rms(x, g):
    return x * lax.rsqrt(jnp.mean(x * x, axis=-1, keepdims=True) + NORM_EPS) * g


def _silu(x):
    return x * jax.nn.sigmoid(x)


def _softplus(x):
    return jnp.maximum(x, 0.0) + jnp.log(1.0 + jnp.exp(-jnp.abs(x)))


def _iota2(shape, axis):
    return lax.broadcasted_iota(jnp.int32, shape, axis)


def _head_seg():
    return (_iota2((GROUP_WIDTH, GROUP_WIDTH), 0) // HEAD_DIM == _iota2((GROUP_WIDTH, GROUP_WIDTH), 1) // HEAD_DIM).astype(f32)


def _segsum(x, seg):
    return _dot(x, seg, precision=HI)


def _eye(n):
    return (_iota2((n, n), 0) == _iota2((n, n), 1)).astype(f32)


def _split(x):
    hi = x.astype(bf16)
    return hi, (x - hi.astype(f32)).astype(bf16)


def _dot3(a, b):
    return _dot(a[0], b[0]) + (_dot(a[0], b[1]) + _dot(a[1], b[0]))


def _nilpotent_inverse(ns, eye, index):
    ts = [eye + n for n in ns]
    ps = [_split(n) for n in ns]
    for _ in range(int(math.log2(index)) - 1):
        ps = [_split(_dot3(p, p)) for p in ps]
        tsp = [_split(t) for t in ts]
        ts = [t + _dot3(p, tp) for t, p, tp in zip(ts, ps, tsp)]
    return ts


def _unit_lower_inverse(ns, c):
    eye = _eye(c)
    same = _iota2((c, c), 0) // INV_BLOCK == _iota2((c, c), 1) // INV_BLOCK
    t_diag = _nilpotent_inverse([jnp.where(same, n, 0.0) for n in ns], eye, INV_BLOCK)
    tds = [_split(t) for t in t_diag]
    ps = [_dot3(td, _split(jnp.where(same, 0.0, n))) for td, n in zip(tds, ns)]
    t_off = _nilpotent_inverse(ps, eye, c // INV_BLOCK)
    return [_dot3(_split(to), td) for to, td in zip(t_off, tds)]


def _mod_body(c_ref, w_ref, b_ref, o_ref):
    s = _silu(c_ref[...]).astype(bf16)
    o_ref[0] = _dot(s, w_ref[0].astype(bf16)) + b_ref[0]


def _modulation(c_all, w_mod, b_mod):
    depth, d, n9 = w_mod.shape
    rows = c_all.shape[0]
    tn = 1152
    return pl.pallas_call(
        _mod_body,
        grid=(depth, n9 // tn),
        in_specs=[pl.BlockSpec((rows, d), lambda l, j: (0, 0)),
                  pl.BlockSpec((1, d, tn), lambda l, j: (l, 0, j)),
                  pl.BlockSpec((1, 1, tn), lambda l, j: (l, 0, j))],
        out_specs=pl.BlockSpec((1, rows, tn), lambda l, j: (l, 0, j)),
        out_shape=jax.ShapeDtypeStruct((depth, rows, n9), f32),
        compiler_params=_cparams("arbitrary", "arbitrary"),
        name="modulation",
    )(c_all, w_mod, b_mod.reshape(depth, 1, n9))


def _ffn_body(x_ref, sh_ref, sc_ref, g_ref, npre_ref, npost_ref, wg_ref, wu_ref, wd_ref, o_ref, *, tf):
    x = x_ref[0]
    h = (_rms(x, npre_ref[...]) * (1.0 + sc_ref[0]) + sh_ref[0]).astype(bf16)
    acc = None
    for j in range(wg_ref.shape[1] // tf):
        cs = slice(j * tf, (j + 1) * tf)
        act = (_silu(_dot(h, wg_ref[:, cs])) * _dot(h, wu_ref[:, cs])).astype(bf16)
        part = _dot(act, wd_ref[cs, :])
        acc = part if acc is None else acc + part
    o_ref[0] = x + FFN_RES * g_ref[0] * _rms(acc, npost_ref[...])


def _ffn(x, sh, sc, g, npre, npost, wg, wu, wd, l, i, tm, tf):
    b, seq, d = x.shape
    r = sh.shape[1]
    dff = wg.shape[-1]
    mod_spec = pl.BlockSpec((1, r, d), lambda bi, t: (bi, 0 if r == 1 else t, 0))
    vec_spec = pl.BlockSpec((1, d), lambda bi, t: (0, 0))
    once = pl.Buffered(1)
    return pl.pallas_call(
        functools.partial(_ffn_body, tf=tf),
        grid=(b, seq // tm),
        in_specs=[pl.BlockSpec((1, tm, d), lambda bi, t: (bi, t, 0)),
                  mod_spec, mod_spec, mod_spec, vec_spec, vec_spec,
                  pl.BlockSpec((None, None, d, dff), lambda bi, t: (l, i, 0, 0), pipeline_mode=once),
                  pl.BlockSpec((None, None, d, dff), lambda bi, t: (l, i, 0, 0), pipeline_mode=once),
                  pl.BlockSpec((None, None, dff, d), lambda bi, t: (l, i, 0, 0), pipeline_mode=once)],
        out_specs=pl.BlockSpec((1, tm, d), lambda bi, t: (bi, t, 0)),
        out_shape=jax.ShapeDtypeStruct(x.shape, f32),
        compiler_params=_cparams("arbitrary", "arbitrary"),
        name="ffn_halfstep",
    )(x, sh, sc, g, npre, npost, wg, wu, wd)


def _inproj_body(x_ref, sh_ref, sc_ref, npre_ref, *refs):
    n = len(refs) // 2
    h = (_rms(x_ref[0], npre_ref[...]) * (1.0 + sc_ref[0]) + sh_ref[0]).astype(bf16)
    for w_ref, o_ref in zip(refs[:n], refs[n:]):
        o_ref[0] = _dot(h, w_ref[...])


def _inproj(x, sh, sc, npre, ws, tm):
    b, seq, d = x.shape
    r = sh.shape[1]
    mod_spec = pl.BlockSpec((1, r, d), lambda bi, t: (bi, 0 if r == 1 else t, 0))
    return pl.pallas_call(
        _inproj_body,
        grid=(b, seq // tm),
        in_specs=[pl.BlockSpec((1, tm, d), lambda bi, t: (bi, t, 0)), mod_spec, mod_spec,
                  pl.BlockSpec((1, d), lambda bi, t: (0, 0))]
                 + [pl.BlockSpec(w.shape, lambda bi, t: (0, 0)) for w in ws],
        out_specs=[pl.BlockSpec((1, tm, w.shape[1]), lambda bi, t: (bi, t, 0)) for w in ws],
        out_shape=[jax.ShapeDtypeStruct((b, seq, w.shape[1]), f32) for w in ws],
        compiler_params=_cparams("arbitrary", "arbitrary"),
        name="mixer_inproj",
    )(x, sh, sc, npre, *ws)


def _outproj_body(x_ref, ya_ref, yb_ref, yc_ref, yd_ref, g_ref, npost_ref, w_ref, o_ref):
    y = jnp.concatenate([ya_ref[0], yb_ref[0], yc_ref[0], yd_ref[0]], axis=-1).astype(bf16)
    o_ref[0] = x_ref[0] + g_ref[0] * _rms(_dot(y, w_ref[...]), npost_ref[...])


def _outproj(x, ys, g, npost, w, tm):
    b, seq, d = x.shape
    r = g.shape[1]
    y_spec = pl.BlockSpec((1, tm, GROUP_WIDTH), lambda bi, t: (bi, t, 0))
    return pl.pallas_call(
        _outproj_body,
        grid=(b, seq // tm),
        in_specs=[pl.BlockSpec((1, tm, d), lambda bi, t: (bi, t, 0)), y_spec, y_spec, y_spec, y_spec,
                  pl.BlockSpec((1, r, d), lambda bi, t: (bi, 0 if r == 1 else t, 0)),
                  pl.BlockSpec((1, d), lambda bi, t: (0, 0)),
                  pl.BlockSpec(w.shape, lambda bi, t: (0, 0))],
        out_specs=pl.BlockSpec((1, tm, d), lambda bi, t: (bi, t, 0)),
        out_shape=jax.ShapeDtypeStruct(x.shape, f32),
        compiler_params=_cparams("arbitrary", "arbitrary"),
        name="mixer_outproj",
    )(x, *ys, g, npost, w)


def _rwkv_pre(p, prev, mu, w0, w2p, a0, a2p, g2p, k_k, k_a, seg):
    xs = p + (prev - p) * mu
    r = xs[:, 0:GROUP_WIDTH]
    k = xs[:, GROUP_WIDTH:2 * GROUP_WIDTH]
    v = xs[:, 2 * GROUP_WIDTH:3 * GROUP_WIDTH]
    lo = xs[:, 3 * GROUP_WIDTH:]
    w_log = -_softplus(-(w0 + _dot(jnp.tanh(lo), w2p))) - 0.5
    logw = -jnp.exp(w_log)
    a = jax.nn.sigmoid(a0 + _dot(lo, a2p))
    g = _dot(jax.nn.sigmoid(lo), g2p)
    kk = k * k_k
    kk = kk * lax.rsqrt(_segsum(kk * kk, seg) + 1e-6)
    k2 = k * (1.0 + (a - 1.0) * k_a)
    return r, logw, k2, v, kk, a, g


def _rwkv_post(y, r, k2, v, g, r_k, ln_w, ln_b, seg):
    mean = _segsum(y, seg) * (1.0 / HEAD_DIM)
    d = y - mean
    var = _segsum(d * d, seg) * (1.0 / HEAD_DIM)
    yn = d * lax.rsqrt(var + RWKV_LN_EPS) * ln_w + ln_b
    bonus = _segsum(r * k2 * r_k, seg) * v
    return (yn + bonus) * g


def _rwkv_body(p_ref, mu_ref, w0_ref, w2_ref, a0_ref, a2_ref, g2_ref, kk_ref, ka_ref, rk_ref, lnw_ref, lnb_ref,
               y_ref, s_out_ref, s_scr, carry_scr, *, tt):
    c = CHUNK
    t = pl.program_id(1)

    @pl.when(t == 0)
    def _():
        s_scr[...] = jnp.zeros_like(s_scr)
        carry_scr[...] = jnp.zeros_like(carry_scr)

    seg = _head_seg()
    p = p_ref[0]
    row = _iota2((tt, 1), 0)
    prev = jnp.where(row == 0, carry_scr[7:8, :], pltpu.roll(p, 1, axis=0))
    carry_scr[...] = p[tt - 8:tt, :]
    r, logw, k2, v, kk, a, g = _rwkv_pre(p, prev, mu_ref[...], w0_ref[...], w2_ref[...], a0_ref[...], a2_ref[...],
                                         g2_ref[...], kk_ref[...], ka_ref[...], seg)
    kka = kk * a
    ri, ci = _iota2((c, c), 0), _iota2((c, c), 1)
    incl, strict = ri >= ci, ri > ci
    lmat = incl.astype(f32)

    n_ch = tt // c
    hsl = [slice(h * HEAD_DIM, (h + 1) * HEAD_DIM) for h in range(N_HEADS)]
    pairs = [(ch, h) for ch in range(n_ch) for h in range(N_HEADS)]

    rt, at, kt, bt, kh, bh, e_last = [], [], [], [], [], [], []
    for ch in range(n_ch):
        sl = slice(ch * c, (ch + 1) * c)
        lw = logw[sl]
        gcum = _dot(lmat, lw, precision=HI)
        g_last = gcum[c - 1:c, :]
        e_ng = jnp.exp(-gcum)
        e_rem = jnp.exp(g_last - gcum)
        e_last.append(jnp.exp(g_last))
        rt.append(r[sl] * jnp.exp(gcum))
        at.append(-kk[sl] * jnp.exp(gcum - lw))
        kt.append(k2[sl] * e_ng)
        bt.append(kka[sl] * e_ng)
        kh.append(k2[sl] * e_rem)
        bh.append(kka[sl] * e_rem)
    amat = [_dot_nt(jnp.concatenate([at[ch][:, hsl[h]], rt[ch][:, hsl[h]]], axis=0),
                    jnp.concatenate([kt[ch][:, hsl[h]], bt[ch][:, hsl[h]]], axis=0)) for ch, h in pairs]
    vh = [v[ch * c:(ch + 1) * c, hsl[h]] for ch, h in pairs]
    tmat = _unit_lower_inverse([jnp.where(strict, m[:c, c:], 0.0) for m in amat], c)
    akv = [_dot(jnp.where(strict, m[:c, :c], 0.0), x) for m, x in zip(amat, vh)]
    rkv = [_dot(jnp.where(incl, m[c:, :c], 0.0), x) for m, x in zip(amat, vh)]
    a_rb = [jnp.where(incl, m[c:, c:], 0.0) for m in amat]
    tx = [_dot(tm, jnp.concatenate([at[ch][:, hsl[h]], x], axis=1)) for tm, x, (ch, h) in zip(tmat, akv, pairs)]

    state = [s_scr[h] for h in range(N_HEADS)]
    y_chunks = []
    for ch in range(n_ch):
        idx = [ch * N_HEADS + h for h in range(N_HEADS)]
        u = [_dot_nt(tx[i][:, :HEAD_DIM], state[h]) + tx[i][:, HEAD_DIM:] for h, i in enumerate(idx)]
        y_chunks.append(jnp.concatenate(
            [_dot_nt(rt[ch][:, hsl[h]], state[h]) + rkv[i] + _dot(a_rb[i], u[h]) for h, i in enumerate(idx)], axis=1))
        state = [state[h] * e_last[ch][:, hsl[h]]
                 + _dot_tn(jnp.concatenate([vh[i], u[h]], axis=0),
                           jnp.concatenate([kh[ch][:, hsl[h]], bh[ch][:, hsl[h]]], axis=0))
                 for h, i in enumerate(idx)]
    for h in range(N_HEADS):
        s_scr[h] = state[h]
    y = jnp.concatenate(y_chunks, axis=0)
    y_ref[0] = _rwkv_post(y, r, k2, v, g, rk_ref[...], lnw_ref[...], lnb_ref[...], seg)
    s_out_ref[0] = s_scr[...]


def _rwkv_prompt(pa, prm, tt):
    b, seq, _ = pa.shape
    vec = lambda a: pl.BlockSpec(a.shape, lambda bi, t: (0, 0))
    return pl.pallas_call(
        functools.partial(_rwkv_body, tt=tt),
        grid=(b, seq // tt),
        in_specs=[pl.BlockSpec((1, tt, A_COLS), lambda bi, t: (bi, t, 0))] + [vec(a) for a in prm],
        out_specs=[pl.BlockSpec((1, tt, GROUP_WIDTH), lambda bi, t: (bi, t, 0)),
                   pl.BlockSpec((1, N_HEADS, HEAD_DIM, HEAD_DIM), lambda bi, t: (bi, 0, 0, 0))],
        out_shape=[jax.ShapeDtypeStruct((b, seq, GROUP_WIDTH), f32),
                   jax.ShapeDtypeStruct((b, N_HEADS, HEAD_DIM, HEAD_DIM), f32)],
        scratch_shapes=[pltpu.VMEM((N_HEADS, HEAD_DIM, HEAD_DIM), f32), pltpu.VMEM((8, A_COLS), f32)],
        compiler_params=_cparams("arbitrary", "arbitrary"),
        name="rwkv7_prompt",
    )(pa, *prm)


def _xpos(x, cos, sin_signed):
    n = x.shape[-1]
    even = (_iota2(x.shape, 1) % 2) == 0
    partner = jnp.where(even, pltpu.roll(x, n - 1, axis=1), pltpu.roll(x, 1, axis=1))
    return x * cos + partner * sin_signed


def _ret_body(p_ref, cos_ref, sin_ref, dmask_ref, eg_ref, erem_ref, elast_ref, y_ref, s_out_ref, s_scr, *, tt):
    c = CHUNK
    t = pl.program_id(1)

    @pl.when(t == 0)
    def _():
        s_scr[...] = jnp.zeros_like(s_scr)

    seg = _head_seg()
    p = p_ref[0]
    q = _xpos(p[:, 0:GROUP_WIDTH], cos_ref[...], sin_ref[...])
    k = _xpos(p[:, GROUP_WIDTH:2 * GROUP_WIDTH], cos_ref[...], sin_ref[...]) * HEAD_DIM ** -0.5
    v = p[:, 2 * GROUP_WIDTH:3 * GROUP_WIDTH]
    gt = p[:, 3 * GROUP_WIDTH:]
    e_last = elast_ref[...]
    o_chunks = []
    for ch in range(tt // c):
        sl = slice(ch * c, (ch + 1) * c)
        qg = q[sl] * eg_ref[...]
        kr = k[sl] * erem_ref[...]
        o_heads = []
        for h in range(N_HEADS):
            hs = slice(h * HEAD_DIM, (h + 1) * HEAD_DIM)
            s_old = s_scr[h]
            vh = v[sl, hs]
            intra = _dot(_dot_nt(q[sl, hs], k[sl, hs]) * dmask_ref[h], vh)
            o_heads.append(intra + _dot(qg[:, hs], s_old))
            s_scr[h] = s_old * e_last[:, hs] + _dot_tn(kr[:, hs], vh)
        o_chunks.append(jnp.concatenate(o_heads, axis=1))
    o = jnp.concatenate(o_chunks, axis=0)
    o = o * lax.rsqrt(_segsum(o * o, seg) * (1.0 / HEAD_DIM) + NORM_EPS)
    y_ref[0] = _silu(gt) * o
    s_out_ref[0] = s_scr[...]


def _ret_tables(seq, pos0):
    inv = 1.0 / (RET_THETA ** jnp.linspace(0.0, 1.0, HEAD_DIM // 2, dtype=f32))
    ang = (pos0 + jnp.arange(seq)).astype(f32)[:, None] * inv[None, :]
    cos = jnp.tile(jnp.repeat(jnp.cos(ang), 2, axis=1), (1, N_HEADS))
    sgn = jnp.tile(jnp.array([-1.0, 1.0], f32), HEAD_DIM // 2)
    sin = jnp.tile(jnp.repeat(jnp.sin(ang), 2, axis=1) * sgn[None, :], (1, N_HEADS))
    lg = np.log(1.0 - 2.0 ** (-5.0 - np.arange(N_HEADS, dtype=np.float64)))
    i = np.arange(CHUNK)
    diff = i[:, None] - i[None, :]
    dmask = np.where(diff[None] >= 0, np.exp(lg[:, None, None] * np.maximum(diff, 0)[None]), 0.0)
    lane = np.repeat(lg, HEAD_DIM)[None, :]
    e_g = np.exp((i[:, None] + 1) * lane)
    e_rem = np.exp((CHUNK - 1 - i[:, None]) * lane)
    e_last = np.exp(CHUNK * lane)
    e_one = np.exp(lane)
    return cos, sin, tuple(jnp.asarray(a, f32) for a in (dmask, e_g, e_rem, e_last, e_one))


def _ret_prompt(pb, cos, sin, tabs, tt):
    b, seq, _ = pb.shape
    dmask, e_g, e_rem, e_last, _ = tabs
    full = lambda a: pl.BlockSpec(a.shape, lambda bi, t: (0,) * a.ndim)
    return pl.pallas_call(
        functools.partial(_ret_body, tt=tt),
        grid=(b, seq // tt),
        in_specs=[pl.BlockSpec((1, tt, 4 * GROUP_WIDTH), lambda bi, t: (bi, t, 0)),
                  pl.BlockSpec((tt, GROUP_WIDTH), lambda bi, t: (t, 0)),
                  pl.BlockSpec((tt, GROUP_WIDTH), lambda bi, t: (t, 0)),
                  full(dmask), full(e_g), full(e_rem), full(e_last)],
        out_specs=[pl.BlockSpec((1, tt, GROUP_WIDTH), lambda bi, t: (bi, t, 0)),
                   pl.BlockSpec((1, N_HEADS, HEAD_DIM, HEAD_DIM), lambda bi, t: (bi, 0, 0, 0))],
        out_shape=[jax.ShapeDtypeStruct((b, seq, GROUP_WIDTH), f32),
                   jax.ShapeDtypeStruct((b, N_HEADS, HEAD_DIM, HEAD_DIM), f32)],
        scratch_shapes=[pltpu.VMEM((N_HEADS, HEAD_DIM, HEAD_DIM), f32)],
        compiler_params=_cparams("arbitrary", "arbitrary"),
        name="retention_prompt",
    )(pb, cos, sin, dmask, e_g, e_rem, e_last)


def _gdn_scalars(pab, alog_p, dtb_p, sel_a, sel_b):
    logd = -jnp.exp(alog_p) * _softplus(pab + dtb_p)
    beta = jax.nn.sigmoid(pab)
    return _dot(logd, sel_a, precision=HI), _dot(beta, sel_b, precision=HI)


def _gdn_qkv(conv, seg):
    cs = _silu(conv)
    q = cs[:, 0:GROUP_WIDTH]
    k = cs[:, GROUP_WIDTH:2 * GROUP_WIDTH]
    v = cs[:, 2 * GROUP_WIDTH:]
    q = q * lax.rsqrt(_segsum(q * q, seg) + 1e-6) * HEAD_DIM ** -0.5
    k = k * lax.rsqrt(_segsum(k * k, seg) + 1e-6)
    return q, k, v


def _gdn_body(pc_ref, pab_ref, cw_ref, alog_ref, dtb_ref, sela_ref, selb_ref, nw_ref,
              y_ref, s_out_ref, s_scr, ext_scr, *, tt):
    c = CHUNK
    t = pl.program_id(1)

    @pl.when(t == 0)
    def _():
        s_scr[...] = jnp.zeros_like(s_scr)
        ext_scr[pl.ds(0, 8), :] = jnp.zeros((8, C_QKV), f32)

    seg = _head_seg()
    pc = pc_ref[0]
    gate = pc[:, C_QKV:]
    ext_scr[pl.ds(8, tt), :] = pc[:, :C_QKV]
    cw = cw_ref[...]
    conv = sum(ext_scr[pl.ds(8 - (GDN_CONV - 1) + j, tt), :] * cw[j:j + 1, :] for j in range(GDN_CONV))
    ext_scr[pl.ds(0, 8), :] = pc[tt - 8:tt, :C_QKV]
    q, k, v = _gdn_qkv(conv, seg)
    logd, beta = _gdn_scalars(pab_ref[0], alog_ref[...], dtb_ref[...], sela_ref[...], selb_ref[...])
    kb = k * beta
    vb = v * beta

    ri, ci = _iota2((c, c), 0), _iota2((c, c), 1)
    incl, strict = ri >= ci, ri > ci
    lmat = incl.astype(f32)
    ones = jnp.ones((c, c), f32)
    eye4 = (_iota2((c, GROUP_WIDTH), 0) == _iota2((c, GROUP_WIDTH), 1) % HEAD_DIM).astype(f32)
    incl4 = _iota2((c, GROUP_WIDTH), 0) >= _iota2((c, GROUP_WIDTH), 1) % HEAD_DIM

    n_ch = tt // c
    hsl = [slice(h * HEAD_DIM, (h + 1) * HEAD_DIM) for h in range(N_HEADS)]
    pairs = [(ch, h) for ch in range(n_ch) for h in range(N_HEADS)]

    dmask, qg, kbg, kr, e_last = [], [], [], [], []
    for ch in range(n_ch):
        sl = slice(ch * c, (ch + 1) * c)
        gcum = _dot(lmat, logd[sl], precision=HI)
        gcum_t = _dot(ones, gcum * eye4, precision=HI)
        g_last = gcum[c - 1:c, :]
        e_g = jnp.exp(gcum)
        e_last.append(jnp.exp(g_last))
        dmask.append(jnp.exp(jnp.where(incl4, gcum - gcum_t, -jnp.inf)))
        qg.append(q[sl] * e_g)
        kbg.append(kb[sl] * e_g)
        kr.append(k[sl] * jnp.exp(g_last - gcum))
    rows = lambda x, ch, h: x[ch * c:(ch + 1) * c, hsl[h]]
    kq = [_dot_nt(jnp.concatenate([rows(kb, ch, h), rows(q, ch, h)], axis=0), rows(k, ch, h)) for ch, h in pairs]
    tmat = _unit_lower_inverse([jnp.where(strict, -x[:c] * dmask[ch][:, hsl[h]], 0.0) for x, (ch, h) in zip(kq, pairs)], c)
    aqk = [x[c:] * dmask[ch][:, hsl[h]] for x, (ch, h) in zip(kq, pairs)]
    uw = [_dot(tm, jnp.concatenate([rows(vb, ch, h), kbg[ch][:, hsl[h]]], axis=1)) for tm, (ch, h) in zip(tmat, pairs)]

    state = [s_scr[h] for h in range(N_HEADS)]
    o_chunks = []
    for ch in range(n_ch):
        idx = [ch * N_HEADS + h for h in range(N_HEADS)]
        vnew = [uw[i][:, :HEAD_DIM] - _dot(uw[i][:, HEAD_DIM:], state[h]) for h, i in enumerate(idx)]
        o_chunks.append(jnp.concatenate(
            [_dot(qg[ch][:, hsl[h]], state[h]) + _dot(aqk[i], vnew[h]) for h, i in enumerate(idx)], axis=1))
        state = [state[h] * e_last[ch][:, hsl[h]] + _dot_tn(kr[ch][:, hsl[h]], vnew[h]) for h in range(N_HEADS)]
    for h in range(N_HEADS):
        s_scr[h] = state[h]
    o = jnp.concatenate(o_chunks, axis=0)
    o = o * lax.rsqrt(_segsum(o * o, seg) * (1.0 / HEAD_DIM) + NORM_EPS) * nw_ref[...] * _silu(gate)
    y_ref[0] = o
    s_out_ref[0] = s_scr[...]


def _gdn_prompt(pc, pab, prm, tt):
    b, seq, _ = pc.shape
    vec = lambda a: pl.BlockSpec(a.shape, lambda bi, t: (0, 0))
    return pl.pallas_call(
        functools.partial(_gdn_body, tt=tt),
        grid=(b, seq // tt),
        in_specs=[pl.BlockSpec((1, tt, 4 * GROUP_WIDTH), lambda bi, t: (bi, t, 0)),
                  pl.BlockSpec((1, tt, 128), lambda bi, t: (bi, t, 0))] + [vec(a) for a in prm],
        out_specs=[pl.BlockSpec((1, tt, GROUP_WIDTH), lambda bi, t: (bi, t, 0)),
                   pl.BlockSpec((1, N_HEADS, HEAD_DIM, HEAD_DIM), lambda bi, t: (bi, 0, 0, 0))],
        out_shape=[jax.ShapeDtypeStruct((b, seq, GROUP_WIDTH), f32),
                   jax.ShapeDtypeStruct((b, N_HEADS, HEAD_DIM, HEAD_DIM), f32)],
        scratch_shapes=[pltpu.VMEM((N_HEADS, HEAD_DIM, HEAD_DIM), f32), pltpu.VMEM((tt + 8, C_QKV), f32)],
        compiler_params=_cparams("arbitrary", "arbitrary"),
        name="gdn_prompt",
    )(pc, pab, *prm)


def _block_means_body(k_ref, o_ref, *, nb):
    k = k_ref[0]
    o_ref[0] = jnp.sum(k.reshape(nb, MOBA_BLOCK, GROUP_WIDTH), axis=1) * (1.0 / MOBA_BLOCK)


def _block_means(k):
    b, seq, _ = k.shape
    nb = seq // MOBA_BLOCK
    return pl.pallas_call(
        functools.partial(_block_means_body, nb=nb),
        grid=(b,),
        in_specs=[pl.BlockSpec((1, seq, GROUP_WIDTH), lambda bi: (bi, 0, 0))],
        out_specs=pl.BlockSpec((1, nb, GROUP_WIDTH), lambda bi: (bi, 0, 0)),
        out_shape=jax.ShapeDtypeStruct((b, nb, GROUP_WIDTH), f32),
        compiler_params=_cparams("arbitrary"),
        name="moba_block_means",
    )(k)


def _top_blocks(gate, past, nb):
    blk = _iota2(gate.shape, 1).astype(f32)
    g = jnp.where(past, gate, -jnp.inf)
    picks = []
    for _ in range(MOBA_TOPK):
        m = jnp.max(g, axis=1, keepdims=True)
        idx = jnp.min(jnp.where(g == m, blk, float(nb)), axis=1, keepdims=True)
        picks.append(jnp.where(m > -jnp.inf, idx, -1.0))
        g = jnp.where(blk == idx, -jnp.inf, g)
    return picks


def _moba_body(q_ref, k_ref, v_ref, mean_ref, o_ref, *, nb):
    i = pl.program_id(1)
    tq = MOBA_QBLOCK
    own = (i * tq) // MOBA_BLOCK
    qs = q_ref[0] * HEAD_DIM ** -0.5
    qb = qs.astype(bf16)
    means = mean_ref[0]
    blk = _iota2((tq, nb), 1)
    past = blk < own
    hsl = [slice(h * HEAD_DIM, (h + 1) * HEAD_DIM) for h in range(N_HEADS)]
    picks = [_top_blocks(_dot_nt(qs[:, hs], means[:, hs], precision=HI), past, nb) for hs in hsl]

    k_own = k_ref[0, pl.ds(pl.multiple_of(own * MOBA_BLOCK, MOBA_BLOCK), MOBA_BLOCK), :]
    v_own = v_ref[0, pl.ds(pl.multiple_of(own * MOBA_BLOCK, MOBA_BLOCK), MOBA_BLOCK), :]
    qpos = i * tq + _iota2((tq, MOBA_BLOCK), 0)
    kpos = own * MOBA_BLOCK + _iota2((tq, MOBA_BLOCK), 1)
    causal = kpos <= qpos
    carry = []
    for hs in hsl:
        s = jnp.where(causal, _dot_nt(qb[:, hs], k_own[:, hs]), NEG)
        m = jnp.max(s, axis=1, keepdims=True)
        p = jnp.exp(s - m)
        carry += [m, jnp.sum(p, axis=1, keepdims=True), _dot(p.astype(bf16), v_own[:, hs])]

    def step(j, carry):
        start = pl.multiple_of(j * MOBA_BLOCK, MOBA_BLOCK)
        kj = k_ref[0, pl.ds(start, MOBA_BLOCK), :]
        vj = v_ref[0, pl.ds(start, MOBA_BLOCK), :]
        jf = j.astype(f32)
        out = []
        for h, hs in enumerate(hsl):
            m_old, l_old, acc = carry[3 * h:3 * h + 3]
            chosen = functools.reduce(jnp.logical_or, [pk == jf for pk in picks[h]])
            s = jnp.where(chosen, _dot_nt(qb[:, hs], kj[:, hs]), NEG)
            m_new = jnp.maximum(m_old, jnp.max(s, axis=1, keepdims=True))
            alpha = jnp.exp(m_old - m_new)
            p = jnp.exp(s - m_new)
            out += [m_new, alpha * l_old + jnp.sum(p, axis=1, keepdims=True),
                    alpha * acc + _dot(p.astype(bf16), vj[:, hs])]
        return tuple(out)

    carry = lax.fori_loop(0, own, step, tuple(carry))
    o_ref[0] = jnp.concatenate([carry[3 * h + 2] / carry[3 * h + 1] for h in range(N_HEADS)], axis=1)


def _moba_prompt(q, k, v):
    b, seq, _ = q.shape
    nb = seq // MOBA_BLOCK
    assert seq % MOBA_BLOCK == 0 and nb <= 32
    means = _block_means(k)
    kv_spec = pl.BlockSpec((1, seq, GROUP_WIDTH), lambda bi, i: (bi, 0, 0))
    return pl.pallas_call(
        functools.partial(_moba_body, nb=nb),
        grid=(b, seq // MOBA_QBLOCK),
        in_specs=[pl.BlockSpec((1, MOBA_QBLOCK, GROUP_WIDTH), lambda bi, i: (bi, i, 0)), kv_spec, kv_spec,
                  pl.BlockSpec((1, nb, GROUP_WIDTH), lambda bi, i: (bi, 0, 0))],
        out_specs=pl.BlockSpec((1, MOBA_QBLOCK, GROUP_WIDTH), lambda bi, i: (bi, i, 0)),
        out_shape=jax.ShapeDtypeStruct((b, seq, GROUP_WIDTH), f32),
        compiler_params=_cparams("arbitrary", "arbitrary"),
        name="moba_prompt",
    )(q, k.astype(bf16), v.astype(bf16), means)


def _row_to_col(row, eye):
    return jnp.sum(eye * row, axis=1, keepdims=True)


def _col_to_row(col, eye):
    return jnp.sum(eye * col, axis=0, keepdims=True)


def _sample_mix_body(pa_ref, shift_ref, pb_ref, pc_ref, pab_ref, conv_ref, srw_ref, srt_ref, sgd_ref,
                     mu_ref, w0_ref, w2_ref, a0_ref, a2_ref, g2_ref, kk_ref, ka_ref, rk_ref, lnw_ref, lnb_ref,
                     cos_ref, sin_ref, eone_ref, cw_ref, alog_ref, dtb_ref, sela_ref, selb_ref, nw_ref,
                     ya_ref, yb_ref, yc_ref, srw_out, srt_out, sgd_out):
    seg = _head_seg()
    eye = _eye(HEAD_DIM)
    hsl = [slice(h * HEAD_DIM, (h + 1) * HEAD_DIM) for h in range(N_HEADS)]

    r, logw, k2, v, kk, a, g = _rwkv_pre(pa_ref[0], shift_ref[0], mu_ref[...], w0_ref[...], w2_ref[...], a0_ref[...],
                                         a2_ref[...], g2_ref[...], kk_ref[...], ka_ref[...], seg)
    w = jnp.exp(logw)
    kka = kk * a
    ys = []
    for h, hs in enumerate(hsl):
        s_old = srw_ref[0, h]
        sa = jnp.sum(s_old * (-kk[:, hs]), axis=1, keepdims=True)
        s_new = s_old * w[:, hs] + sa * kka[:, hs] + _row_to_col(v[:, hs], eye) * k2[:, hs]
        srw_out[0, h] = s_new
        ys.append(_col_to_row(jnp.sum(s_new * r[:, hs], axis=1, keepdims=True), eye))
    ya_ref[0] = _rwkv_post(jnp.concatenate(ys, axis=1), r, k2, v, g, rk_ref[...], lnw_ref[...], lnb_ref[...], seg)

    pb = pb_ref[0]
    q = _xpos(pb[:, 0:GROUP_WIDTH], cos_ref[...], sin_ref[...])
    k = _xpos(pb[:, GROUP_WIDTH:2 * GROUP_WIDTH], cos_ref[...], sin_ref[...]) * HEAD_DIM ** -0.5
    v = pb[:, 2 * GROUP_WIDTH:3 * GROUP_WIDTH]
    e_one = eone_ref[...]
    os_ = []
    for h, hs in enumerate(hsl):
        s_new = srt_ref[0, h] * e_one[:, hs] + _row_to_col(k[:, hs], eye) * v[:, hs]
        srt_out[0, h] = s_new
        os_.append(jnp.sum(_row_to_col(q[:, hs], eye) * s_new, axis=0, keepdims=True))
    o = jnp.concatenate(os_, axis=1)
    o = o * lax.rsqrt(_segsum(o * o, seg) * (1.0 / HEAD_DIM) + NORM_EPS)
    yb_ref[0] = _silu(pb[:, 3 * GROUP_WIDTH:]) * o

    pc = pc_ref[0]
    cw = cw_ref[...]
    prev = conv_ref[0]
    conv = pc[:, :C_QKV] * cw[GDN_CONV - 1:GDN_CONV, :]
    for j in range(GDN_CONV - 1):
        conv = conv + prev[j:j + 1, :] * cw[j:j + 1, :]
    q, k, v = _gdn_qkv(conv, seg)
    logd, beta = _gdn_scalars(pab_ref[0], alog_ref[...], dtb_ref[...], sela_ref[...], selb_ref[...])
    alpha = jnp.exp(logd)
    os_ = []
    for h, hs in enumerate(hsl):
        s_old = sgd_ref[0, h]
        k_col = _row_to_col(k[:, hs], eye)
        ks = jnp.sum(k_col * s_old, axis=0, keepdims=True)
        s_new = alpha[:, hs] * s_old + k_col * (beta[:, hs] * (v[:, hs] - alpha[:, hs] * ks))
        sgd_out[0, h] = s_new
        os_.append(jnp.sum(_row_to_col(q[:, hs], eye) * s_new, axis=0, keepdims=True))
    o = jnp.concatenate(os_, axis=1)
    yc_ref[0] = o * lax.rsqrt(_segsum(o * o, seg) * (1.0 / HEAD_DIM) + NORM_EPS) * nw_ref[...] * _silu(pc[:, C_QKV:])


def _sample_mixers(pa, shift, pb, pc, pab, conv_prev, s_rwkv, s_ret, s_gdn, rwkv_prm, ret_prm, gdn_prm):
    nseq = pa.shape[0]
    row = lambda a: pl.BlockSpec((1, 1, a.shape[-1]), lambda b: (b, 0, 0))
    state = pl.BlockSpec((1, N_HEADS, HEAD_DIM, HEAD_DIM), lambda b: (b, 0, 0, 0))
    full = lambda a: pl.BlockSpec(a.shape, lambda b: (0,) * a.ndim)
    r3 = lambda a: a.reshape(nseq, 1, a.shape[-1])
    prm = tuple(rwkv_prm) + tuple(ret_prm) + tuple(gdn_prm)
    outs = pl.pallas_call(
        _sample_mix_body,
        grid=(nseq,),
        in_specs=[row(pa), row(shift), row(pb), row(pc), row(pab),
                  pl.BlockSpec((1, GDN_CONV - 1, C_QKV), lambda b: (b, 0, 0)), state, state, state]
                 + [full(a) for a in prm],
        out_specs=[row(pa[..., :GROUP_WIDTH])] * 3 + [state] * 3,
        out_shape=[jax.ShapeDtypeStruct((nseq, 1, GROUP_WIDTH), f32)] * 3
                  + [jax.ShapeDtypeStruct((nseq, N_HEADS, HEAD_DIM, HEAD_DIM), f32)] * 3,
        compiler_params=_cparams("arbitrary"),
        name="sample_mixers",
    )(r3(pa), r3(shift), r3(pb), r3(pc), r3(pab), conv_prev, s_rwkv, s_ret, s_gdn, *prm)
    return outs


def _moba_sample_body(pt_ref, q_ref, kn_ref, vn_ref, *refs, nb, bps):
    del pt_ref
    npg = bps * PAGES_PER_BLOCK
    k_refs, v_refs = refs[:npg], refs[npg:2 * npg]
    o_ref, gate_scr, m_scr, l_scr, acc_scr = refs[2 * npg:]
    j = pl.program_id(1)
    hmask = (_iota2((8, GROUP_WIDTH), 1) // HEAD_DIM == _iota2((8, GROUP_WIDTH), 0)).astype(f32)
    q8 = hmask * (q_ref[0] * HEAD_DIM ** -0.5)
    block_t = lambda page_refs: jnp.concatenate(
        [r[...].reshape(GROUP_WIDTH, PAGE_SIZE) for r in page_refs], axis=1)
    for e in range(bps):
        pages = slice(e * PAGES_PER_BLOCK, (e + 1) * PAGES_PER_BLOCK)
        kt = block_t(k_refs[pages])
        vt = block_t(v_refs[pages])
        mean = jnp.sum(kt, axis=1, keepdims=True) * (1.0 / MOBA_BLOCK)
        gate = _dot(q8, mean, precision=HI)
        s = _dot(q8, kt)
        m = jnp.max(s, axis=1, keepdims=True)
        p = jnp.exp(s - m)
        slot = j * bps + e
        gate_scr[slot] = jnp.broadcast_to(gate, (8, 128))
        m_scr[slot] = jnp.broadcast_to(m, (8, 128))
        l_scr[slot] = jnp.broadcast_to(jnp.sum(p, axis=1, keepdims=True), (8, 128))
        acc_scr[slot] = _dot_nt(p, vt)

    @pl.when(j == nb // bps - 1)
    def _():
        g = gate_scr[...]
        blk = _iota2(g.shape, 0).astype(f32)
        sel = jnp.zeros(g.shape, jnp.bool_)
        for _ in range(min(MOBA_TOPK, nb)):
            gm = jnp.max(g, axis=0, keepdims=True)
            idx = jnp.min(jnp.where(g == gm, blk, float(nb)), axis=0, keepdims=True)
            pick = blk == idx
            sel = jnp.logical_or(sel, pick)
            g = jnp.where(pick, -jnp.inf, g)
        s_own = jnp.broadcast_to(jnp.sum(q8 * kn_ref[0], axis=1, keepdims=True), (8, 128))
        mb = jnp.where(sel, m_scr[...], NEG)
        m_all = jnp.maximum(jnp.max(mb, axis=0), s_own)
        wgt = jnp.where(sel, jnp.exp(mb - m_all[None]), 0.0)
        w_own = jnp.exp(s_own - m_all)
        denom = jnp.sum(wgt * l_scr[...], axis=0) + w_own
        wide = lambda a: jnp.concatenate([a, a], axis=-1)
        num = jnp.sum(wide(wgt) * acc_scr[...], axis=0) + wide(w_own) * vn_ref[0]
        o_ref[0] = jnp.sum(hmask * (num / wide(denom)), axis=0, keepdims=True)


def _moba_sample(q, k_new, v_new, cache_kt, cache_vt, page_table, layer, bps):
    nseq, n_pages = page_table.shape
    nb = n_pages // PAGES_PER_BLOCK
    assert n_pages % PAGES_PER_BLOCK == 0 and nb % bps == 0
    npg = bps * PAGES_PER_BLOCK
    row = pl.BlockSpec((1, 1, GROUP_WIDTH), lambda b, j, pt: (b, 0, 0))
    page = lambda e: pl.BlockSpec((None, None, N_HEADS, HEAD_DIM, PAGE_SIZE),
                                  lambda b, j, pt: (layer, pt[b, npg * j + e], 0, 0, 0))
    pages = [page(e) for e in range(npg)]
    r3 = lambda a: a.reshape(nseq, 1, GROUP_WIDTH)
    return pl.pallas_call(
        functools.partial(_moba_sample_body, nb=nb, bps=bps),
        grid_spec=pltpu.PrefetchScalarGridSpec(
            num_scalar_prefetch=1,
            grid=(nseq, nb // bps),
            in_specs=[row, row, row] + pages + pages,
            out_specs=row,
            scratch_shapes=[pltpu.VMEM((nb, 8, 128), f32)] * 3 + [pltpu.VMEM((nb, 8, GROUP_WIDTH), f32)]),
        out_shape=jax.ShapeDtypeStruct((nseq, 1, GROUP_WIDTH), f32),
        compiler_params=_cparams("arbitrary", "arbitrary"),
        name="moba_sample",
    )(page_table, r3(q), r3(k_new), r3(v_new), *([cache_kt] * npg), *([cache_vt] * npg)).reshape(nseq, GROUP_WIDTH)


def _pad_rows(w, start, total):
    return jnp.zeros((total, w.shape[1]), w.dtype).at[start:start + w.shape[0]].set(w)


def _mixer_params(l, rwkv_mu, rwkv_w0, rwkv_w2, rwkv_a0, rwkv_a2, rwkv_g2, rwkv_k_k, rwkv_k_a, rwkv_r_k, rwkv_ln_w,
                  rwkv_ln_b, gdn_conv_w, gdn_a_log, gdn_dt_bias, gdn_norm_w):
    row = lambda a: a.reshape(1, -1)
    rwkv = (row(rwkv_mu[l]), row(rwkv_w0[l]), _pad_rows(rwkv_w2[l], 0, LORA_W), row(rwkv_a0[l]),
            _pad_rows(rwkv_a2[l], RWKV_W_LORA, LORA_W), _pad_rows(rwkv_g2[l], RWKV_W_LORA + RWKV_A_LORA, LORA_W),
            row(rwkv_k_k[l]), row(rwkv_k_a[l]), row(rwkv_r_k[l]), row(rwkv_ln_w[l]), row(rwkv_ln_b[l]))
    lane = np.arange(128)[:, None]
    col_head = (np.arange(GROUP_WIDTH) // HEAD_DIM)[None, :]
    sel_a = jnp.asarray(lane == col_head, f32)
    sel_b = jnp.asarray(lane == col_head + N_HEADS, f32)
    pad128 = lambda a: jnp.zeros((1, 128), f32).at[0, :N_HEADS].set(a)
    gdn = (gdn_conv_w[l], pad128(gdn_a_log[l]), pad128(gdn_dt_bias[l]), sel_a, sel_b,
           jnp.tile(gdn_norm_w[l], N_HEADS).reshape(1, GROUP_WIDTH))
    return rwkv, gdn


def kernel(x_prompt, x_sample, c_prompt, c_sample, cache_k, cache_v, page_table, state_rwkv, state_rwkv_shift,
           state_ret, state_gdn, state_gdn_conv, w_mod, b_mod, norm_pre, norm_post, ffn_w_gate, ffn_w_up, ffn_w_down,
           w_in, w_out, rwkv_mu, rwkv_w0, rwkv_w2, rwkv_a0, rwkv_a2, rwkv_g2, rwkv_k_k, rwkv_k_a, rwkv_r_k, rwkv_ln_w,
           rwkv_ln_b, gdn_conv_w, gdn_a_log, gdn_dt_bias, gdn_norm_w):
    depth = w_in.shape[0]
    bp, seq, d = x_prompt.shape
    bs = x_sample.shape[0]
    assert x_sample.shape[1] == 1
    past_len = page_table.shape[1] * PAGE_SIZE
    assert past_len % MOBA_BLOCK == 0
    cache_kt = jnp.transpose(cache_k, (0, 1, 3, 4, 2))
    cache_vt = jnp.transpose(cache_v, (0, 1, 3, 4, 2))

    n_c = bp + bs
    c_all = jnp.concatenate([c_prompt, c_sample, jnp.zeros((-n_c % 8, d), f32)], axis=0)
    mods = _modulation(c_all, w_mod, b_mod)

    wg, wu, wdn = ffn_w_gate.astype(bf16), ffn_w_up.astype(bf16), ffn_w_down.astype(bf16)
    w_in_b, w_out_b = w_in.astype(bf16), w_out.astype(bf16)
    a0, b0 = A_COLS, A_COLS + 4 * GROUP_WIDTH
    c0 = b0 + C_QKV + GROUP_WIDTH
    d0 = c0 + 2 * N_HEADS
    col_groups = [(0, a0), (a0, b0), (b0, c0), (d0, d0 + GROUP_WIDTH), (d0 + GROUP_WIDTH, d0 + 2 * GROUP_WIDTH),
                  (d0 + 2 * GROUP_WIDTH, d0 + 3 * GROUP_WIDTH)]

    cos_p, sin_p, ret_tabs = _ret_tables(seq, 0)
    cos_s, sin_s, _ = _ret_tables(1, past_len)
    tm = 512
    tt = 256

    xp, xs = x_prompt, x_sample.reshape(1, bs, d)
    news_p, news_s = [], []
    for l in range(depth):
        mp = [m.reshape(bp, 1, d) for m in jnp.split(mods[l, :bp], 9, axis=-1)]
        ms = [m.reshape(1, bs, d) for m in jnp.split(mods[l, bp:n_c], 9, axis=-1)]
        npre = [norm_pre[l, i].reshape(1, d) for i in range(3)]
        npost = [norm_post[l, i].reshape(1, d) for i in range(3)]
        ws = [w_in_b[l][:, lo:hi] for lo, hi in col_groups]
        ws.append(jnp.pad(w_in_b[l][:, c0:d0], ((0, 0), (0, 128 - 2 * N_HEADS))))
        rwkv_prm, gdn_prm = _mixer_params(l, rwkv_mu, rwkv_w0, rwkv_w2, rwkv_a0, rwkv_a2, rwkv_g2, rwkv_k_k, rwkv_k_a,
                                          rwkv_r_k, rwkv_ln_w, rwkv_ln_b, gdn_conv_w, gdn_a_log, gdn_dt_bias, gdn_norm_w)

        xp = _ffn(xp, mp[0], mp[1], mp[2], npre[0], npost[0], wg, wu, wdn, l, 0, tm, 256)
        pa, pb, pc, qd, kd, vd, pab = _inproj(xp, mp[3], mp[4], npre[1], ws, tm)
        ya, rw_new = _rwkv_prompt(pa, rwkv_prm, tt)
        yb, rt_new = _ret_prompt(pb, cos_p, sin_p, ret_tabs, tt)
        yc, gd_new = _gdn_prompt(pc, pab, gdn_prm, tt)
        yd = _moba_prompt(qd, kd, vd)
        xp = _outproj(xp, (ya, yb, yc, yd), mp[5], npost[1], w_out_b[l], tm)
        xp = _ffn(xp, mp[6], mp[7], mp[8], npre[2], npost[2], wg, wu, wdn, l, 1, tm, 256)
        hd = lambda a: a.reshape(a.shape[0], a.shape[1], N_HEADS, HEAD_DIM)
        news_p.append((hd(kd), hd(vd), rw_new, pa[:, -1], rt_new, gd_new, pc[:, seq - (GDN_CONV - 1):, :C_QKV]))

        xs = _ffn(xs, ms[0], ms[1], ms[2], npre[0], npost[0], wg, wu, wdn, l, 0, bs, 256)
        pa, pb, pc, qd, kd, vd, pab = (a[0] for a in _inproj(xs, ms[3], ms[4], npre[1], ws, bs))
        ya, yb, yc, rw_new, rt_new, gd_new = _sample_mixers(
            pa, state_rwkv_shift[l], pb, pc, pab, state_gdn_conv[l], state_rwkv[l], state_ret[l], state_gdn[l],
            rwkv_prm, (cos_s, sin_s, ret_tabs[4]), gdn_prm)
        yd = _moba_sample(qd, kd, vd, cache_kt, cache_vt, page_table, l, 4)
        ys = tuple(a.reshape(1, bs, GROUP_WIDTH) for a in (ya, yb, yc, yd))
        xs = _outproj(xs, ys, ms[5], npost[1], w_out_b[l], bs)
        xs = _ffn(xs, ms[6], ms[7], ms[8], npre[2], npost[2], wg, wu, wdn, l, 1, bs, 256)
        hs_ = lambda a: a.reshape(bs, 1, N_HEADS, HEAD_DIM)
        conv_new = jnp.concatenate([state_gdn_conv[l][:, 1:], pc[:, None, :C_QKV]], axis=1)
        news_s.append((hs_(kd), hs_(vd), rw_new, pa, rt_new, gd_new, conv_new))

    kp, vp, rwp, shp, rtp, gdp, cvp = [jnp.stack(t) for t in zip(*news_p)]
    ks, vs, rws, shs, rts, gds, cvs = [jnp.stack(t) for t in zip(*news_s)]
    return (xp, xs.reshape(bs, 1, d), kp, vp, ks, vs, rwp, rws, shp, shs, rtp, rts, gdp, gds, cvp, cvs)
```

```python
import functools
import math

import numpy as np
import jax
import jax.numpy as jnp
from jax import lax
from jax.experimental import pallas as pl
from jax.experimental.pallas import tpu as pltpu

f32 = jnp.float32
bf16 = jnp.bfloat16
HI = lax.Precision.HIGHEST

GROUP_WIDTH = 256
HEAD_DIM = 64
N_HEADS = 4
FFN_RES = 0.5
NORM_EPS = 1e-6
RWKV_W_LORA = 32
RWKV_A_LORA = 32
RWKV_G_LORA = 64
RWKV_LN_EPS = 64e-5
RET_THETA = 10000.0
GDN_CONV = 4
MOBA_BLOCK = 256
MOBA_TOPK = 3
SUM_ROWS = 16
PAGE_SIZE = 128
PAGES_PER_BLOCK = MOBA_BLOCK // PAGE_SIZE
CHUNK = 64
INV_BLOCK = 16
LORA_W = RWKV_W_LORA + RWKV_A_LORA + RWKV_G_LORA
A_COLS = 3 * GROUP_WIDTH + LORA_W
C_QKV = 3 * GROUP_WIDTH
NEG = -1e30
VMEM_LIMIT = 56 * 1024 * 1024


def _cparams(*sem):
    return pltpu.CompilerParams(dimension_semantics=sem, vmem_limit_bytes=VMEM_LIMIT)


def _dot(a, b, precision=None):
    return jnp.dot(a, b, preferred_element_type=f32, precision=precision)


def _dot_nt(a, b, precision=None):
    return lax.dot_general(a, b, (((1,), (1,)), ((), ())), preferred_element_type=f32, precision=precision)


def _dot_tn(a, b, precision=None):
    return lax.dot_general(a, b, (((0,), (0,)), ((), ())), preferred_element_type=f32, precision=precision)


def _rms(x, g):
    return x * lax.rsqrt(jnp.mean(x * x, axis=-1, keepdims=True) + NORM_EPS) * g


def _silu(x):
    return x * jax.nn.sigmoid(x)


def _softplus(x):
    return jnp.maximum(x, 0.0) + jnp.log(1.0 + jnp.exp(-jnp.abs(x)))


def _iota2(shape, axis):
    return lax.broadcasted_iota(jnp.int32, shape, axis)


def _head_seg():
    return (_iota2((GROUP_WIDTH, GROUP_WIDTH), 0) // HEAD_DIM == _iota2((GROUP_WIDTH, GROUP_WIDTH), 1) // HEAD_DIM).astype(f32)


def _segsum(x, seg):
    return _dot(x, seg, precision=HI)


def _eye(n):
    return (_iota2((n, n), 0) == _iota2((n, n), 1)).astype(f32)


def _split(x):
    hi = x.astype(bf16)
    return hi, (x - hi.astype(f32)).astype(bf16)


def _dot3(a, b):
    return _dot(a[0], b[0]) + (_dot(a[0], b[1]) + _dot(a[1], b[0]))


def _nilpotent_inverse(ns, eye, index):
    ts = [eye + n for n in ns]
    ps = [_split(n) for n in ns]
    for _ in range(int(math.log2(index)) - 1):
        ps = [_split(_dot3(p, p)) for p in ps]
        tsp = [_split(t) for t in ts]
        ts = [t + _dot3(p, tp) for t, p, tp in zip(ts, ps, tsp)]
    return ts


def _unit_lower_inverse(ns, c):
    eye = _eye(c)
    same = _iota2((c, c), 0) // INV_BLOCK == _iota2((c, c), 1) // INV_BLOCK
    t_diag = _nilpotent_inverse([jnp.where(same, n, 0.0) for n in ns], eye, INV_BLOCK)
    tds = [_split(t) for t in t_diag]
    ps = [_dot3(td, _split(jnp.where(same, 0.0, n))) for td, n in zip(tds, ns)]
    t_off = _nilpotent_inverse(ps, eye, c // INV_BLOCK)
    return [_dot3(_split(to), td) for to, td in zip(t_off, tds)]


def _mod_body(c_ref, w_ref, b_ref, o_ref):
    s = _silu(c_ref[...]).astype(bf16)
    o_ref[0] = _dot(s, w_ref[0].astype(bf16)) + b_ref[0]


def _modulation(c_all, w_mod, b_mod):
    depth, d, n9 = w_mod.shape
    rows = c_all.shape[0]
    tn = 1152
    return pl.pallas_call(
        _mod_body,
        grid=(depth, n9 // tn),
        in_specs=[pl.BlockSpec((rows, d), lambda l, j: (0, 0)),
                  pl.BlockSpec((1, d, tn), lambda l, j: (l, 0, j)),
                  pl.BlockSpec((1, 1, tn), lambda l, j: (l, 0, j))],
        out_specs=pl.BlockSpec((1, rows, tn), lambda l, j: (l, 0, j)),
        out_shape=jax.ShapeDtypeStruct((depth, rows, n9), f32),
        compiler_params=_cparams("arbitrary", "arbitrary"),
        name="modulation",
    )(c_all, w_mod, b_mod.reshape(depth, 1, n9))


def _ffn_body(x_ref, sh_ref, sc_ref, g_ref, npre_ref, npost_ref, wg_ref, wu_ref, wd_ref, o_ref, *, tf):
    x = x_ref[0]
    h = (_rms(x, npre_ref[...]) * (1.0 + sc_ref[0]) + sh_ref[0]).astype(bf16)
    acc = None
    for j in range(wg_ref.shape[1] // tf):
        cs = slice(j * tf, (j + 1) * tf)
        act = (_silu(_dot(h, wg_ref[:, cs])) * _dot(h, wu_ref[:, cs])).astype(bf16)
        part = _dot(act, wd_ref[cs, :])
        acc = part if acc is None else acc + part
    o_ref[0] = x + FFN_RES * g_ref[0] * _rms(acc, npost_ref[...])


def _ffn(x, sh, sc, g, npre, npost, wg, wu, wd, l, i, tm, tf):
    b, seq, d = x.shape
    r = sh.shape[1]
    dff = wg.shape[-1]
    mod_spec = pl.BlockSpec((1, r, d), lambda bi, t: (bi, 0 if r == 1 else t, 0))
    vec_spec = pl.BlockSpec((1, d), lambda bi, t: (0, 0))
    once = pl.Buffered(1)
    return pl.pallas_call(
        functools.partial(_ffn_body, tf=tf),
        grid=(b, seq // tm),
        in_specs=[pl.BlockSpec((1, tm, d), lambda bi, t: (bi, t, 0)),
                  mod_spec, mod_spec, mod_spec, vec_spec, vec_spec,
                  pl.BlockSpec((None, None, d, dff), lambda bi, t: (l, i, 0, 0), pipeline_mode=once),
                  pl.BlockSpec((None, None, d, dff), lambda bi, t: (l, i, 0, 0), pipeline_mode=once),
                  pl.BlockSpec((None, None, dff, d), lambda bi, t: (l, i, 0, 0), pipeline_mode=once)],
        out_specs=pl.BlockSpec((1, tm, d), lambda bi, t: (bi, t, 0)),
        out_shape=jax.ShapeDtypeStruct(x.shape, f32),
        compiler_params=_cparams("arbitrary", "arbitrary"),
        name="ffn_halfstep",
    )(x, sh, sc, g, npre, npost, wg, wu, wd)


def _inproj_body(x_ref, sh_ref, sc_ref, npre_ref, *refs):
    n = len(refs) // 2
    h = (_rms(x_ref[0], npre_ref[...]) * (1.0 + sc_ref[0]) + sh_ref[0]).astype(bf16)
    for w_ref, o_ref in zip(refs[:n], refs[n:]):
        o_ref[0] = _dot(h, w_ref[...])


def _inproj(x, sh, sc, npre, ws, tm):
    b, seq, d = x.shape
    r = sh.shape[1]
    mod_spec = pl.BlockSpec((1, r, d), lambda bi, t: (bi, 0 if r == 1 else t, 0))
    return pl.pallas_call(
        _inproj_body,
        grid=(b, seq // tm),
        in_specs=[pl.BlockSpec((1, tm, d), lambda bi, t: (bi, t, 0)), mod_spec, mod_spec,
                  pl.BlockSpec((1, d), lambda bi, t: (0, 0))]
                 + [pl.BlockSpec(w.shape, lambda bi, t: (0, 0)) for w in ws],
        out_specs=[pl.BlockSpec((1, tm, w.shape[1]), lambda bi, t: (bi, t, 0)) for w in ws],
        out_shape=[jax.ShapeDtypeStruct((b, seq, w.shape[1]), f32) for w in ws],
        compiler_params=_cparams("arbitrary", "arbitrary"),
        name="mixer_inproj",
    )(x, sh, sc, npre, *ws)


def _outproj_body(x_ref, ya_ref, yb_ref, yc_ref, yd_ref, g_ref, npost_ref, w_ref, o_ref):
    y = jnp.concatenate([ya_ref[0], yb_ref[0], yc_ref[0], yd_ref[0]], axis=-1).astype(bf16)
    o_ref[0] = x_ref[0] + g_ref[0] * _rms(_dot(y, w_ref[...]), npost_ref[...])


def _outproj(x, ys, g, npost, w, tm):
    b, seq, d = x.shape
    r = g.shape[1]
    y_spec = pl.BlockSpec((1, tm, GROUP_WIDTH), lambda bi, t: (bi, t, 0))
    return pl.pallas_call(
        _outproj_body,
        grid=(b, seq // tm),
        in_specs=[pl.BlockSpec((1, tm, d), lambda bi, t: (bi, t, 0)), y_spec, y_spec, y_spec, y_spec,
                  pl.BlockSpec((1, r, d), lambda bi, t: (bi, 0 if r == 1 else t, 0)),
                  pl.BlockSpec((1, d), lambda bi, t: (0, 0)),
                  pl.BlockSpec(w.shape, lambda bi, t: (0, 0))],
        out_specs=pl.BlockSpec((1, tm, d), lambda bi, t: (bi, t, 0)),
        out_shape=jax.ShapeDtypeStruct(x.shape, f32),
        compiler_params=_cparams("arbitrary", "arbitrary"),
        name="mixer_outproj",
    )(x, *ys, g, npost, w)


def _rwkv_pre(p, prev, mu, w0, w2p, a0, a2p, g2p, k_k, k_a, seg):
    xs = p + (prev - p) * mu
    r = xs[:, 0:GROUP_WIDTH]
    k = xs[:, GROUP_WIDTH:2 * GROUP_WIDTH]
    v = xs[:, 2 * GROUP_WIDTH:3 * GROUP_WIDTH]
    lo = xs[:, 3 * GROUP_WIDTH:]
    w_log = -_softplus(-(w0 + _dot(jnp.tanh(lo), w2p))) - 0.5
    logw = -jnp.exp(w_log)
    a = jax.nn.sigmoid(a0 + _dot(lo, a2p))
    g = _dot(jax.nn.sigmoid(lo), g2p)
    kk = k * k_k
    kk = kk * lax.rsqrt(_segsum(kk * kk, seg) + 1e-6)
    k2 = k * (1.0 + (a - 1.0) * k_a)
    return r, logw, k2, v, kk, a, g


def _rwkv_post(y, r, k2, v, g, r_k, ln_w, ln_b, seg):
    mean = _segsum(y, seg) * (1.0 / HEAD_DIM)
    d = y - mean
    var = _segsum(d * d, seg) * (1.0 / HEAD_DIM)
    yn = d * lax.rsqrt(var + RWKV_LN_EPS) * ln_w + ln_b
    bonus = _segsum(r * k2 * r_k, seg) * v
    return (yn + bonus) * g


def _rwkv_body(p_ref, mu_ref, w0_ref, w2_ref, a0_ref, a2_ref, g2_ref, kk_ref, ka_ref, rk_ref, lnw_ref, lnb_ref,
               y_ref, s_out_ref, s_scr, carry_scr, *, tt):
    c = CHUNK
    t = pl.program_id(1)

    @pl.when(t == 0)
    def _():
        s_scr[...] = jnp.zeros_like(s_scr)
        carry_scr[...] = jnp.zeros_like(carry_scr)

    seg = _head_seg()
    p = p_ref[0]
    row = _iota2((tt, 1), 0)
    prev = jnp.where(row == 0, carry_scr[7:8, :], pltpu.roll(p, 1, axis=0))
    carry_scr[...] = p[tt - 8:tt, :]
    r, logw, k2, v, kk, a, g = _rwkv_pre(p, prev, mu_ref[...], w0_ref[...], w2_ref[...], a0_ref[...], a2_ref[...],
                                         g2_ref[...], kk_ref[...], ka_ref[...], seg)
    kka = kk * a
    ri, ci = _iota2((c, c), 0), _iota2((c, c), 1)
    incl, strict = ri >= ci, ri > ci
    lmat = incl.astype(f32)

    n_ch = tt // c
    hsl = [slice(h * HEAD_DIM, (h + 1) * HEAD_DIM) for h in range(N_HEADS)]
    pairs = [(ch, h) for ch in range(n_ch) for h in range(N_HEADS)]

    rt, at, kt, bt, kh, bh, e_last = [], [], [], [], [], [], []
    for ch in range(n_ch):
        sl = slice(ch * c, (ch + 1) * c)
        lw = logw[sl]
        gcum = _dot(lmat, lw, precision=HI)
        g_last = gcum[c - 1:c, :]
        e_ng = jnp.exp(-gcum)
        e_rem = jnp.exp(g_last - gcum)
        e_last.append(jnp.exp(g_last))
        rt.append(r[sl] * jnp.exp(gcum))
        at.append(-kk[sl] * jnp.exp(gcum - lw))
        kt.append(k2[sl] * e_ng)
        bt.append(kka[sl] * e_ng)
        kh.append(k2[sl] * e_rem)
        bh.append(kka[sl] * e_rem)
    amat = [_dot_nt(jnp.concatenate([at[ch][:, hsl[h]], rt[ch][:, hsl[h]]], axis=0),
                    jnp.concatenate([kt[ch][:, hsl[h]], bt[ch][:, hsl[h]]], axis=0)) for ch, h in pairs]
    vh = [v[ch * c:(ch + 1) * c, hsl[h]] for ch, h in pairs]
    tmat = _unit_lower_inverse([jnp.where(strict, m[:c, c:], 0.0) for m in amat], c)
    akv = [_dot(jnp.where(strict, m[:c, :c], 0.0), x) for m, x in zip(amat, vh)]
    rkv = [_dot(jnp.where(incl, m[c:, :c], 0.0), x) for m, x in zip(amat, vh)]
    a_rb = [jnp.where(incl, m[c:, c:], 0.0) for m in amat]
    tx = [_dot(tm, jnp.concatenate([at[ch][:, hsl[h]], x], axis=1)) for tm, x, (ch, h) in zip(tmat, akv, pairs)]

    state = [s_scr[h] for h in range(N_HEADS)]
    y_chunks = []
    for ch in range(n_ch):
        idx = [ch * N_HEADS + h for h in range(N_HEADS)]
        u = [_dot_nt(tx[i][:, :HEAD_DIM], state[h]) + tx[i][:, HEAD_DIM:] for h, i in enumerate(idx)]
        y_chunks.append(jnp.concatenate(
            [_dot_nt(rt[ch][:, hsl[h]], state[h]) + rkv[i] + _dot(a_rb[i], u[h]) for h, i in enumerate(idx)], axis=1))
        state = [state[h] * e_last[ch][:, hsl[h]]
                 + _dot_tn(jnp.concatenate([vh[i], u[h]], axis=0),
                           jnp.concatenate([kh[ch][:, hsl[h]], bh[ch][:, hsl[h]]], axis=0))
                 for h, i in enumerate(idx)]
    for h in range(N_HEADS):
        s_scr[h] = state[h]
    y = jnp.concatenate(y_chunks, axis=0)
    y_ref[0] = _rwkv_post(y, r, k2, v, g, rk_ref[...], lnw_ref[...], lnb_ref[...], seg)
    s_out_ref[0] = s_scr[...]


def _rwkv_prompt(pa, prm, tt):
    b, seq, _ = pa.shape
    vec = lambda a: pl.BlockSpec(a.shape, lambda bi, t: (0, 0))
    return pl.pallas_call(
        functools.partial(_rwkv_body, tt=tt),
        grid=(b, seq // tt),
        in_specs=[pl.BlockSpec((1, tt, A_COLS), lambda bi, t: (bi, t, 0))] + [vec(a) for a in prm],
        out_specs=[pl.BlockSpec((1, tt, GROUP_WIDTH), lambda bi, t: (bi, t, 0)),
                   pl.BlockSpec((1, N_HEADS, HEAD_DIM, HEAD_DIM), lambda bi, t: (bi, 0, 0, 0))],
        out_shape=[jax.ShapeDtypeStruct((b, seq, GROUP_WIDTH), f32),
                   jax.ShapeDtypeStruct((b, N_HEADS, HEAD_DIM, HEAD_DIM), f32)],
        scratch_shapes=[pltpu.VMEM((N_HEADS, HEAD_DIM, HEAD_DIM), f32), pltpu.VMEM((8, A_COLS), f32)],
        compiler_params=_cparams("arbitrary", "arbitrary"),
        name="rwkv7_prompt",
    )(pa, *prm)


def _xpos(x, cos, sin_signed):
    n = x.shape[-1]
    even = (_iota2(x.shape, 1) % 2) == 0
    partner = jnp.where(even, pltpu.roll(x, n - 1, axis=1), pltpu.roll(x, 1, axis=1))
    return x * cos + partner * sin_signed


def _ret_body(p_ref, cos_ref, sin_ref, dmask_ref, eg_ref, erem_ref, elast_ref, y_ref, s_out_ref, s_scr, *, tt):
    c = CHUNK
    t = pl.program_id(1)

    @pl.when(t == 0)
    def _():
        s_scr[...] = jnp.zeros_like(s_scr)

    seg = _head_seg()
    p = p_ref[0]
    q = _xpos(p[:, 0:GROUP_WIDTH], cos_ref[...], sin_ref[...])
    k = _xpos(p[:, GROUP_WIDTH:2 * GROUP_WIDTH], cos_ref[...], sin_ref[...]) * HEAD_DIM ** -0.5
    v = p[:, 2 * GROUP_WIDTH:3 * GROUP_WIDTH]
    gt = p[:, 3 * GROUP_WIDTH:]
    e_last = elast_ref[...]
    n_ch = tt // c
    hsl = [slice(h * HEAD_DIM, (h + 1) * HEAD_DIM) for h in range(N_HEADS)]
    pairs = [(ch, h) for ch in range(n_ch) for h in range(N_HEADS)]
    rows = lambda x, ch, h: x[ch * c:(ch + 1) * c, hsl[h]]

    qg = [q[ch * c:(ch + 1) * c] * eg_ref[...] for ch in range(n_ch)]
    kr = [k[ch * c:(ch + 1) * c] * erem_ref[...] for ch in range(n_ch)]
    qk = [_dot_nt(rows(q, ch, h), rows(k, ch, h)) * dmask_ref[h] for ch, h in pairs]
    intra = [_dot(x, rows(v, ch, h)) for x, (ch, h) in zip(qk, pairs)]
    kv = [_dot_tn(kr[ch][:, hsl[h]], rows(v, ch, h)) for ch, h in pairs]

    state = [s_scr[h] for h in range(N_HEADS)]
    o_chunks = []
    for ch in range(n_ch):
        o_chunks.append(jnp.concatenate(
            [intra[ch * N_HEADS + h] + _dot(qg[ch][:, hsl[h]], state[h]) for h in range(N_HEADS)], axis=1))
        state = [state[h] * e_last[:, hsl[h]] + kv[ch * N_HEADS + h] for h in range(N_HEADS)]
    for h in range(N_HEADS):
        s_scr[h] = state[h]
    o = jnp.concatenate(o_chunks, axis=0)
    o = o * lax.rsqrt(_segsum(o * o, seg) * (1.0 / HEAD_DIM) + NORM_EPS)
    y_ref[0] = _silu(gt) * o
    s_out_ref[0] = s_scr[...]


def _ret_tables(seq, pos0):
    inv = 1.0 / (RET_THETA ** jnp.linspace(0.0, 1.0, HEAD_DIM // 2, dtype=f32))
    ang = (pos0 + jnp.arange(seq)).astype(f32)[:, None] * inv[None, :]
    cos = jnp.tile(jnp.repeat(jnp.cos(ang), 2, axis=1), (1, N_HEADS))
    sgn = jnp.tile(jnp.array([-1.0, 1.0], f32), HEAD_DIM // 2)
    sin = jnp.tile(jnp.repeat(jnp.sin(ang), 2, axis=1) * sgn[None, :], (1, N_HEADS))
    lg = np.log(1.0 - 2.0 ** (-5.0 - np.arange(N_HEADS, dtype=np.float64)))
    i = np.arange(CHUNK)
    diff = i[:, None] - i[None, :]
    dmask = np.where(diff[None] >= 0, np.exp(lg[:, None, None] * np.maximum(diff, 0)[None]), 0.0)
    lane = np.repeat(lg, HEAD_DIM)[None, :]
    e_g = np.exp((i[:, None] + 1) * lane)
    e_rem = np.exp((CHUNK - 1 - i[:, None]) * lane)
    e_last = np.exp(CHUNK * lane)
    e_one = np.exp(lane)
    return cos, sin, tuple(jnp.asarray(a, f32) for a in (dmask, e_g, e_rem, e_last, e_one))


def _ret_prompt(pb, cos, sin, tabs, tt):
    b, seq, _ = pb.shape
    dmask, e_g, e_rem, e_last, _ = tabs
    full = lambda a: pl.BlockSpec(a.shape, lambda bi, t: (0,) * a.ndim)
    return pl.pallas_call(
        functools.partial(_ret_body, tt=tt),
        grid=(b, seq // tt),
        in_specs=[pl.BlockSpec((1, tt, 4 * GROUP_WIDTH), lambda bi, t: (bi, t, 0)),
                  pl.BlockSpec((tt, GROUP_WIDTH), lambda bi, t: (t, 0)),
                  pl.BlockSpec((tt, GROUP_WIDTH), lambda bi, t: (t, 0)),
                  full(dmask), full(e_g), full(e_rem), full(e_last)],
        out_specs=[pl.BlockSpec((1, tt, GROUP_WIDTH), lambda bi, t: (bi, t, 0)),
                   pl.BlockSpec((1, N_HEADS, HEAD_DIM, HEAD_DIM), lambda bi, t: (bi, 0, 0, 0))],
        out_shape=[jax.ShapeDtypeStruct((b, seq, GROUP_WIDTH), f32),
                   jax.ShapeDtypeStruct((b, N_HEADS, HEAD_DIM, HEAD_DIM), f32)],
        scratch_shapes=[pltpu.VMEM((N_HEADS, HEAD_DIM, HEAD_DIM), f32)],
        compiler_params=_cparams("arbitrary", "arbitrary"),
        name="retention_prompt",
    )(pb, cos, sin, dmask, e_g, e_rem, e_last)


def _gdn_scalars(pab, alog_p, dtb_p, sel_a, sel_b):
    logd = -jnp.exp(alog_p) * _softplus(pab + dtb_p)
    beta = jax.nn.sigmoid(pab)
    return _dot(logd, sel_a, precision=HI), _dot(beta, sel_b, precision=HI)


def _gdn_qkv(conv, seg):
    cs = _silu(conv)
    q = cs[:, 0:GROUP_WIDTH]
    k = cs[:, GROUP_WIDTH:2 * GROUP_WIDTH]
    v = cs[:, 2 * GROUP_WIDTH:]
    q = q * lax.rsqrt(_segsum(q * q, seg) + 1e-6) * HEAD_DIM ** -0.5
    k = k * lax.rsqrt(_segsum(k * k, seg) + 1e-6)
    return q, k, v


def _gdn_body(pc_ref, pab_ref, cw_ref, alog_ref, dtb_ref, sela_ref, selb_ref, nw_ref,
              y_ref, s_out_ref, s_scr, ext_scr, *, tt):
    c = CHUNK
    t = pl.program_id(1)

    @pl.when(t == 0)
    def _():
        s_scr[...] = jnp.zeros_like(s_scr)
        ext_scr[pl.ds(0, 8), :] = jnp.zeros((8, C_QKV), f32)

    seg = _head_seg()
    pc = pc_ref[0]
    gate = pc[:, C_QKV:]
    ext_scr[pl.ds(8, tt), :] = pc[:, :C_QKV]
    cw = cw_ref[...]
    conv = sum(ext_scr[pl.ds(8 - (GDN_CONV - 1) + j, tt), :] * cw[j:j + 1, :] for j in range(GDN_CONV))
    ext_scr[pl.ds(0, 8), :] = pc[tt - 8:tt, :C_QKV]
    q, k, v = _gdn_qkv(conv, seg)
    logd, beta = _gdn_scalars(pab_ref[0], alog_ref[...], dtb_ref[...], sela_ref[...], selb_ref[...])
    kb = k * beta
    vb = v * beta

    ri, ci = _iota2((c, c), 0), _iota2((c, c), 1)
    incl, strict = ri >= ci, ri > ci
    lmat = incl.astype(f32)
    ones = jnp.ones((c, c), f32)
    eye4 = (_iota2((c, GROUP_WIDTH), 0) == _iota2((c, GROUP_WIDTH), 1) % HEAD_DIM).astype(f32)
    incl4 = _iota2((c, GROUP_WIDTH), 0) >= _iota2((c, GROUP_WIDTH), 1) % HEAD_DIM

    n_ch = tt // c
    hsl = [slice(h * HEAD_DIM, (h + 1) * HEAD_DIM) for h in range(N_HEADS)]
    pairs = [(ch, h) for ch in range(n_ch) for h in range(N_HEADS)]

    dmask, qg, kbg, kr, e_last = [], [], [], [], []
    for ch in range(n_ch):
        sl = slice(ch * c, (ch + 1) * c)
        gcum = _dot(lmat, logd[sl], precision=HI)
        gcum_t = _dot(ones, gcum * eye4, precision=HI)
        g_last = gcum[c - 1:c, :]
        e_g = jnp.exp(gcum)
        e_last.append(jnp.exp(g_last))
        dmask.append(jnp.exp(jnp.where(incl4, gcum - gcum_t, -jnp.inf)))
        qg.append(q[sl] * e_g)
        kbg.append(kb[sl] * e_g)
        kr.append(k[sl] * jnp.exp(g_last - gcum))
    rows = lambda x, ch, h: x[ch * c:(ch + 1) * c, hsl[h]]
    kq = [_dot_nt(jnp.concatenate([rows(kb, ch, h), rows(q, ch, h)], axis=0), rows(k, ch, h)) for ch, h in pairs]
    tmat = _unit_lower_inverse([jnp.where(strict, -x[:c] * dmask[ch][:, hsl[h]], 0.0) for x, (ch, h) in zip(kq, pairs)], c)
    aqk = [x[c:] * dmask[ch][:, hsl[h]] for x, (ch, h) in zip(kq, pairs)]
    uw = [_dot(tm, jnp.concatenate([rows(vb, ch, h), kbg[ch][:, hsl[h]]], axis=1)) for tm, (ch, h) in zip(tmat, pairs)]

    state = [s_scr[h] for h in range(N_HEADS)]
    o_chunks = []
    for ch in range(n_ch):
        idx = [ch * N_HEADS + h for h in range(N_HEADS)]
        vnew = [uw[i][:, :HEAD_DIM] - _dot(uw[i][:, HEAD_DIM:], state[h]) for h, i in enumerate(idx)]
        o_chunks.append(jnp.concatenate(
            [_dot(qg[ch][:, hsl[h]], state[h]) + _dot(aqk[i], vnew[h]) for h, i in enumerate(idx)], axis=1))
        state = [state[h] * e_last[ch][:, hsl[h]] + _dot_tn(kr[ch][:, hsl[h]], vnew[h]) for h in range(N_HEADS)]
    for h in range(N_HEADS):
        s_scr[h] = state[h]
    o = jnp.concatenate(o_chunks, axis=0)
    o = o * lax.rsqrt(_segsum(o * o, seg) * (1.0 / HEAD_DIM) + NORM_EPS) * nw_ref[...] * _silu(gate)
    y_ref[0] = o
    s_out_ref[0] = s_scr[...]


def _gdn_prompt(pc, pab, prm, tt):
    b, seq, _ = pc.shape
    vec = lambda a: pl.BlockSpec(a.shape, lambda bi, t: (0, 0))
    return pl.pallas_call(
        functools.partial(_gdn_body, tt=tt),
        grid=(b, seq // tt),
        in_specs=[pl.BlockSpec((1, tt, 4 * GROUP_WIDTH), lambda bi, t: (bi, t, 0)),
                  pl.BlockSpec((1, tt, 128), lambda bi, t: (bi, t, 0))] + [vec(a) for a in prm],
        out_specs=[pl.BlockSpec((1, tt, GROUP_WIDTH), lambda bi, t: (bi, t, 0)),
                   pl.BlockSpec((1, N_HEADS, HEAD_DIM, HEAD_DIM), lambda bi, t: (bi, 0, 0, 0))],
        out_shape=[jax.ShapeDtypeStruct((b, seq, GROUP_WIDTH), f32),
                   jax.ShapeDtypeStruct((b, N_HEADS, HEAD_DIM, HEAD_DIM), f32)],
        scratch_shapes=[pltpu.VMEM((N_HEADS, HEAD_DIM, HEAD_DIM), f32), pltpu.VMEM((tt + 8, C_QKV), f32)],
        compiler_params=_cparams("arbitrary", "arbitrary"),
        name="gdn_prompt",
    )(pc, pab, *prm)


def _block_means_body(k_ref, o_ref, *, nb):
    k = k_ref[0]
    o_ref[0] = jnp.sum(k.reshape(nb, MOBA_BLOCK, GROUP_WIDTH), axis=1) * (1.0 / MOBA_BLOCK)


def _block_means(k):
    b, seq, _ = k.shape
    nb = seq // MOBA_BLOCK
    return pl.pallas_call(
        functools.partial(_block_means_body, nb=nb),
        grid=(b,),
        in_specs=[pl.BlockSpec((1, seq, GROUP_WIDTH), lambda bi: (bi, 0, 0))],
        out_specs=pl.BlockSpec((1, nb, GROUP_WIDTH), lambda bi: (bi, 0, 0)),
        out_shape=jax.ShapeDtypeStruct((b, nb, GROUP_WIDTH), f32),
        compiler_params=_cparams("arbitrary"),
        name="moba_block_means",
    )(k)


def _top_blocks(gate, own, nb):
    blk = _iota2(gate.shape, 0).astype(f32)
    g = jnp.where(blk < own, gate, -jnp.inf)
    picks = []
    for _ in range(MOBA_TOPK):
        m = jnp.max(g, axis=0, keepdims=True)
        idx = jnp.min(jnp.where(g == m, blk, float(nb)), axis=0, keepdims=True)
        picks.append(jnp.where(m > -jnp.inf, idx, -1.0))
        g = jnp.where(blk == idx, -jnp.inf, g)
    return picks


def _moba_body(qt_ref, k_ref, vt_ref, mean_ref, o_ref, *, nb):
    i = pl.program_id(1)
    blk_w = MOBA_BLOCK
    means = mean_ref[0]
    hsl = [slice(h * HEAD_DIM, (h + 1) * HEAD_DIM) for h in range(N_HEADS)]
    qt = [qt_ref[0, h] * HEAD_DIM ** -0.5 for h in range(N_HEADS)]
    qtb = [x.astype(bf16) for x in qt]
    picks = [_top_blocks(_dot(means[:, hsl[h]], qt[h], precision=HI), i.astype(f32), nb) for h in range(N_HEADS)]

    own_start = pl.multiple_of(i * blk_w, blk_w)
    k_own = k_ref[0, pl.ds(own_start, blk_w), :]
    causal = _iota2((blk_w, blk_w), 0) <= _iota2((blk_w, blk_w), 1)
    s_own = [jnp.where(causal, _dot(k_own[:, hsl[h]], qtb[h]), NEG) for h in range(N_HEADS)]
    m_own = [jnp.max(x, axis=0, keepdims=True) for x in s_own]
    p_own = [jnp.exp(x - m).astype(bf16) for x, m in zip(s_own, m_own)]
    carry = []
    for h in range(N_HEADS):
        carry += [m_own[h], _dot(vt_ref[0, h, :, pl.ds(own_start, blk_w)], p_own[h])]

    def step(jj, carry):
        js = [2 * jj, 2 * jj + 1]
        starts = [pl.multiple_of(j * blk_w, blk_w) for j in js]
        kj = [k_ref[0, pl.ds(st, blk_w), :] for st in starts]
        units = [(h, e) for h in range(N_HEADS) for e in range(2)]
        s = [_dot(kj[e][:, hsl[h]], qtb[h]) for h, e in units]
        chosen = [functools.reduce(jnp.logical_or, [pk == js[e].astype(f32) for pk in picks[h]]) for h, e in units]
        s = [jnp.where(c, x, NEG) for c, x in zip(chosen, s)]
        m_new = [jnp.maximum(carry[2 * h], jnp.maximum(jnp.max(s[2 * h], axis=0, keepdims=True),
                                                       jnp.max(s[2 * h + 1], axis=0, keepdims=True)))
                 for h in range(N_HEADS)]
        p = [jnp.exp(x - m_new[h]).astype(bf16) for x, (h, e) in zip(s, units)]
        pv = [_dot(vt_ref[0, h, :, pl.ds(starts[e], blk_w)], x) for x, (h, e) in zip(p, units)]
        out = []
        for h in range(N_HEADS):
            out += [m_new[h], jnp.exp(carry[2 * h] - m_new[h]) * carry[2 * h + 1] + (pv[2 * h] + pv[2 * h + 1])]
        return tuple(out)

    carry = lax.fori_loop(0, (i + 1) // 2, step, tuple(carry))
    for h in range(N_HEADS):
        acc = carry[2 * h + 1]
        o_ref[0, h] = acc[:HEAD_DIM] / acc[HEAD_DIM:HEAD_DIM + 1]


def _moba_prompt(q, k, v):
    b, seq, _ = q.shape
    nb = seq // MOBA_BLOCK
    assert seq % MOBA_BLOCK == 0
    means = _block_means(k)
    heads_t = lambda a: jnp.transpose(a.reshape(b, seq, N_HEADS, HEAD_DIM), (0, 2, 3, 1))
    vt = jnp.concatenate([heads_t(v), jnp.ones((b, N_HEADS, SUM_ROWS, seq), f32)], axis=2).astype(bf16)
    tile_t = pl.BlockSpec((1, N_HEADS, HEAD_DIM, MOBA_BLOCK), lambda bi, i: (bi, 0, 0, i))
    o_t = pl.pallas_call(
        functools.partial(_moba_body, nb=nb),
        grid=(b, nb),
        in_specs=[tile_t,
                  pl.BlockSpec((1, seq, GROUP_WIDTH), lambda bi, i: (bi, 0, 0)),
                  pl.BlockSpec((1, N_HEADS, HEAD_DIM + SUM_ROWS, seq), lambda bi, i: (bi, 0, 0, 0)),
                  pl.BlockSpec((1, nb, GROUP_WIDTH), lambda bi, i: (bi, 0, 0))],
        out_specs=tile_t,
        out_shape=jax.ShapeDtypeStruct((b, N_HEADS, HEAD_DIM, seq), f32),
        compiler_params=_cparams("arbitrary", "arbitrary"),
        name="moba_prompt",
    )(heads_t(q), k.astype(bf16), vt, means)
    return jnp.transpose(o_t, (0, 3, 1, 2)).reshape(b, seq, GROUP_WIDTH)


def _row_to_col(row, eye):
    return jnp.sum(eye * row, axis=1, keepdims=True)


def _col_to_row(col, eye):
    return jnp.sum(eye * col, axis=0, keepdims=True)


def _sample_mix_body(pa_ref, shift_ref, pb_ref, pc_ref, pab_ref, conv_ref, srw_ref, srt_ref, sgd_ref,
                     mu_ref, w0_ref, w2_ref, a0_ref, a2_ref, g2_ref, kk_ref, ka_ref, rk_ref, lnw_ref, lnb_ref,
                     cos_ref, sin_ref, eone_ref, cw_ref, alog_ref, dtb_ref, sela_ref, selb_ref, nw_ref,
                     ya_ref, yb_ref, yc_ref, srw_out, srt_out, sgd_out):
    seg = _head_seg()
    eye = _eye(HEAD_DIM)
    hsl = [slice(h * HEAD_DIM, (h + 1) * HEAD_DIM) for h in range(N_HEADS)]

    r, logw, k2, v, kk, a, g = _rwkv_pre(pa_ref[0], shift_ref[0], mu_ref[...], w0_ref[...], w2_ref[...], a0_ref[...],
                                         a2_ref[...], g2_ref[...], kk_ref[...], ka_ref[...], seg)
    w = jnp.exp(logw)
    kka = kk * a
    ys = []
    for h, hs in enumerate(hsl):
        s_old = srw_ref[0, h]
        sa = jnp.sum(s_old * (-kk[:, hs]), axis=1, keepdims=True)
        s_new = s_old * w[:, hs] + sa * kka[:, hs] + _row_to_col(v[:, hs], eye) * k2[:, hs]
        srw_out[0, h] = s_new
        ys.append(_col_to_row(jnp.sum(s_new * r[:, hs], axis=1, keepdims=True), eye))
    ya_ref[0] = _rwkv_post(jnp.concatenate(ys, axis=1), r, k2, v, g, rk_ref[...], lnw_ref[...], lnb_ref[...], seg)

    pb = pb_ref[0]
    q = _xpos(pb[:, 0:GROUP_WIDTH], cos_ref[...], sin_ref[...])
    k = _xpos(pb[:, GROUP_WIDTH:2 * GROUP_WIDTH], cos_ref[...], sin_ref[...]) * HEAD_DIM ** -0.5
    v = pb[:, 2 * GROUP_WIDTH:3 * GROUP_WIDTH]
    e_one = eone_ref[...]
    os_ = []
    for h, hs in enumerate(hsl):
        s_new = srt_ref[0, h] * e_one[:, hs] + _row_to_col(k[:, hs], eye) * v[:, hs]
        srt_out[0, h] = s_new
        os_.append(jnp.sum(_row_to_col(q[:, hs], eye) * s_new, axis=0, keepdims=True))
    o = jnp.concatenate(os_, axis=1)
    o = o * lax.rsqrt(_segsum(o * o, seg) * (1.0 / HEAD_DIM) + NORM_EPS)
    yb_ref[0] = _silu(pb[:, 3 * GROUP_WIDTH:]) * o

    pc = pc_ref[0]
    cw = cw_ref[...]
    prev = conv_ref[0]
    conv = pc[:, :C_QKV] * cw[GDN_CONV - 1:GDN_CONV, :]
    for j in range(GDN_CONV - 1):
        conv = conv + prev[j:j + 1, :] * cw[j:j + 1, :]
    q, k, v = _gdn_qkv(conv, seg)
    logd, beta = _gdn_scalars(pab_ref[0], alog_ref[...], dtb_ref[...], sela_ref[...], selb_ref[...])
    alpha = jnp.exp(logd)
    os_ = []
    for h, hs in enumerate(hsl):
        s_old = sgd_ref[0, h]
        k_col = _row_to_col(k[:, hs], eye)
        ks = jnp.sum(k_col * s_old, axis=0, keepdims=True)
        s_new = alpha[:, hs] * s_old + k_col * (beta[:, hs] * (v[:, hs] - alpha[:, hs] * ks))
        sgd_out[0, h] = s_new
        os_.append(jnp.sum(_row_to_col(q[:, hs], eye) * s_new, axis=0, keepdims=True))
    o = jnp.concatenate(os_, axis=1)
    yc_ref[0] = o * lax.rsqrt(_segsum(o * o, seg) * (1.0 / HEAD_DIM) + NORM_EPS) * nw_ref[...] * _silu(pc[:, C_QKV:])


def _sample_mixers(pa, shift, pb, pc, pab, conv_prev, s_rwkv, s_ret, s_gdn, rwkv_prm, ret_prm, gdn_prm):
    nseq = pa.shape[0]
    row = lambda a: pl.BlockSpec((1, 1, a.shape[-1]), lambda b: (b, 0, 0))
    state = pl.BlockSpec((1, N_HEADS, HEAD_DIM, HEAD_DIM), lambda b: (b, 0, 0, 0))
    full = lambda a: pl.BlockSpec(a.shape, lambda b: (0,) * a.ndim)
    r3 = lambda a: a.reshape(nseq, 1, a.shape[-1])
    prm = tuple(rwkv_prm) + tuple(ret_prm) + tuple(gdn_prm)
    outs = pl.pallas_call(
        _sample_mix_body,
        grid=(nseq,),
        in_specs=[row(pa), row(shift), row(pb), row(pc), row(pab),
                  pl.BlockSpec((1, GDN_CONV - 1, C_QKV), lambda b: (b, 0, 0)), state, state, state]
                 + [full(a) for a in prm],
        out_specs=[row(pa[..., :GROUP_WIDTH])] * 3 + [state] * 3,
        out_shape=[jax.ShapeDtypeStruct((nseq, 1, GROUP_WIDTH), f32)] * 3
                  + [jax.ShapeDtypeStruct((nseq, N_HEADS, HEAD_DIM, HEAD_DIM), f32)] * 3,
        compiler_params=_cparams("arbitrary"),
        name="sample_mixers",
    )(r3(pa), r3(shift), r3(pb), r3(pc), r3(pab), conv_prev, s_rwkv, s_ret, s_gdn, *prm)
    return outs


def _moba_sample_body(pt_ref, q_ref, qb_ref, kn_ref, vn_ref, *refs, nb, bps):
    del pt_ref
    npg = bps * PAGES_PER_BLOCK
    k_refs, v_refs = refs[:npg], refs[npg:2 * npg]
    o_ref, gate_scr, m_scr, l_scr, acc_scr = refs[2 * npg:]
    j = pl.program_id(1)
    hmask = (_iota2((N_HEADS, GROUP_WIDTH), 1) // HEAD_DIM == _iota2((N_HEADS, GROUP_WIDTH), 0)).astype(f32)
    qb = qb_ref[0]
    qb = jnp.concatenate([qb] * (MOBA_BLOCK // 128), axis=1)
    block_t = lambda page_refs: jnp.concatenate(
        [r[...].reshape(GROUP_WIDTH, PAGE_SIZE) for r in page_refs], axis=1)
    blocks = [slice(e * PAGES_PER_BLOCK, (e + 1) * PAGES_PER_BLOCK) for e in range(bps)]
    kt = [block_t(k_refs[pg]) for pg in blocks]
    s = [jnp.sum((x * qb).reshape(N_HEADS, HEAD_DIM, MOBA_BLOCK), axis=1) for x in kt]
    gate = [jnp.sum(x, axis=1, keepdims=True) * (1.0 / MOBA_BLOCK) for x in s]
    m = [jnp.max(x, axis=1, keepdims=True) for x in s]
    p = [jnp.exp(x - mx) for x, mx in zip(s, m)]
    pv = [_dot_nt(x, block_t(v_refs[pg])) for x, pg in zip(p, blocks)]
    for e in range(bps):
        slot = j * bps + e
        gate_scr[slot] = jnp.broadcast_to(gate[e], (N_HEADS, 128))
        m_scr[slot] = jnp.broadcast_to(m[e], (N_HEADS, 128))
        l_scr[slot] = jnp.broadcast_to(jnp.sum(p[e], axis=1, keepdims=True), (N_HEADS, 128))
        acc_scr[slot] = pv[e]

    @pl.when(j == nb // bps - 1)
    def _():
        g = gate_scr[...]
        blk = _iota2(g.shape, 0).astype(f32)
        sel = jnp.zeros(g.shape, jnp.bool_)
        for _ in range(min(MOBA_TOPK, nb)):
            gm = jnp.max(g, axis=0, keepdims=True)
            idx = jnp.min(jnp.where(g == gm, blk, float(nb)), axis=0, keepdims=True)
            pick = blk == idx
            sel = jnp.logical_or(sel, pick)
            g = jnp.where(pick, -jnp.inf, g)
        q_own = hmask * (q_ref[0] * HEAD_DIM ** -0.5)
        s_own = jnp.broadcast_to(jnp.sum(q_own * kn_ref[0], axis=1, keepdims=True), (N_HEADS, 128))
        mb = jnp.where(sel, m_scr[...], NEG)
        m_all = jnp.maximum(jnp.max(mb, axis=0), s_own)
        wgt = jnp.where(sel, jnp.exp(mb - m_all[None]), 0.0)
        w_own = jnp.exp(s_own - m_all)
        denom = jnp.sum(wgt * l_scr[...], axis=0) + w_own
        wide = lambda a: jnp.concatenate([a, a], axis=-1)
        num = jnp.sum(wide(wgt) * acc_scr[...], axis=0) + wide(w_own) * vn_ref[0]
        o_ref[0] = jnp.sum(hmask * (num / wide(denom)), axis=0, keepdims=True)


def _moba_sample(q, k_new, v_new, cache_kt, cache_vt, page_table, layer, bps):
    nseq, n_pages = page_table.shape
    nb = n_pages // PAGES_PER_BLOCK
    assert n_pages % PAGES_PER_BLOCK == 0 and nb % bps == 0
    npg = bps * PAGES_PER_BLOCK
    row = pl.BlockSpec((1, 1, GROUP_WIDTH), lambda b, j, pt: (b, 0, 0))
    page = lambda e: pl.BlockSpec((None, None, N_HEADS, HEAD_DIM, PAGE_SIZE),
                                  lambda b, j, pt: (layer, pt[b, npg * j + e], 0, 0, 0))
    pages = [page(e) for e in range(npg)]
    r3 = lambda a: a.reshape(nseq, 1, GROUP_WIDTH)
    q_lanes = jnp.broadcast_to((q * HEAD_DIM ** -0.5)[:, :, None], (nseq, GROUP_WIDTH, 128))
    return pl.pallas_call(
        functools.partial(_moba_sample_body, nb=nb, bps=bps),
        grid_spec=pltpu.PrefetchScalarGridSpec(
            num_scalar_prefetch=1,
            grid=(nseq, nb // bps),
            in_specs=[row, pl.BlockSpec((1, GROUP_WIDTH, 128), lambda b, j, pt: (b, 0, 0)), row, row] + pages + pages,
            out_specs=row,
            scratch_shapes=[pltpu.VMEM((nb, N_HEADS, 128), f32)] * 3 + [pltpu.VMEM((nb, N_HEADS, GROUP_WIDTH), f32)]),
        out_shape=jax.ShapeDtypeStruct((nseq, 1, GROUP_WIDTH), f32),
        compiler_params=_cparams("arbitrary", "arbitrary"),
        name="moba_sample",
    )(page_table, r3(q), q_lanes, r3(k_new), r3(v_new), *([cache_kt] * npg), *([cache_vt] * npg)
      ).reshape(nseq, GROUP_WIDTH)


def _pad_rows(w, start, total):
    return jnp.zeros((total, w.shape[1]), w.dtype).at[start:start + w.shape[0]].set(w)


def _mixer_params(l, rwkv_mu, rwkv_w0, rwkv_w2, rwkv_a0, rwkv_a2, rwkv_g2, rwkv_k_k, rwkv_k_a, rwkv_r_k, rwkv_ln_w,
                  rwkv_ln_b, gdn_conv_w, gdn_a_log, gdn_dt_bias, gdn_norm_w):
    row = lambda a: a.reshape(1, -1)
    rwkv = (row(rwkv_mu[l]), row(rwkv_w0[l]), _pad_rows(rwkv_w2[l], 0, LORA_W), row(rwkv_a0[l]),
            _pad_rows(rwkv_a2[l], RWKV_W_LORA, LORA_W), _pad_rows(rwkv_g2[l], RWKV_W_LORA + RWKV_A_LORA, LORA_W),
            row(rwkv_k_k[l]), row(rwkv_k_a[l]), row(rwkv_r_k[l]), row(rwkv_ln_w[l]), row(rwkv_ln_b[l]))
    lane = np.arange(128)[:, None]
    col_head = (np.arange(GROUP_WIDTH) // HEAD_DIM)[None, :]
    sel_a = jnp.asarray(lane == col_head, f32)
    sel_b = jnp.asarray(lane == col_head + N_HEADS, f32)
    pad128 = lambda a: jnp.zeros((1, 128), f32).at[0, :N_HEADS].set(a)
    gdn = (gdn_conv_w[l], pad128(gdn_a_log[l]), pad128(gdn_dt_bias[l]), sel_a, sel_b,
           jnp.tile(gdn_norm_w[l], N_HEADS).reshape(1, GROUP_WIDTH))
    return rwkv, gdn


def kernel(x_prompt, x_sample, c_prompt, c_sample, cache_k, cache_v, page_table, state_rwkv, state_rwkv_shift,
           state_ret, state_gdn, state_gdn_conv, w_mod, b_mod, norm_pre, norm_post, ffn_w_gate, ffn_w_up, ffn_w_down,
           w_in, w_out, rwkv_mu, rwkv_w0, rwkv_w2, rwkv_a0, rwkv_a2, rwkv_g2, rwkv_k_k, rwkv_k_a, rwkv_r_k, rwkv_ln_w,
           rwkv_ln_b, gdn_conv_w, gdn_a_log, gdn_dt_bias, gdn_norm_w):
    depth = w_in.shape[0]
    bp, seq, d = x_prompt.shape
    bs = x_sample.shape[0]
    assert x_sample.shape[1] == 1
    past_len = page_table.shape[1] * PAGE_SIZE
    assert past_len % MOBA_BLOCK == 0
    cache_kt = jnp.transpose(cache_k, (0, 1, 3, 4, 2))
    cache_vt = jnp.transpose(cache_v, (0, 1, 3, 4, 2))

    n_c = bp + bs
    c_all = jnp.concatenate([c_prompt, c_sample, jnp.zeros((-n_c % 8, d), f32)], axis=0)
    mods = _modulation(c_all, w_mod, b_mod)

    wg, wu, wdn = ffn_w_gate.astype(bf16), ffn_w_up.astype(bf16), ffn_w_down.astype(bf16)
    w_in_b, w_out_b = w_in.astype(bf16), w_out.astype(bf16)
    a0, b0 = A_COLS, A_COLS + 4 * GROUP_WIDTH
    c0 = b0 + C_QKV + GROUP_WIDTH
    d0 = c0 + 2 * N_HEADS
    col_groups = [(0, a0), (a0, b0), (b0, c0), (d0, d0 + GROUP_WIDTH), (d0 + GROUP_WIDTH, d0 + 2 * GROUP_WIDTH),
                  (d0 + 2 * GROUP_WIDTH, d0 + 3 * GROUP_WIDTH)]

    cos_p, sin_p, ret_tabs = _ret_tables(seq, 0)
    cos_s, sin_s, _ = _ret_tables(1, past_len)
    tm = 512
    tt = 256

    xp, xs = x_prompt, x_sample.reshape(1, bs, d)
    news_p, news_s = [], []
    for l in range(depth):
        mp = [m.reshape(bp, 1, d) for m in jnp.split(mods[l, :bp], 9, axis=-1)]
        ms = [m.reshape(1, bs, d) for m in jnp.split(mods[l, bp:n_c], 9, axis=-1)]
        npre = [norm_pre[l, i].reshape(1, d) for i in range(3)]
        npost = [norm_post[l, i].reshape(1, d) for i in range(3)]
        ws = [w_in_b[l][:, lo:hi] for lo, hi in col_groups]
        ws.append(jnp.pad(w_in_b[l][:, c0:d0], ((0, 0), (0, 128 - 2 * N_HEADS))))
        rwkv_prm, gdn_prm = _mixer_params(l, rwkv_mu, rwkv_w0, rwkv_w2, rwkv_a0, rwkv_a2, rwkv_g2, rwkv_k_k, rwkv_k_a,
                                          rwkv_r_k, rwkv_ln_w, rwkv_ln_b, gdn_conv_w, gdn_a_log, gdn_dt_bias, gdn_norm_w)

        xp = _ffn(xp, mp[0], mp[1], mp[2], npre[0], npost[0], wg, wu, wdn, l, 0, tm, 256)
        pa, pb, pc, qd, kd, vd, pab = _inproj(xp, mp[3], mp[4], npre[1], ws, tm)
        ya, rw_new = _rwkv_prompt(pa, rwkv_prm, tt)
        yb, rt_new = _ret_prompt(pb, cos_p, sin_p, ret_tabs, tt)
        yc, gd_new = _gdn_prompt(pc, pab, gdn_prm, tt)
        yd = _moba_prompt(qd, kd, vd)
        xp = _outproj(xp, (ya, yb, yc, yd), mp[5], npost[1], w_out_b[l], tm)
        xp = _ffn(xp, mp[6], mp[7], mp[8], npre[2], npost[2], wg, wu, wdn, l, 1, tm, 256)
        hd = lambda a: a.reshape(a.shape[0], a.shape[1], N_HEADS, HEAD_DIM)
        news_p.append((hd(kd), hd(vd), rw_new, pa[:, -1], rt_new, gd_new, pc[:, seq - (GDN_CONV - 1):, :C_QKV]))

        xs = _ffn(xs, ms[0], ms[1], ms[2], npre[0], npost[0], wg, wu, wdn, l, 0, bs, 256)
        pa, pb, pc, qd, kd, vd, pab = (a[0] for a in _inproj(xs, ms[3], ms[4], npre[1], ws, bs))
        ya, yb, yc, rw_new, rt_new, gd_new = _sample_mixers(
            pa, state_rwkv_shift[l], pb, pc, pab, state_gdn_conv[l], state_rwkv[l], state_ret[l], state_gdn[l],
            rwkv_prm, (cos_s, sin_s, ret_tabs[4]), gdn_prm)
        yd = _moba_sample(qd, kd, vd, cache_kt, cache_vt, page_table, l, 4)
        ys = tuple(a.reshape(1, bs, GROUP_WIDTH) for a in (ya, yb, yc, yd))
        xs = _outproj(xs, ys, ms[5], npost[1], w_out_b[l], bs)
        xs = _ffn(xs, ms[6], ms[7], ms[8], npre[2], npost[2], wg, wu, wdn, l, 1, bs, 256)
        hs_ = lambda a: a.reshape(bs, 1, N_HEADS, HEAD_DIM)
        conv_new = jnp.concatenate([state_gdn_conv[l][:, 1:], pc[:, None, :C_QKV]], axis=1)
        news_s.append((hs_(kd), hs_(vd), rw_new, pa, rt_new, gd_new, conv_new))

    kp, vp, rwp, shp, rtp, gdp, cvp = [jnp.stack(t) for t in zip(*news_p)]
    ks, vs, rws, shs, rts, gds, cvs = [jnp.stack(t) for t in zip(*news_s)]
    return (xp, xs.reshape(bs, 1, d), kp, vp, ks, vs, rwp, rws, shp, shs, rtp, rts, gdp, gds, cvp, cvs)
```

```python
import functools
import math

import numpy as np
import jax
import jax.numpy as jnp
from jax import lax
from jax.experimental import pallas as pl
from jax.experimental.pallas import tpu as pltpu

f32 = jnp.float32
bf16 = jnp.bfloat16
HI = lax.Precision.HIGHEST

GROUP_WIDTH = 256
HEAD_DIM = 64
N_HEADS = 4
FFN_RES = 0.5
NORM_EPS = 1e-6
RWKV_W_LORA = 32
RWKV_A_LORA = 32
RWKV_G_LORA = 64
RWKV_LN_EPS = 64e-5
RET_THETA = 10000.0
GDN_CONV = 4
MOBA_BLOCK = 256
MOBA_TOPK = 3
SUM_ROWS = 16
PAGE_SIZE = 128
PAGES_PER_BLOCK = MOBA_BLOCK // PAGE_SIZE
CHUNK = 64
FFN_TF = 256
INV_BLOCK = 16
LORA_W = RWKV_W_LORA + RWKV_A_LORA + RWKV_G_LORA
A_COLS = 3 * GROUP_WIDTH + LORA_W
C_QKV = 3 * GROUP_WIDTH
NEG = -1e30
LOG2E = math.log2(math.e)
VMEM_LIMIT = 56 * 1024 * 1024


def _cparams(*sem):
    return pltpu.CompilerParams(dimension_semantics=sem, vmem_limit_bytes=VMEM_LIMIT)


def _dot(a, b, precision=None):
    return jnp.dot(a, b, preferred_element_type=f32, precision=precision)


def _dot_nt(a, b, precision=None):
    return lax.dot_general(a, b, (((1,), (1,)), ((), ())), preferred_element_type=f32, precision=precision)


def _dot_tn(a, b, precision=None):
    return lax.dot_general(a, b, (((0,), (0,)), ((), ())), preferred_element_type=f32, precision=precision)


def _rms(x, g):
    return x * lax.rsqrt(jnp.mean(x * x, axis=-1, keepdims=True) + NORM_EPS) * g


def _silu(x):
    return x * jax.nn.sigmoid(x)


def _softplus(x):
    return jnp.maximum(x, 0.0) + jnp.log(1.0 + jnp.exp(-jnp.abs(x)))


def _iota2(shape, axis):
    return lax.broadcasted_iota(jnp.int32, shape, axis)


def _head_seg():
    return (_iota2((GROUP_WIDTH, GROUP_WIDTH), 0) // HEAD_DIM == _iota2((GROUP_WIDTH, GROUP_WIDTH), 1) // HEAD_DIM).astype(f32)


def _segsum(x, seg):
    return _dot(x, seg, precision=HI)


def _eye(n):
    return (_iota2((n, n), 0) == _iota2((n, n), 1)).astype(f32)


def _split(x):
    hi = x.astype(bf16)
    return hi, (x - hi.astype(f32)).astype(bf16)


def _dot3(a, b):
    return _dot(a[0], b[0]) + (_dot(a[0], b[1]) + _dot(a[1], b[0]))


def _bdot(a, b):
    return _dot(a.astype(bf16), b.astype(bf16))


def _bdot_nt(a, b):
    return _dot_nt(a.astype(bf16), b.astype(bf16))


def _bdot_tn(a, b):
    return _dot_tn(a.astype(bf16), b.astype(bf16))


def _nilpotent_inverse(ns, eye, index):
    ts = [eye + n for n in ns]
    ps = [_split(n) for n in ns]
    for _ in range(int(math.log2(index)) - 1):
        ps = [_split(_dot3(p, p)) for p in ps]
        tsp = [_split(t) for t in ts]
        ts = [t + _dot3(p, tp) for t, p, tp in zip(ts, ps, tsp)]
    return ts


def _unit_lower_inverse(ns, c):
    eye = _eye(c)
    same = _iota2((c, c), 0) // INV_BLOCK == _iota2((c, c), 1) // INV_BLOCK
    t_diag = _nilpotent_inverse([jnp.where(same, n, 0.0) for n in ns], eye, INV_BLOCK)
    tds = [t.astype(bf16) for t in t_diag]
    ps = [_dot(td, jnp.where(same, 0.0, n).astype(bf16)) for td, n in zip(tds, ns)]
    t_off = [eye + p for p in ps]
    pw = [p.astype(bf16) for p in ps]
    for _ in range(int(math.log2(c // INV_BLOCK)) - 1):
        pw = [_dot(p, p).astype(bf16) for p in pw]
        t_off = [t + _dot(p, t.astype(bf16)) for t, p in zip(t_off, pw)]
    return [_dot(to.astype(bf16), td) for to, td in zip(t_off, tds)]


def _mod_body(c_ref, w_ref, b_ref, o_ref):
    s = _silu(c_ref[...]).astype(bf16)
    o_ref[0] = _dot(s, w_ref[0].astype(bf16)) + b_ref[0]


def _modulation(c_all, w_mod, b_mod):
    depth, d, n9 = w_mod.shape
    rows = c_all.shape[0]
    tn = 1152
    return pl.pallas_call(
        _mod_body,
        grid=(depth, n9 // tn),
        in_specs=[pl.BlockSpec((rows, d), lambda l, j: (0, 0)),
                  pl.BlockSpec((1, d, tn), lambda l, j: (l, 0, j)),
                  pl.BlockSpec((1, 1, tn), lambda l, j: (l, 0, j))],
        out_specs=pl.BlockSpec((1, rows, tn), lambda l, j: (l, 0, j)),
        out_shape=jax.ShapeDtypeStruct((depth, rows, n9), f32),
        compiler_params=_cparams("arbitrary", "arbitrary"),
        name="modulation",
    )(c_all, w_mod, b_mod.reshape(depth, 1, n9))


def _ffn_body(x_ref, sh_ref, sc_ref, g_ref, npre_ref, npost_ref, wg_ref, wu_ref, wd_ref, o_ref, *, tf):
    x = x_ref[0]
    h = (_rms(x, npre_ref[...]) * (1.0 + sc_ref[0]) + sh_ref[0]).astype(bf16)
    acc = None
    for j in range(wg_ref.shape[1] // tf):
        cs = slice(j * tf, (j + 1) * tf)
        act = (_silu(_dot(h, wg_ref[:, cs])) * _dot(h, wu_ref[:, cs])).astype(bf16)
        part = _dot(act, wd_ref[cs, :])
        acc = part if acc is None else acc + part
    o_ref[0] = x + FFN_RES * g_ref[0] * _rms(acc, npost_ref[...])


def _ffn(x, sh, sc, g, npre, npost, wg, wu, wd, l, i, tm, tf):
    b, seq, d = x.shape
    r = sh.shape[1]
    dff = wg.shape[-1]
    mod_spec = pl.BlockSpec((1, r, d), lambda bi, t: (bi, 0 if r == 1 else t, 0))
    vec_spec = pl.BlockSpec((1, d), lambda bi, t: (0, 0))
    once = pl.Buffered(1)
    return pl.pallas_call(
        functools.partial(_ffn_body, tf=tf),
        grid=(b, seq // tm),
        in_specs=[pl.BlockSpec((1, tm, d), lambda bi, t: (bi, t, 0)),
                  mod_spec, mod_spec, mod_spec, vec_spec, vec_spec,
                  pl.BlockSpec((None, None, d, dff), lambda bi, t: (l, i, 0, 0), pipeline_mode=once),
                  pl.BlockSpec((None, None, d, dff), lambda bi, t: (l, i, 0, 0), pipeline_mode=once),
                  pl.BlockSpec((None, None, dff, d), lambda bi, t: (l, i, 0, 0), pipeline_mode=once)],
        out_specs=pl.BlockSpec((1, tm, d), lambda bi, t: (bi, t, 0)),
        out_shape=jax.ShapeDtypeStruct(x.shape, f32),
        compiler_params=_cparams("arbitrary", "arbitrary"),
        name="ffn_halfstep",
    )(x, sh, sc, g, npre, npost, wg, wu, wd)


def _inproj_body(x_ref, sh_ref, sc_ref, npre_ref, *refs):
    n = len(refs) // 2
    h = (_rms(x_ref[0], npre_ref[...]) * (1.0 + sc_ref[0]) + sh_ref[0]).astype(bf16)
    for w_ref, o_ref in zip(refs[:n], refs[n:]):
        o_ref[0] = _dot(h, w_ref[...])


def _inproj(x, sh, sc, npre, ws, l, tm):
    b, seq, d = x.shape
    r = sh.shape[1]
    mod_spec = pl.BlockSpec((1, r, d), lambda bi, t: (bi, 0 if r == 1 else t, 0))
    return pl.pallas_call(
        _inproj_body,
        grid=(b, seq // tm),
        in_specs=[pl.BlockSpec((1, tm, d), lambda bi, t: (bi, t, 0)), mod_spec, mod_spec,
                  pl.BlockSpec((1, d), lambda bi, t: (0, 0))]
                 + [pl.BlockSpec((None,) + w.shape[1:], lambda bi, t: (l, 0, 0)) for w in ws],
        out_specs=[pl.BlockSpec((1, tm, w.shape[2]), lambda bi, t: (bi, t, 0)) for w in ws],
        out_shape=[jax.ShapeDtypeStruct((b, seq, w.shape[2]), f32) for w in ws],
        compiler_params=_cparams("arbitrary", "arbitrary"),
        name="mixer_inproj",
    )(x, sh, sc, npre, *ws)


def _outproj_body(x_ref, ya_ref, yb_ref, yc_ref, yd_ref, g_ref, npost_ref, w_ref, o_ref):
    y = jnp.concatenate([ya_ref[0], yb_ref[0], yc_ref[0], yd_ref[0]], axis=-1).astype(bf16)
    o_ref[0] = x_ref[0] + g_ref[0] * _rms(_dot(y, w_ref[...]), npost_ref[...])


def _outproj(x, ys, g, npost, w, l, tm):
    b, seq, d = x.shape
    r = g.shape[1]
    y_spec = pl.BlockSpec((1, tm, GROUP_WIDTH), lambda bi, t: (bi, t, 0))
    return pl.pallas_call(
        _outproj_body,
        grid=(b, seq // tm),
        in_specs=[pl.BlockSpec((1, tm, d), lambda bi, t: (bi, t, 0)), y_spec, y_spec, y_spec, y_spec,
                  pl.BlockSpec((1, r, d), lambda bi, t: (bi, 0 if r == 1 else t, 0)),
                  pl.BlockSpec((1, d), lambda bi, t: (0, 0)),
                  pl.BlockSpec((None,) + w.shape[1:], lambda bi, t: (l, 0, 0))],
        out_specs=pl.BlockSpec((1, tm, d), lambda bi, t: (bi, t, 0)),
        out_shape=jax.ShapeDtypeStruct(x.shape, f32),
        compiler_params=_cparams("arbitrary", "arbitrary"),
        name="mixer_outproj",
    )(x, *ys, g, npost, w)


def _rwkv_pre(p, prev, mu, w0, w2p, a0, a2p, g2p, k_k, k_a, seg):
    xs = p + (prev - p) * mu
    r = xs[:, 0:GROUP_WIDTH]
    k = xs[:, GROUP_WIDTH:2 * GROUP_WIDTH]
    v = xs[:, 2 * GROUP_WIDTH:3 * GROUP_WIDTH]
    lo = xs[:, 3 * GROUP_WIDTH:]
    w_log = -_softplus(-(w0 + _dot(jnp.tanh(lo), w2p))) - 0.5
    logw = -jnp.exp(w_log)
    a = jax.nn.sigmoid(a0 + _dot(lo, a2p))
    g = _dot(jax.nn.sigmoid(lo), g2p)
    kk = k * k_k
    kk = kk * lax.rsqrt(_segsum(kk * kk, seg) + 1e-6)
    k2 = k * (1.0 + (a - 1.0) * k_a)
    return r, logw, k2, v, kk, a, g


def _rwkv_post(y, r, k2, v, g, r_k, ln_w, ln_b, seg):
    mean = _segsum(y, seg) * (1.0 / HEAD_DIM)
    d = y - mean
    var = _segsum(d * d, seg) * (1.0 / HEAD_DIM)
    yn = d * lax.rsqrt(var + RWKV_LN_EPS) * ln_w + ln_b
    bonus = _segsum(r * k2 * r_k, seg) * v
    return (yn + bonus) * g


def _rwkv_body(p_ref, mu_ref, w0_ref, w2_ref, a0_ref, a2_ref, g2_ref, kk_ref, ka_ref, rk_ref, lnw_ref, lnb_ref,
               y_ref, s_out_ref, s_scr, carry_scr, *, tt):
    c = CHUNK
    t = pl.program_id(1)

    @pl.when(t == 0)
    def _():
        s_scr[...] = jnp.zeros_like(s_scr)
        carry_scr[...] = jnp.zeros_like(carry_scr)

    seg = _head_seg()
    p = p_ref[0]
    row = _iota2((tt, 1), 0)
    prev = jnp.where(row == 0, carry_scr[7:8, :], pltpu.roll(p, 1, axis=0))
    carry_scr[...] = p[tt - 8:tt, :]
    r, logw, k2, v, kk, a, g = _rwkv_pre(p, prev, mu_ref[...], w0_ref[...], w2_ref[...], a0_ref[...], a2_ref[...],
                                         g2_ref[...], kk_ref[...], ka_ref[...], seg)
    kka = kk * a
    ri, ci = _iota2((c, c), 0), _iota2((c, c), 1)
    incl, strict = ri >= ci, ri > ci
    lmat = incl.astype(f32)

    n_ch = tt // c
    hsl = [slice(h * HEAD_DIM, (h + 1) * HEAD_DIM) for h in range(N_HEADS)]
    pairs = [(ch, h) for ch in range(n_ch) for h in range(N_HEADS)]

    rt, at, kt, bt, kh, bh, e_last = [], [], [], [], [], [], []
    for ch in range(n_ch):
        sl = slice(ch * c, (ch + 1) * c)
        lw = logw[sl]
        gcum = _dot(lmat, lw, precision=HI)
        g_last = gcum[c - 1:c, :]
        e_ng = jnp.exp(-gcum)
        e_rem = jnp.exp(g_last - gcum)
        e_last.append(jnp.exp(g_last))
        rt.append(r[sl] * jnp.exp(gcum))
        at.append(-kk[sl] * jnp.exp(gcum - lw))
        kt.append(k2[sl] * e_ng)
        bt.append(kka[sl] * e_ng)
        kh.append(k2[sl] * e_rem)
        bh.append(kka[sl] * e_rem)
    atb, rtb, ktb, btb, khb, bhb = ([x.astype(bf16) for x in xs] for xs in (at, rt, kt, bt, kh, bh))
    vb = v.astype(bf16)
    amat = [_dot_nt(jnp.concatenate([atb[ch][:, hsl[h]], rtb[ch][:, hsl[h]]], axis=0),
                    jnp.concatenate([ktb[ch][:, hsl[h]], btb[ch][:, hsl[h]]], axis=0)) for ch, h in pairs]
    vh = [vb[ch * c:(ch + 1) * c, hsl[h]] for ch, h in pairs]
    tmat = _unit_lower_inverse([jnp.where(strict, m[:c, c:], 0.0) for m in amat], c)
    akv = [_dot(jnp.where(strict, m[:c, :c], 0.0).astype(bf16), x) for m, x in zip(amat, vh)]
    rkv = [_dot(jnp.where(incl, m[c:, :c], 0.0).astype(bf16), x) for m, x in zip(amat, vh)]
    a_rb = [jnp.where(incl, m[c:, c:], 0.0).astype(bf16) for m in amat]
    tx = [_dot(tm.astype(bf16), jnp.concatenate([atb[ch][:, hsl[h]], x.astype(bf16)], axis=1))
          for tm, x, (ch, h) in zip(tmat, akv, pairs)]

    state = [s_scr[h] for h in range(N_HEADS)]
    y_chunks = []
    for ch in range(n_ch):
        idx = [ch * N_HEADS + h for h in range(N_HEADS)]
        sb = [x.astype(bf16) for x in state]
        u = [_dot_nt(tx[i][:, :HEAD_DIM].astype(bf16), sb[h]) + tx[i][:, HEAD_DIM:] for h, i in enumerate(idx)]
        ub = [x.astype(bf16) for x in u]
        y_chunks.append(jnp.concatenate(
            [_dot_nt(rtb[ch][:, hsl[h]], sb[h]) + rkv[i] + _dot(a_rb[i], ub[h]) for h, i in enumerate(idx)], axis=1))
        state = [state[h] * e_last[ch][:, hsl[h]]
                 + _dot_tn(jnp.concatenate([vh[i], ub[h]], axis=0),
                           jnp.concatenate([khb[ch][:, hsl[h]], bhb[ch][:, hsl[h]]], axis=0))
                 for h, i in enumerate(idx)]
    for h in range(N_HEADS):
        s_scr[h] = state[h]
    y = jnp.concatenate(y_chunks, axis=0)
    y_ref[0] = _rwkv_post(y, r, k2, v, g, rk_ref[...], lnw_ref[...], lnb_ref[...], seg)
    s_out_ref[0] = s_scr[...]


def _rwkv_prompt(pa, prm, tt):
    b, seq, _ = pa.shape
    vec = lambda a: pl.BlockSpec(a.shape, lambda bi, t: (0, 0))
    return pl.pallas_call(
        functools.partial(_rwkv_body, tt=tt),
        grid=(b, seq // tt),
        in_specs=[pl.BlockSpec((1, tt, A_COLS), lambda bi, t: (bi, t, 0))] + [vec(a) for a in prm],
        out_specs=[pl.BlockSpec((1, tt, GROUP_WIDTH), lambda bi, t: (bi, t, 0)),
                   pl.BlockSpec((1, N_HEADS, HEAD_DIM, HEAD_DIM), lambda bi, t: (bi, 0, 0, 0))],
        out_shape=[jax.ShapeDtypeStruct((b, seq, GROUP_WIDTH), f32),
                   jax.ShapeDtypeStruct((b, N_HEADS, HEAD_DIM, HEAD_DIM), f32)],
        scratch_shapes=[pltpu.VMEM((N_HEADS, HEAD_DIM, HEAD_DIM), f32), pltpu.VMEM((8, A_COLS), f32)],
        compiler_params=_cparams("arbitrary", "arbitrary"),
        name="rwkv7_prompt",
    )(pa, *prm)


def _xpos(x, cos, sin_signed):
    n = x.shape[-1]
    even = (_iota2(x.shape, 1) % 2) == 0
    partner = jnp.where(even, pltpu.roll(x, n - 1, axis=1), pltpu.roll(x, 1, axis=1))
    return x * cos + partner * sin_signed


def _ret_body(p_ref, cos_ref, sin_ref, dmask_ref, eg_ref, erem_ref, elast_ref, y_ref, s_out_ref, s_scr, *, tt):
    c = CHUNK
    t = pl.program_id(1)

    @pl.when(t == 0)
    def _():
        s_scr[...] = jnp.zeros_like(s_scr)

    seg = _head_seg()
    p = p_ref[0]
    q = _xpos(p[:, 0:GROUP_WIDTH], cos_ref[...], sin_ref[...])
    k = _xpos(p[:, GROUP_WIDTH:2 * GROUP_WIDTH], cos_ref[...], sin_ref[...]) * HEAD_DIM ** -0.5
    v = p[:, 2 * GROUP_WIDTH:3 * GROUP_WIDTH]
    gt = p[:, 3 * GROUP_WIDTH:]
    e_last = elast_ref[...]
    n_ch = tt // c
    hsl = [slice(h * HEAD_DIM, (h + 1) * HEAD_DIM) for h in range(N_HEADS)]
    pairs = [(ch, h) for ch in range(n_ch) for h in range(N_HEADS)]
    rows = lambda x, ch, h: x[ch * c:(ch + 1) * c, hsl[h]]

    qg = [q[ch * c:(ch + 1) * c] * eg_ref[...] for ch in range(n_ch)]
    kr = [k[ch * c:(ch + 1) * c] * erem_ref[...] for ch in range(n_ch)]
    qk = [_dot_nt(rows(q, ch, h), rows(k, ch, h)) * dmask_ref[h] for ch, h in pairs]
    intra = [_dot(x, rows(v, ch, h)) for x, (ch, h) in zip(qk, pairs)]
    kv = [_dot_tn(kr[ch][:, hsl[h]], rows(v, ch, h)) for ch, h in pairs]

    state = [s_scr[h] for h in range(N_HEADS)]
    o_chunks = []
    for ch in range(n_ch):
        o_chunks.append(jnp.concatenate(
            [intra[ch * N_HEADS + h] + _dot(qg[ch][:, hsl[h]], state[h]) for h in range(N_HEADS)], axis=1))
        state = [state[h] * e_last[:, hsl[h]] + kv[ch * N_HEADS + h] for h in range(N_HEADS)]
    for h in range(N_HEADS):
        s_scr[h] = state[h]
    o = jnp.concatenate(o_chunks, axis=0)
    o = o * lax.rsqrt(_segsum(o * o, seg) * (1.0 / HEAD_DIM) + NORM_EPS)
    y_ref[0] = _silu(gt) * o
    s_out_ref[0] = s_scr[...]


def _ret_tables(seq, pos0):
    inv = 1.0 / (RET_THETA ** jnp.linspace(0.0, 1.0, HEAD_DIM // 2, dtype=f32))
    ang = (pos0 + jnp.arange(seq)).astype(f32)[:, None] * inv[None, :]
    cos = jnp.tile(jnp.repeat(jnp.cos(ang), 2, axis=1), (1, N_HEADS))
    sgn = jnp.tile(jnp.array([-1.0, 1.0], f32), HEAD_DIM // 2)
    sin = jnp.tile(jnp.repeat(jnp.sin(ang), 2, axis=1) * sgn[None, :], (1, N_HEADS))
    lg = np.log(1.0 - 2.0 ** (-5.0 - np.arange(N_HEADS, dtype=np.float64)))
    i = np.arange(CHUNK)
    diff = i[:, None] - i[None, :]
    dmask = np.where(diff[None] >= 0, np.exp(lg[:, None, None] * np.maximum(diff, 0)[None]), 0.0)
    lane = np.repeat(lg, HEAD_DIM)[None, :]
    e_g = np.exp((i[:, None] + 1) * lane)
    e_rem = np.exp((CHUNK - 1 - i[:, None]) * lane)
    e_last = np.exp(CHUNK * lane)
    e_one = np.exp(lane)
    return cos, sin, tuple(jnp.asarray(a, f32) for a in (dmask, e_g, e_rem, e_last, e_one))


def _ret_prompt(pb, cos, sin, tabs, tt):
    b, seq, _ = pb.shape
    dmask, e_g, e_rem, e_last, _ = tabs
    full = lambda a: pl.BlockSpec(a.shape, lambda bi, t: (0,) * a.ndim)
    return pl.pallas_call(
        functools.partial(_ret_body, tt=tt),
        grid=(b, seq // tt),
        in_specs=[pl.BlockSpec((1, tt, 4 * GROUP_WIDTH), lambda bi, t: (bi, t, 0)),
                  pl.BlockSpec((tt, GROUP_WIDTH), lambda bi, t: (t, 0)),
                  pl.BlockSpec((tt, GROUP_WIDTH), lambda bi, t: (t, 0)),
                  full(dmask), full(e_g), full(e_rem), full(e_last)],
        out_specs=[pl.BlockSpec((1, tt, GROUP_WIDTH), lambda bi, t: (bi, t, 0)),
                   pl.BlockSpec((1, N_HEADS, HEAD_DIM, HEAD_DIM), lambda bi, t: (bi, 0, 0, 0))],
        out_shape=[jax.ShapeDtypeStruct((b, seq, GROUP_WIDTH), f32),
                   jax.ShapeDtypeStruct((b, N_HEADS, HEAD_DIM, HEAD_DIM), f32)],
        scratch_shapes=[pltpu.VMEM((N_HEADS, HEAD_DIM, HEAD_DIM), f32)],
        compiler_params=_cparams("arbitrary", "arbitrary"),
        name="retention_prompt",
    )(pb, cos, sin, dmask, e_g, e_rem, e_last)


def _gdn_scalars(pab, alog_p, dtb_p, sel_a, sel_b):
    logd = -jnp.exp(alog_p) * _softplus(pab + dtb_p)
    beta = jax.nn.sigmoid(pab)
    return _dot(logd, sel_a, precision=HI), _dot(beta, sel_b, precision=HI)


def _gdn_qkv(conv, seg):
    cs = _silu(conv)
    q = cs[:, 0:GROUP_WIDTH]
    k = cs[:, GROUP_WIDTH:2 * GROUP_WIDTH]
    v = cs[:, 2 * GROUP_WIDTH:]
    q = q * lax.rsqrt(_segsum(q * q, seg) + 1e-6) * HEAD_DIM ** -0.5
    k = k * lax.rsqrt(_segsum(k * k, seg) + 1e-6)
    return q, k, v


def _gdn_body(pc_ref, pab_ref, cw_ref, alog_ref, dtb_ref, sela_ref, selb_ref, nw_ref,
              y_ref, s_out_ref, s_scr, ext_scr, *, tt):
    c = CHUNK
    t = pl.program_id(1)

    @pl.when(t == 0)
    def _():
        s_scr[...] = jnp.zeros_like(s_scr)
        ext_scr[pl.ds(0, 8), :] = jnp.zeros((8, C_QKV), f32)

    seg = _head_seg()
    pc = pc_ref[0]
    gate = pc[:, C_QKV:]
    ext_scr[pl.ds(8, tt), :] = pc[:, :C_QKV]
    cw = cw_ref[...]
    conv = sum(ext_scr[pl.ds(8 - (GDN_CONV - 1) + j, tt), :] * cw[j:j + 1, :] for j in range(GDN_CONV))
    ext_scr[pl.ds(0, 8), :] = pc[tt - 8:tt, :C_QKV]
    q, k, v = _gdn_qkv(conv, seg)
    logd, beta = _gdn_scalars(pab_ref[0], alog_ref[...], dtb_ref[...], sela_ref[...], selb_ref[...])
    kb = k * beta
    vb = v * beta

    ri, ci = _iota2((c, c), 0), _iota2((c, c), 1)
    incl, strict = ri >= ci, ri > ci
    lmat = incl.astype(f32)
    ones = jnp.ones((c, c), f32)
    eye4 = (_iota2((c, GROUP_WIDTH), 0) == _iota2((c, GROUP_WIDTH), 1) % HEAD_DIM).astype(f32)
    incl4 = _iota2((c, GROUP_WIDTH), 0) >= _iota2((c, GROUP_WIDTH), 1) % HEAD_DIM

    n_ch = tt // c
    hsl = [slice(h * HEAD_DIM, (h + 1) * HEAD_DIM) for h in range(N_HEADS)]
    pairs = [(ch, h) for ch in range(n_ch) for h in range(N_HEADS)]

    dmask, qg, kbg, kr, e_last = [], [], [], [], []
    for ch in range(n_ch):
        sl = slice(ch * c, (ch + 1) * c)
        gcum = _dot(lmat, logd[sl], precision=HI)
        gcum_t = _dot(ones, gcum * eye4, precision=HI)
        g_last = gcum[c - 1:c, :]
        e_g = jnp.exp(gcum)
        e_last.append(jnp.exp(g_last))
        dmask.append(jnp.exp(jnp.where(incl4, gcum - gcum_t, -jnp.inf)))
        qg.append(q[sl] * e_g)
        kbg.append(kb[sl] * e_g)
        kr.append(k[sl] * jnp.exp(g_last - gcum))
    rows = lambda x, ch, h: x[ch * c:(ch + 1) * c, hsl[h]]
    kq = [_dot_nt(jnp.concatenate([rows(kb, ch, h), rows(q, ch, h)], axis=0), rows(k, ch, h)) for ch, h in pairs]
    tmat = _unit_lower_inverse([jnp.where(strict, -x[:c] * dmask[ch][:, hsl[h]], 0.0) for x, (ch, h) in zip(kq, pairs)], c)
    aqk = [x[c:] * dmask[ch][:, hsl[h]] for x, (ch, h) in zip(kq, pairs)]
    uw = [_dot(tm, jnp.concatenate([rows(vb, ch, h), kbg[ch][:, hsl[h]]], axis=1)) for tm, (ch, h) in zip(tmat, pairs)]

    state = [s_scr[h] for h in range(N_HEADS)]
    o_chunks = []
    for ch in range(n_ch):
        idx = [ch * N_HEADS + h for h in range(N_HEADS)]
        vnew = [uw[i][:, :HEAD_DIM] - _dot(uw[i][:, HEAD_DIM:], state[h]) for h, i in enumerate(idx)]
        o_chunks.append(jnp.concatenate(
            [_dot(qg[ch][:, hsl[h]], state[h]) + _dot(aqk[i], vnew[h]) for h, i in enumerate(idx)], axis=1))
        state = [state[h] * e_last[ch][:, hsl[h]] + _dot_tn(kr[ch][:, hsl[h]], vnew[h]) for h in range(N_HEADS)]
    for h in range(N_HEADS):
        s_scr[h] = state[h]
    o = jnp.concatenate(o_chunks, axis=0)
    o = o * lax.rsqrt(_segsum(o * o, seg) * (1.0 / HEAD_DIM) + NORM_EPS) * nw_ref[...] * _silu(gate)
    y_ref[0] = o
    s_out_ref[0] = s_scr[...]


def _gdn_prompt(pc, pab, prm, tt):
    b, seq, _ = pc.shape
    vec = lambda a: pl.BlockSpec(a.shape, lambda bi, t: (0, 0))
    return pl.pallas_call(
        functools.partial(_gdn_body, tt=tt),
        grid=(b, seq // tt),
        in_specs=[pl.BlockSpec((1, tt, 4 * GROUP_WIDTH), lambda bi, t: (bi, t, 0)),
                  pl.BlockSpec((1, tt, 128), lambda bi, t: (bi, t, 0))] + [vec(a) for a in prm],
        out_specs=[pl.BlockSpec((1, tt, GROUP_WIDTH), lambda bi, t: (bi, t, 0)),
                   pl.BlockSpec((1, N_HEADS, HEAD_DIM, HEAD_DIM), lambda bi, t: (bi, 0, 0, 0))],
        out_shape=[jax.ShapeDtypeStruct((b, seq, GROUP_WIDTH), f32),
                   jax.ShapeDtypeStruct((b, N_HEADS, HEAD_DIM, HEAD_DIM), f32)],
        scratch_shapes=[pltpu.VMEM((N_HEADS, HEAD_DIM, HEAD_DIM), f32), pltpu.VMEM((tt + 8, C_QKV), f32)],
        compiler_params=_cparams("arbitrary", "arbitrary"),
        name="gdn_prompt",
    )(pc, pab, *prm)


def _block_means_body(k_ref, o_ref, *, nb):
    k = k_ref[0]
    o_ref[0] = jnp.sum(k.reshape(nb, MOBA_BLOCK, GROUP_WIDTH), axis=1) * (1.0 / MOBA_BLOCK)


def _block_means(k):
    b, seq, _ = k.shape
    nb = seq // MOBA_BLOCK
    return pl.pallas_call(
        functools.partial(_block_means_body, nb=nb),
        grid=(b,),
        in_specs=[pl.BlockSpec((1, seq, GROUP_WIDTH), lambda bi: (bi, 0, 0))],
        out_specs=pl.BlockSpec((1, nb, GROUP_WIDTH), lambda bi: (bi, 0, 0)),
        out_shape=jax.ShapeDtypeStruct((b, nb, GROUP_WIDTH), f32),
        compiler_params=_cparams("arbitrary"),
        name="moba_block_means",
    )(k)


def _top_blocks(gate, own, nb):
    blk = _iota2(gate.shape, 0).astype(f32)
    g = jnp.where(blk < own, gate, -jnp.inf)
    picks = []
    for _ in range(MOBA_TOPK):
        m = jnp.max(g, axis=0, keepdims=True)
        idx = jnp.min(jnp.where(g == m, blk, float(nb)), axis=0, keepdims=True)
        picks.append(jnp.where(m > -jnp.inf, idx, -1.0))
        g = jnp.where(blk == idx, -jnp.inf, g)
    return picks


def _moba_body(qt_ref, k_ref, vt_ref, mean_ref, o_ref, *, nb):
    i = pl.program_id(1)
    blk_w = MOBA_BLOCK
    means = mean_ref[0]
    hsl = [slice(h * HEAD_DIM, (h + 1) * HEAD_DIM) for h in range(N_HEADS)]
    qt = [qt_ref[0, h] * HEAD_DIM ** -0.5 for h in range(N_HEADS)]
    qtb = [(x * LOG2E).astype(bf16) for x in qt]
    picks =[_top_blocks(_dot(means[:, hsl[h]], qt[h], precision=HI), i.astype(f32), nb) for h in range(N_HEADS)]

    own_start = pl.multiple_of(i * blk_w, blk_w)
    k_own = k_ref[0, pl.ds(own_start, blk_w), :]
    causal = _iota2((blk_w, blk_w), 0) <= _iota2((blk_w, blk_w), 1)
    s_own = [jnp.where(causal, _dot(k_own[:, hsl[h]], qtb[h]), NEG) for h in range(N_HEADS)]
    m_own = [jnp.max(x, axis=0, keepdims=True) for x in s_own]
    p_own = [jnp.exp2(x - m).astype(bf16) for x, m in zip(s_own, m_own)]
    carry = []
    for h in range(N_HEADS):
        carry += [m_own[h], _dot(vt_ref[0, h, :, pl.ds(own_start, blk_w)], p_own[h])]

    def step(jj, carry):
        js = [2 * jj, 2 * jj + 1]
        starts = [pl.multiple_of(j * blk_w, blk_w) for j in js]
        kj = [k_ref[0, pl.ds(st, blk_w), :] for st in starts]
        units = [(h, e) for h in range(N_HEADS) for e in range(2)]
        s = [_dot(kj[e][:, hsl[h]], qtb[h]) for h, e in units]
        chosen = [functools.reduce(jnp.logical_or, [pk == js[e].astype(f32) for pk in picks[h]]) for h, e in units]
        m_blk = [jnp.where(c, jnp.max(x, axis=0, keepdims=True), NEG) for c, x in zip(chosen, s)]
        m_new = [jnp.maximum(carry[2 * h], jnp.maximum(m_blk[2 * h], m_blk[2 * h + 1])) for h in range(N_HEADS)]
        shift = [jnp.where(c, m_new[h], -NEG) for c, (h, e) in zip(chosen, units)]
        p = [jnp.exp2(x - sh).astype(bf16) for x, sh in zip(s, shift)]
        pv = [_dot(vt_ref[0, h, :, pl.ds(starts[e], blk_w)], x) for x, (h, e) in zip(p, units)]
        out = []
        for h in range(N_HEADS):
            out += [m_new[h], jnp.exp2(carry[2 * h] - m_new[h]) * carry[2 * h + 1] + (pv[2 * h] + pv[2 * h + 1])]
        return tuple(out)

    carry = lax.fori_loop(0, (i + 1) // 2, step, tuple(carry))
    for h in range(N_HEADS):
        acc = carry[2 * h + 1]
        o_ref[0, h] = acc[:HEAD_DIM] / acc[HEAD_DIM:HEAD_DIM + 1]


def _moba_prompt(q, k, v):
    b, seq, _ = q.shape
    nb = seq // MOBA_BLOCK
    assert seq % MOBA_BLOCK == 0
    means = _block_means(k)
    heads_t = lambda a: jnp.transpose(a.reshape(b, seq, N_HEADS, HEAD_DIM), (0, 2, 3, 1))
    vt = jnp.concatenate([heads_t(v), jnp.ones((b, N_HEADS, SUM_ROWS, seq), f32)], axis=2).astype(bf16)
    tile_t = pl.BlockSpec((1, N_HEADS, HEAD_DIM, MOBA_BLOCK), lambda bi, i: (bi, 0, 0, i))
    o_t = pl.pallas_call(
        functools.partial(_moba_body, nb=nb),
        grid=(b, nb),
        in_specs=[tile_t,
                  pl.BlockSpec((1, seq, GROUP_WIDTH), lambda bi, i: (bi, 0, 0)),
                  pl.BlockSpec((1, N_HEADS, HEAD_DIM + SUM_ROWS, seq), lambda bi, i: (bi, 0, 0, 0)),
                  pl.BlockSpec((1, nb, GROUP_WIDTH), lambda bi, i: (bi, 0, 0))],
        out_specs=tile_t,
        out_shape=jax.ShapeDtypeStruct((b, N_HEADS, HEAD_DIM, seq), f32),
        compiler_params=_cparams("arbitrary", "arbitrary"),
        name="moba_prompt",
    )(heads_t(q), k.astype(bf16), vt, means)
    return jnp.transpose(o_t, (0, 3, 1, 2)).reshape(b, seq, GROUP_WIDTH)


def _row_to_col(row, eye):
    return jnp.sum(eye * row, axis=1, keepdims=True)


def _col_to_row(col, eye):
    return jnp.sum(eye * col, axis=0, keepdims=True)


def _sample_mix_body(pa_ref, shift_ref, pb_ref, pc_ref, pab_ref, conv_ref, srw_ref, srt_ref, sgd_ref,
                     mu_ref, w0_ref, w2_ref, a0_ref, a2_ref, g2_ref, kk_ref, ka_ref, rk_ref, lnw_ref, lnb_ref,
                     cos_ref, sin_ref, eone_ref, cw_ref, alog_ref, dtb_ref, sela_ref, selb_ref, nw_ref,
                     ya_ref, yb_ref, yc_ref, srw_out, srt_out, sgd_out):
    seg = _head_seg()
    eye = _eye(HEAD_DIM)
    hsl = [slice(h * HEAD_DIM, (h + 1) * HEAD_DIM) for h in range(N_HEADS)]

    r, logw, k2, v, kk, a, g = _rwkv_pre(pa_ref[0], shift_ref[0], mu_ref[...], w0_ref[...], w2_ref[...], a0_ref[...],
                                         a2_ref[...], g2_ref[...], kk_ref[...], ka_ref[...], seg)
    w = jnp.exp(logw)
    kka = kk * a
    ys = []
    for h, hs in enumerate(hsl):
        s_old = srw_ref[0, h]
        sa = jnp.sum(s_old * (-kk[:, hs]), axis=1, keepdims=True)
        s_new = s_old * w[:, hs] + sa * kka[:, hs] + _row_to_col(v[:, hs], eye) * k2[:, hs]
        srw_out[0, h] = s_new
        ys.append(_col_to_row(jnp.sum(s_new * r[:, hs], axis=1, keepdims=True), eye))
    ya_ref[0] = _rwkv_post(jnp.concatenate(ys, axis=1), r, k2, v, g, rk_ref[...], lnw_ref[...], lnb_ref[...], seg)

    pb = pb_ref[0]
    q = _xpos(pb[:, 0:GROUP_WIDTH], cos_ref[...], sin_ref[...])
    k = _xpos(pb[:, GROUP_WIDTH:2 * GROUP_WIDTH], cos_ref[...], sin_ref[...]) * HEAD_DIM ** -0.5
    v = pb[:, 2 * GROUP_WIDTH:3 * GROUP_WIDTH]
    e_one = eone_ref[...]
    os_ = []
    for h, hs in enumerate(hsl):
        s_new = srt_ref[0, h] * e_one[:, hs] + _row_to_col(k[:, hs], eye) * v[:, hs]
        srt_out[0, h] = s_new
        os_.append(jnp.sum(_row_to_col(q[:, hs], eye) * s_new, axis=0, keepdims=True))
    o = jnp.concatenate(os_, axis=1)
    o = o * lax.rsqrt(_segsum(o * o, seg) * (1.0 / HEAD_DIM) + NORM_EPS)
    yb_ref[0] = _silu(pb[:, 3 * GROUP_WIDTH:]) * o

    pc = pc_ref[0]
    cw = cw_ref[...]
    prev = conv_ref[0]
    conv = pc[:, :C_QKV] * cw[GDN_CONV - 1:GDN_CONV, :]
    for j in range(GDN_CONV - 1):
        conv = conv + prev[j:j + 1, :] * cw[j:j + 1, :]
    q, k, v = _gdn_qkv(conv, seg)
    logd, beta = _gdn_scalars(pab_ref[0], alog_ref[...], dtb_ref[...], sela_ref[...], selb_ref[...])
    alpha = jnp.exp(logd)
    os_ = []
    for h, hs in enumerate(hsl):
        s_old = sgd_ref[0, h]
        k_col = _row_to_col(k[:, hs], eye)
        ks = jnp.sum(k_col * s_old, axis=0, keepdims=True)
        s_new = alpha[:, hs] * s_old + k_col * (beta[:, hs] * (v[:, hs] - alpha[:, hs] * ks))
        sgd_out[0, h] = s_new
        os_.append(jnp.sum(_row_to_col(q[:, hs], eye) * s_new, axis=0, keepdims=True))
    o = jnp.concatenate(os_, axis=1)
    yc_ref[0] = o * lax.rsqrt(_segsum(o * o, seg) * (1.0 / HEAD_DIM) + NORM_EPS) * nw_ref[...] * _silu(pc[:, C_QKV:])


def _sample_mixers(pa, shift, pb, pc, pab, conv_prev, s_rwkv, s_ret, s_gdn, rwkv_prm, ret_prm, gdn_prm):
    nseq = pa.shape[0]
    row = lambda a: pl.BlockSpec((1, 1, a.shape[-1]), lambda b: (b, 0, 0))
    state = pl.BlockSpec((1, N_HEADS, HEAD_DIM, HEAD_DIM), lambda b: (b, 0, 0, 0))
    full = lambda a: pl.BlockSpec(a.shape, lambda b: (0,) * a.ndim)
    r3 = lambda a: a.reshape(nseq, 1, a.shape[-1])
    prm = tuple(rwkv_prm) + tuple(ret_prm) + tuple(gdn_prm)
    outs = pl.pallas_call(
        _sample_mix_body,
        grid=(nseq,),
        in_specs=[row(pa), row(shift), row(pb), row(pc), row(pab),
                  pl.BlockSpec((1, GDN_CONV - 1, C_QKV), lambda b: (b, 0, 0)), state, state, state]
                 + [full(a) for a in prm],
        out_specs=[row(pa[..., :GROUP_WIDTH])] * 3 + [state] * 3,
        out_shape=[jax.ShapeDtypeStruct((nseq, 1, GROUP_WIDTH), f32)] * 3
                  + [jax.ShapeDtypeStruct((nseq, N_HEADS, HEAD_DIM, HEAD_DIM), f32)] * 3,
        compiler_params=_cparams("arbitrary"),
        name="sample_mixers",
    )(r3(pa), r3(shift), r3(pb), r3(pc), r3(pab), conv_prev, s_rwkv, s_ret, s_gdn, *prm)
    return outs


def _moba_sample_body(pt_ref, q_ref, qb_ref, kn_ref, vn_ref, *refs, nb, bps):
    del pt_ref
    npg = bps * PAGES_PER_BLOCK
    k_refs, v_refs = refs[:npg], refs[npg:2 * npg]
    o_ref, gate_scr, m_scr, l_scr, acc_scr = refs[2 * npg:]
    j = pl.program_id(1)
    hmask = (_iota2((N_HEADS, GROUP_WIDTH), 1) // HEAD_DIM == _iota2((N_HEADS, GROUP_WIDTH), 0)).astype(f32)
    qb = qb_ref[0]
    qb = jnp.concatenate([qb] * (MOBA_BLOCK // 128), axis=1)
    block_t = lambda page_refs: jnp.concatenate(
        [r[...].reshape(GROUP_WIDTH, PAGE_SIZE) for r in page_refs], axis=1)
    blocks = [slice(e * PAGES_PER_BLOCK, (e + 1) * PAGES_PER_BLOCK) for e in range(bps)]
    kt = [block_t(k_refs[pg]) for pg in blocks]
    s = [jnp.sum((x * qb).reshape(N_HEADS, HEAD_DIM, MOBA_BLOCK), axis=1) for x in kt]
    gate = [jnp.sum(x, axis=1, keepdims=True) * (1.0 / MOBA_BLOCK) for x in s]
    m = [jnp.max(x, axis=1, keepdims=True) for x in s]
    p = [jnp.exp(x - mx) for x, mx in zip(s, m)]
    pv = [_dot_nt(x, block_t(v_refs[pg])) for x, pg in zip(p, blocks)]
    for e in range(bps):
        slot = j * bps + e
        gate_scr[slot] = jnp.broadcast_to(gate[e], (N_HEADS, 128))
        m_scr[slot] = jnp.broadcast_to(m[e], (N_HEADS, 128))
        l_scr[slot] = jnp.broadcast_to(jnp.sum(p[e], axis=1, keepdims=True), (N_HEADS, 128))
        acc_scr[slot] = pv[e]

    @pl.when(j == nb // bps - 1)
    def _():
        g = gate_scr[...]
        blk = _iota2(g.shape, 0).astype(f32)
        sel = jnp.zeros(g.shape, jnp.bool_)
        for _ in range(min(MOBA_TOPK, nb)):
            gm = jnp.max(g, axis=0, keepdims=True)
            idx = jnp.min(jnp.where(g == gm, blk, float(nb)), axis=0, keepdims=True)
            pick = blk == idx
            sel = jnp.logical_or(sel, pick)
            g = jnp.where(pick, -jnp.inf, g)
        q_own = hmask * (q_ref[0] * HEAD_DIM ** -0.5)
        s_own = jnp.broadcast_to(jnp.sum(q_own * kn_ref[0], axis=1, keepdims=True), (N_HEADS, 128))
        mb = jnp.where(sel, m_scr[...], NEG)
        m_all = jnp.maximum(jnp.max(mb, axis=0), s_own)
        wgt = jnp.where(sel, jnp.exp(mb - m_all[None]), 0.0)
        w_own = jnp.exp(s_own - m_all)
        denom = jnp.sum(wgt * l_scr[...], axis=0) + w_own
        wide = lambda a: jnp.concatenate([a, a], axis=-1)
        num = jnp.sum(wide(wgt) * acc_scr[...], axis=0) + wide(w_own) * vn_ref[0]
        o_ref[0] = jnp.sum(hmask * (num / wide(denom)), axis=0, keepdims=True)


def _moba_sample(q, k_new, v_new, cache_kt, cache_vt, page_table, layer, bps):
    nseq, n_pages = page_table.shape
    nb = n_pages // PAGES_PER_BLOCK
    assert n_pages % PAGES_PER_BLOCK == 0 and nb % bps == 0
    npg = bps * PAGES_PER_BLOCK
    row = pl.BlockSpec((1, 1, GROUP_WIDTH), lambda b, j, pt: (b, 0, 0))
    page = lambda e: pl.BlockSpec((None, None, N_HEADS, HEAD_DIM, PAGE_SIZE),
                                  lambda b, j, pt: (layer, pt[b, npg * j + e], 0, 0, 0))
    pages = [page(e) for e in range(npg)]
    r3 = lambda a: a.reshape(nseq, 1, GROUP_WIDTH)
    q_lanes = jnp.broadcast_to((q * HEAD_DIM ** -0.5)[:, :, None], (nseq, GROUP_WIDTH, 128))
    return pl.pallas_call(
        functools.partial(_moba_sample_body, nb=nb, bps=bps),
        grid_spec=pltpu.PrefetchScalarGridSpec(
            num_scalar_prefetch=1,
            grid=(nseq, nb // bps),
            in_specs=[row, pl.BlockSpec((1, GROUP_WIDTH, 128), lambda b, j, pt: (b, 0, 0)), row, row] + pages + pages,
            out_specs=row,
            scratch_shapes=[pltpu.VMEM((nb, N_HEADS, 128), f32)] * 3 + [pltpu.VMEM((nb, N_HEADS, GROUP_WIDTH), f32)]),
        out_shape=jax.ShapeDtypeStruct((nseq, 1, GROUP_WIDTH), f32),
        compiler_params=_cparams("arbitrary", "arbitrary"),
        name="moba_sample",
    )(page_table, r3(q), q_lanes, r3(k_new), r3(v_new), *([cache_kt] * npg), *([cache_vt] * npg)
      ).reshape(nseq, GROUP_WIDTH)


def _pad_rows(w, start, total):
    return jnp.zeros((total, w.shape[1]), w.dtype).at[start:start + w.shape[0]].set(w)


def _mixer_params(l, rwkv_mu, rwkv_w0, rwkv_w2, rwkv_a0, rwkv_a2, rwkv_g2, rwkv_k_k, rwkv_k_a, rwkv_r_k, rwkv_ln_w,
                  rwkv_ln_b, gdn_conv_w, gdn_a_log, gdn_dt_bias, gdn_norm_w):
    row = lambda a: a.reshape(1, -1)
    rwkv = (row(rwkv_mu[l]), row(rwkv_w0[l]), _pad_rows(rwkv_w2[l], 0, LORA_W), row(rwkv_a0[l]),
            _pad_rows(rwkv_a2[l], RWKV_W_LORA, LORA_W), _pad_rows(rwkv_g2[l], RWKV_W_LORA + RWKV_A_LORA, LORA_W),
            row(rwkv_k_k[l]), row(rwkv_k_a[l]), row(rwkv_r_k[l]), row(rwkv_ln_w[l]), row(rwkv_ln_b[l]))
    lane = np.arange(128)[:, None]
    col_head = (np.arange(GROUP_WIDTH) // HEAD_DIM)[None, :]
    sel_a = jnp.asarray(lane == col_head, f32)
    sel_b = jnp.asarray(lane == col_head + N_HEADS, f32)
    pad128 = lambda a: jnp.zeros((1, 128), f32).at[0, :N_HEADS].set(a)
    gdn = (gdn_conv_w[l], pad128(gdn_a_log[l]), pad128(gdn_dt_bias[l]), sel_a, sel_b,
           jnp.tile(gdn_norm_w[l], N_HEADS).reshape(1, GROUP_WIDTH))
    return rwkv, gdn


def kernel(x_prompt, x_sample, c_prompt, c_sample, cache_k, cache_v, page_table, state_rwkv, state_rwkv_shift,
           state_ret, state_gdn, state_gdn_conv, w_mod, b_mod, norm_pre, norm_post, ffn_w_gate, ffn_w_up, ffn_w_down,
           w_in, w_out, rwkv_mu, rwkv_w0, rwkv_w2, rwkv_a0, rwkv_a2, rwkv_g2, rwkv_k_k, rwkv_k_a, rwkv_r_k, rwkv_ln_w,
           rwkv_ln_b, gdn_conv_w, gdn_a_log, gdn_dt_bias, gdn_norm_w):
    depth = w_in.shape[0]
    bp, seq, d = x_prompt.shape
    bs = x_sample.shape[0]
    assert x_sample.shape[1] == 1
    past_len = page_table.shape[1] * PAGE_SIZE
    assert past_len % MOBA_BLOCK == 0
    cache_kt = jnp.transpose(cache_k, (0, 1, 3, 4, 2))
    cache_vt = jnp.transpose(cache_v, (0, 1, 3, 4, 2))

    n_c = bp + bs
    c_all = jnp.concatenate([c_prompt, c_sample, jnp.zeros((-n_c % 8, d), f32)], axis=0)
    mods = _modulation(c_all, w_mod, b_mod)

    wg, wu, wdn = ffn_w_gate.astype(bf16), ffn_w_up.astype(bf16), ffn_w_down.astype(bf16)
    w_in_b, w_out_b = w_in.astype(bf16), w_out.astype(bf16)
    a0, b0 = A_COLS, A_COLS + 4 * GROUP_WIDTH
    c0 = b0 + C_QKV + GROUP_WIDTH
    d0 = c0 + 2 * N_HEADS
    col_groups = [(0, a0), (a0, b0), (b0, c0), (d0, d0 + GROUP_WIDTH), (d0 + GROUP_WIDTH, d0 + 2 * GROUP_WIDTH),
                  (d0 + 2 * GROUP_WIDTH, d0 + 3 * GROUP_WIDTH)]

    ws = [w_in_b[:, :, lo:hi] for lo, hi in col_groups]
    ws.append(jnp.pad(w_in_b[:, :, c0:d0], ((0, 0), (0, 0), (0, 128 - 2 * N_HEADS))))
    cos_p, sin_p, ret_tabs = _ret_tables(seq, 0)
    cos_s, sin_s, _ = _ret_tables(1, past_len)
    tm = 512
    tm_ffn = 1024
    tt = 256

    xp, xs = x_prompt, x_sample.reshape(1, bs, d)
    news_p, news_s = [], []
    for l in range(depth):
        mp = [m.reshape(bp, 1, d) for m in jnp.split(mods[l, :bp], 9, axis=-1)]
        ms = [m.reshape(1, bs, d) for m in jnp.split(mods[l, bp:n_c], 9, axis=-1)]
        npre = [norm_pre[l, i].reshape(1, d) for i in range(3)]
        npost = [norm_post[l, i].reshape(1, d) for i in range(3)]
        rwkv_prm, gdn_prm = _mixer_params(l, rwkv_mu, rwkv_w0, rwkv_w2, rwkv_a0, rwkv_a2, rwkv_g2, rwkv_k_k, rwkv_k_a,
                                          rwkv_r_k, rwkv_ln_w, rwkv_ln_b, gdn_conv_w, gdn_a_log, gdn_dt_bias, gdn_norm_w)

        xp = _ffn(xp, mp[0], mp[1], mp[2], npre[0], npost[0], wg, wu, wdn, l, 0, tm_ffn, FFN_TF)
        pa, pb, pc, qd, kd, vd, pab = _inproj(xp, mp[3], mp[4], npre[1], ws, l, tm)
        ya, rw_new = _rwkv_prompt(pa, rwkv_prm, tt)
        yb, rt_new = _ret_prompt(pb, cos_p, sin_p, ret_tabs, tt)
        yc, gd_new = _gdn_prompt(pc, pab, gdn_prm, tt)
        yd = _moba_prompt(qd, kd, vd)
        xp = _outproj(xp, (ya, yb, yc, yd), mp[5], npost[1], w_out_b, l, tm)
        xp = _ffn(xp, mp[6], mp[7], mp[8], npre[2], npost[2], wg, wu, wdn, l, 1, tm_ffn, FFN_TF)
        hd = lambda a: a.reshape(a.shape[0], a.shape[1], N_HEADS, HEAD_DIM)
        news_p.append((hd(kd), hd(vd), rw_new, pa[:, -1], rt_new, gd_new, pc[:, seq - (GDN_CONV - 1):, :C_QKV]))

        xs = _ffn(xs, ms[0], ms[1], ms[2], npre[0], npost[0], wg, wu, wdn, l, 0, bs, FFN_TF)
        pa, pb, pc, qd, kd, vd, pab = (a[0] for a in _inproj(xs, ms[3], ms[4], npre[1], ws, l, bs))
        ya, yb, yc, rw_new, rt_new, gd_new = _sample_mixers(
            pa, state_rwkv_shift[l], pb, pc, pab, state_gdn_conv[l], state_rwkv[l], state_ret[l], state_gdn[l],
            rwkv_prm, (cos_s, sin_s, ret_tabs[4]), gdn_prm)
        yd = _moba_sample(qd, kd, vd, cache_kt, cache_vt, page_table, l, 4)
        ys = tuple(a.reshape(1, bs, GROUP_WIDTH) for a in (ya, yb, yc, yd))
        xs = _outproj(xs, ys, ms[5], npost[1], w_out_b, l, bs)
        xs = _ffn(xs, ms[6], ms[7], ms[8], npre[2], npost[2], wg, wu, wdn, l, 1, bs, FFN_TF)
        hs_ = lambda a: a.reshape(bs, 1, N_HEADS, HEAD_DIM)
        conv_new = jnp.concatenate([state_gdn_conv[l][:, 1:], pc[:, None, :C_QKV]], axis=1)
        news_s.append((hs_(kd), hs_(vd), rw_new, pa, rt_new, gd_new, conv_new))

    kp, vp, rwp, shp, rtp, gdp, cvp = [jnp.stack(t) for t in zip(*news_p)]
    ks, vs, rws, shs, rts, gds, cvs = [jnp.stack(t) for t in zip(*news_s)]
    return (xp, xs.reshape(bs, 1, d), kp, vp, ks, vs, rwp, rws, shp, shs, rtp, rts, gdp, gds, cvp, cvs)
```

```python
import functools
import math

import numpy as np
import jax
import jax.numpy as jnp
from jax import lax
from jax.experimental import pallas as pl
from jax.experimental.pallas import tpu as pltpu

f32 = jnp.float32
bf16 = jnp.bfloat16
HI = lax.Precision.HIGHEST

GROUP_WIDTH = 256
HEAD_DIM = 64
N_HEADS = 4
FFN_RES = 0.5
NORM_EPS = 1e-6
RWKV_W_LORA = 32
RWKV_A_LORA = 32
RWKV_G_LORA = 64
RWKV_LN_EPS = 64e-5
RET_THETA = 10000.0
GDN_CONV = 4
MOBA_BLOCK = 256
MOBA_TOPK = 3
SUM_ROWS = 16
PAGE_SIZE = 128
PAGES_PER_BLOCK = MOBA_BLOCK // PAGE_SIZE
CHUNK = 64
PIPE_GROUP = 4
FFN_TF = 256
INV_BLOCK = 16
LORA_W = RWKV_W_LORA + RWKV_A_LORA + RWKV_G_LORA
A_COLS = 3 * GROUP_WIDTH + LORA_W
C_QKV = 3 * GROUP_WIDTH
NEG = -1e30
LOG2E = math.log2(math.e)
VMEM_LIMIT = 56 * 1024 * 1024


def _cparams(*sem):
    return pltpu.CompilerParams(dimension_semantics=sem, vmem_limit_bytes=VMEM_LIMIT)


def _dot(a, b, precision=None):
    return jnp.dot(a, b, preferred_element_type=f32, precision=precision)


def _dot_nt(a, b, precision=None):
    return lax.dot_general(a, b, (((1,), (1,)), ((), ())), preferred_element_type=f32, precision=precision)


def _dot_tn(a, b, precision=None):
    return lax.dot_general(a, b, (((0,), (0,)), ((), ())), preferred_element_type=f32, precision=precision)


def _rms(x, g):
    return x * lax.rsqrt(jnp.mean(x * x, axis=-1, keepdims=True) + NORM_EPS) * g


def _silu(x):
    return x * jax.nn.sigmoid(x)


def _softplus(x):
    return jnp.maximum(x, 0.0) + jnp.log(1.0 + jnp.exp(-jnp.abs(x)))


def _iota2(shape, axis):
    return lax.broadcasted_iota(jnp.int32, shape, axis)


def _head_seg():
    return (_iota2((GROUP_WIDTH, GROUP_WIDTH), 0) // HEAD_DIM == _iota2((GROUP_WIDTH, GROUP_WIDTH), 1) // HEAD_DIM).astype(f32)


def _segsum(x, seg):
    return _dot(x, seg, precision=HI)


def _eye(n):
    return (_iota2((n, n), 0) == _iota2((n, n), 1)).astype(f32)


def _split(x):
    hi = x.astype(bf16)
    return hi, (x - hi.astype(f32)).astype(bf16)


def _dot3(a, b):
    return _dot(a[0], b[0]) + (_dot(a[0], b[1]) + _dot(a[1], b[0]))


def _bdot(a, b):
    return _dot(a.astype(bf16), b.astype(bf16))


def _bdot_nt(a, b):
    return _dot_nt(a.astype(bf16), b.astype(bf16))


def _bdot_tn(a, b):
    return _dot_tn(a.astype(bf16), b.astype(bf16))


def _nilpotent_inverse(ns, eye, index):
    ts = [eye + n for n in ns]
    ps = [_split(n) for n in ns]
    for _ in range(int(math.log2(index)) - 1):
        ps = [_split(_dot3(p, p)) for p in ps]
        tsp = [_split(t) for t in ts]
        ts = [t + _dot3(p, tp) for t, p, tp in zip(ts, ps, tsp)]
    return ts


def _unit_lower_inverse(ns, c):
    eye = _eye(c)
    same = _iota2((c, c), 0) // INV_BLOCK == _iota2((c, c), 1) // INV_BLOCK
    t_diag = _nilpotent_inverse([jnp.where(same, n, 0.0) for n in ns], eye, INV_BLOCK)
    tds = [t.astype(bf16) for t in t_diag]
    ps = [_dot(td, jnp.where(same, 0.0, n).astype(bf16)) for td, n in zip(tds, ns)]
    t_off = [eye + p for p in ps]
    pw = [p.astype(bf16) for p in ps]
    for _ in range(int(math.log2(c // INV_BLOCK)) - 1):
        pw = [_dot(p, p).astype(bf16) for p in pw]
        t_off = [t + _dot(p, t.astype(bf16)) for t, p in zip(t_off, pw)]
    return [_dot(to.astype(bf16), td) for to, td in zip(t_off, tds)]


def _mod_body(c_ref, w_ref, b_ref, o_ref):
    s = _silu(c_ref[...]).astype(bf16)
    o_ref[0] = _dot(s, w_ref[0].astype(bf16)) + b_ref[0]


def _modulation(c_all, w_mod, b_mod):
    depth, d, n9 = w_mod.shape
    rows = c_all.shape[0]
    tn = 1152
    return pl.pallas_call(
        _mod_body,
        grid=(depth, n9 // tn),
        in_specs=[pl.BlockSpec((rows, d), lambda l, j: (0, 0)),
                  pl.BlockSpec((1, d, tn), lambda l, j: (l, 0, j)),
                  pl.BlockSpec((1, 1, tn), lambda l, j: (l, 0, j))],
        out_specs=pl.BlockSpec((1, rows, tn), lambda l, j: (l, 0, j)),
        out_shape=jax.ShapeDtypeStruct((depth, rows, n9), f32),
        compiler_params=_cparams("arbitrary", "arbitrary"),
        name="modulation",
    )(c_all, w_mod, b_mod.reshape(depth, 1, n9))


def _ffn_body(x_ref, sh_ref, sc_ref, g_ref, npre_ref, npost_ref, wg_ref, wu_ref, wd_ref, o_ref, *, tf):
    x = x_ref[0]
    h = (_rms(x, npre_ref[...]) * (1.0 + sc_ref[0]) + sh_ref[0]).astype(bf16)
    acc = None
    for j in range(wg_ref.shape[1] // tf):
        cs = slice(j * tf, (j + 1) * tf)
        act = (_silu(_dot(h, wg_ref[:, cs])) * _dot(h, wu_ref[:, cs])).astype(bf16)
        part = _dot(act, wd_ref[cs, :])
        acc = part if acc is None else acc + part
    o_ref[0] = x + FFN_RES * g_ref[0] * _rms(acc, npost_ref[...])


def _ffn(x, sh, sc, g, npre, npost, wg, wu, wd, l, i, tm, tf):
    b, seq, d = x.shape
    r = sh.shape[1]
    dff = wg.shape[-1]
    mod_spec = pl.BlockSpec((1, r, d), lambda bi, t: (bi, 0 if r == 1 else t, 0))
    vec_spec = pl.BlockSpec((1, d), lambda bi, t: (0, 0))
    once = pl.Buffered(1)
    return pl.pallas_call(
        functools.partial(_ffn_body, tf=tf),
        grid=(b, seq // tm),
        in_specs=[pl.BlockSpec((1, tm, d), lambda bi, t: (bi, t, 0)),
                  mod_spec, mod_spec, mod_spec, vec_spec, vec_spec,
                  pl.BlockSpec((None, None, d, dff), lambda bi, t: (l, i, 0, 0), pipeline_mode=once),
                  pl.BlockSpec((None, None, d, dff), lambda bi, t: (l, i, 0, 0), pipeline_mode=once),
                  pl.BlockSpec((None, None, dff, d), lambda bi, t: (l, i, 0, 0), pipeline_mode=once)],
        out_specs=pl.BlockSpec((1, tm, d), lambda bi, t: (bi, t, 0)),
        out_shape=jax.ShapeDtypeStruct(x.shape, f32),
        compiler_params=_cparams("arbitrary", "arbitrary"),
        name="ffn_halfstep",
    )(x, sh, sc, g, npre, npost, wg, wu, wd)


def _inproj_body(x_ref, sh_ref, sc_ref, npre_ref, *refs):
    n = len(refs) // 2
    h = (_rms(x_ref[0], npre_ref[...]) * (1.0 + sc_ref[0]) + sh_ref[0]).astype(bf16)
    for w_ref, o_ref in zip(refs[:n], refs[n:]):
        o_ref[0] = _dot(h, w_ref[...])


def _inproj(x, sh, sc, npre, ws, l, tm):
    b, seq, d = x.shape
    r = sh.shape[1]
    mod_spec = pl.BlockSpec((1, r, d), lambda bi, t: (bi, 0 if r == 1 else t, 0))
    return pl.pallas_call(
        _inproj_body,
        grid=(b, seq // tm),
        in_specs=[pl.BlockSpec((1, tm, d), lambda bi, t: (bi, t, 0)), mod_spec, mod_spec,
                  pl.BlockSpec((1, d), lambda bi, t: (0, 0))]
                 + [pl.BlockSpec((None,) + w.shape[1:], lambda bi, t: (l, 0, 0)) for w in ws],
        out_specs=[pl.BlockSpec((1, tm, w.shape[2]), lambda bi, t: (bi, t, 0)) for w in ws],
        out_shape=[jax.ShapeDtypeStruct((b, seq, w.shape[2]), f32) for w in ws],
        compiler_params=_cparams("arbitrary", "arbitrary"),
        name="mixer_inproj",
    )(x, sh, sc, npre, *ws)


def _outproj_body(x_ref, ya_ref, yb_ref, yc_ref, yd_ref, g_ref, npost_ref, w_ref, o_ref):
    y = jnp.concatenate([ya_ref[0], yb_ref[0], yc_ref[0], yd_ref[0]], axis=-1).astype(bf16)
    o_ref[0] = x_ref[0] + g_ref[0] * _rms(_dot(y, w_ref[...]), npost_ref[...])


def _outproj(x, ys, g, npost, w, l, tm):
    b, seq, d = x.shape
    r = g.shape[1]
    y_spec = pl.BlockSpec((1, tm, GROUP_WIDTH), lambda bi, t: (bi, t, 0))
    return pl.pallas_call(
        _outproj_body,
        grid=(b, seq // tm),
        in_specs=[pl.BlockSpec((1, tm, d), lambda bi, t: (bi, t, 0)), y_spec, y_spec, y_spec, y_spec,
                  pl.BlockSpec((1, r, d), lambda bi, t: (bi, 0 if r == 1 else t, 0)),
                  pl.BlockSpec((1, d), lambda bi, t: (0, 0)),
                  pl.BlockSpec((None,) + w.shape[1:], lambda bi, t: (l, 0, 0))],
        out_specs=pl.BlockSpec((1, tm, d), lambda bi, t: (bi, t, 0)),
        out_shape=jax.ShapeDtypeStruct(x.shape, f32),
        compiler_params=_cparams("arbitrary", "arbitrary"),
        name="mixer_outproj",
    )(x, *ys, g, npost, w)


def _rwkv_pre(p, prev, mu, w0, w2p, a0, a2p, g2p, k_k, k_a, seg):
    xs = p + (prev - p) * mu
    r = xs[:, 0:GROUP_WIDTH]
    k = xs[:, GROUP_WIDTH:2 * GROUP_WIDTH]
    v = xs[:, 2 * GROUP_WIDTH:3 * GROUP_WIDTH]
    lo = xs[:, 3 * GROUP_WIDTH:]
    w_log = -_softplus(-(w0 + _dot(jnp.tanh(lo), w2p))) - 0.5
    logw = -jnp.exp(w_log)
    a = jax.nn.sigmoid(a0 + _dot(lo, a2p))
    g = _dot(jax.nn.sigmoid(lo), g2p)
    kk = k * k_k
    kk = kk * lax.rsqrt(_segsum(kk * kk, seg) + 1e-6)
    k2 = k * (1.0 + (a - 1.0) * k_a)
    return r, logw, k2, v, kk, a, g


def _rwkv_post(y, r, k2, v, g, r_k, ln_w, ln_b, seg):
    mean = _segsum(y, seg) * (1.0 / HEAD_DIM)
    d = y - mean
    var = _segsum(d * d, seg) * (1.0 / HEAD_DIM)
    yn = d * lax.rsqrt(var + RWKV_LN_EPS) * ln_w + ln_b
    bonus = _segsum(r * k2 * r_k, seg) * v
    return (yn + bonus) * g


def _rwkv_body(p_ref, mu_ref, w0_ref, w2_ref, a0_ref, a2_ref, g2_ref, kk_ref, ka_ref, rk_ref, lnw_ref, lnb_ref,
               y_ref, s_out_ref, s_scr, carry_scr, *, tt):
    c = CHUNK
    t = pl.program_id(1)

    @pl.when(t == 0)
    def _():
        s_scr[...] = jnp.zeros_like(s_scr)
        carry_scr[...] = jnp.zeros_like(carry_scr)

    seg = _head_seg()
    p = p_ref[0]
    row = _iota2((tt, 1), 0)
    prev = jnp.where(row == 0, carry_scr[7:8, :], pltpu.roll(p, 1, axis=0))
    carry_scr[...] = p[tt - 8:tt, :]
    r, logw, k2, v, kk, a, g = _rwkv_pre(p, prev, mu_ref[...], w0_ref[...], w2_ref[...], a0_ref[...], a2_ref[...],
                                         g2_ref[...], kk_ref[...], ka_ref[...], seg)
    kka = kk * a
    ri, ci = _iota2((c, c), 0), _iota2((c, c), 1)
    incl, strict = ri >= ci, ri > ci
    lmat = incl.astype(f32)

    n_ch = tt // c
    hsl = [slice(h * HEAD_DIM, (h + 1) * HEAD_DIM) for h in range(N_HEADS)]
    vb = v.astype(bf16)

    def phase1(chs):
        pairs = [(ch, h) for ch in chs for h in range(N_HEADS)]
        at, rt, kt, bt, res = {}, {}, {}, {}, {}
        for ch in chs:
            sl = slice(ch * c, (ch + 1) * c)
            lw = logw[sl]
            gcum = _dot(lmat, lw, precision=HI)
            g_last = gcum[c - 1:c, :]
            e_ng = jnp.exp(-gcum)
            e_rem = jnp.exp(g_last - gcum)
            res["e_last", ch] = jnp.exp(g_last)
            rt[ch] = (r[sl] * jnp.exp(gcum)).astype(bf16)
            at[ch] = (-kk[sl] * jnp.exp(gcum - lw)).astype(bf16)
            kt[ch] = (k2[sl] * e_ng).astype(bf16)
            bt[ch] = (kka[sl] * e_ng).astype(bf16)
            res["kh", ch] = (k2[sl] * e_rem).astype(bf16)
            res["bh", ch] = (kka[sl] * e_rem).astype(bf16)
            res["rt", ch] = rt[ch]
        amat = [_dot_nt(jnp.concatenate([at[ch][:, hsl[h]], rt[ch][:, hsl[h]]], axis=0),
                        jnp.concatenate([kt[ch][:, hsl[h]], bt[ch][:, hsl[h]]], axis=0)) for ch, h in pairs]
        vh = [vb[ch * c:(ch + 1) * c, hsl[h]] for ch, h in pairs]
        tmat = _unit_lower_inverse([jnp.where(strict, m[:c, c:], 0.0) for m in amat], c)
        akv = [_dot(jnp.where(strict, m[:c, :c], 0.0).astype(bf16), x) for m, x in zip(amat, vh)]
        rkv = [_dot(jnp.where(incl, m[c:, :c], 0.0).astype(bf16), x) for m, x in zip(amat, vh)]
        tx = [_dot(tm.astype(bf16), jnp.concatenate([at[ch][:, hsl[h]], x.astype(bf16)], axis=1))
              for tm, x, (ch, h) in zip(tmat, akv, pairs)]
        for i, p in enumerate(pairs):
            res["vh", p], res["rkv", p], res["tx", p] = vh[i], rkv[i], tx[i]
            res["a_rb", p] = jnp.where(incl, amat[i][c:, c:], 0.0).astype(bf16)
        return res

    def phase2(chs, res, state):
        for ch in chs:
            heads = range(N_HEADS)
            sb = [x.astype(bf16) for x in state]
            u = [_dot_nt(res["tx", (ch, h)][:, :HEAD_DIM].astype(bf16), sb[h]) + res["tx", (ch, h)][:, HEAD_DIM:]
                 for h in heads]
            ub = [x.astype(bf16) for x in u]
            y_chunks.append(jnp.concatenate(
                [_dot_nt(res["rt", ch][:, hsl[h]], sb[h]) + res["rkv", (ch, h)] + _dot(res["a_rb", (ch, h)], ub[h])
                 for h in heads], axis=1))
            state = [state[h] * res["e_last", ch][:, hsl[h]]
                     + _dot_tn(jnp.concatenate([res["vh", (ch, h)], ub[h]], axis=0),
                               jnp.concatenate([res["kh", ch][:, hsl[h]], res["bh", ch][:, hsl[h]]], axis=0))
                     for h in heads]
        return state

    groups = [list(range(g, min(g + PIPE_GROUP, n_ch))) for g in range(0, n_ch, PIPE_GROUP)]
    state = [s_scr[h] for h in range(N_HEADS)]
    y_chunks = []
    pending = None
    for grp in groups:
        res = phase1(grp)
        if pending is not None:
            state = phase2(pending[0], pending[1], state)
        pending = (grp, res)
    state = phase2(pending[0], pending[1], state)
    for h in range(N_HEADS):
        s_scr[h] = state[h]
    y = jnp.concatenate(y_chunks, axis=0)
    y_ref[0] = _rwkv_post(y, r, k2, v, g, rk_ref[...], lnw_ref[...], lnb_ref[...], seg)
    s_out_ref[0] = s_scr[...]


def _rwkv_prompt(pa, prm, tt):
    b, seq, _ = pa.shape
    vec = lambda a: pl.BlockSpec(a.shape, lambda bi, t: (0, 0))
    return pl.pallas_call(
        functools.partial(_rwkv_body, tt=tt),
        grid=(b, seq // tt),
        in_specs=[pl.BlockSpec((1, tt, A_COLS), lambda bi, t: (bi, t, 0))] + [vec(a) for a in prm],
        out_specs=[pl.BlockSpec((1, tt, GROUP_WIDTH), lambda bi, t: (bi, t, 0)),
                   pl.BlockSpec((1, N_HEADS, HEAD_DIM, HEAD_DIM), lambda bi, t: (bi, 0, 0, 0))],
        out_shape=[jax.ShapeDtypeStruct((b, seq, GROUP_WIDTH), f32),
                   jax.ShapeDtypeStruct((b, N_HEADS, HEAD_DIM, HEAD_DIM), f32)],
        scratch_shapes=[pltpu.VMEM((N_HEADS, HEAD_DIM, HEAD_DIM), f32), pltpu.VMEM((8, A_COLS), f32)],
        compiler_params=_cparams("arbitrary", "arbitrary"),
        name="rwkv7_prompt",
    )(pa, *prm)


def _xpos(x, cos, sin_signed):
    n = x.shape[-1]
    even = (_iota2(x.shape, 1) % 2) == 0
    partner = jnp.where(even, pltpu.roll(x, n - 1, axis=1), pltpu.roll(x, 1, axis=1))
    return x * cos + partner * sin_signed


def _ret_body(p_ref, cos_ref, sin_ref, dmask_ref, eg_ref, erem_ref, elast_ref, y_ref, s_out_ref, s_scr, *, tt):
    c = CHUNK
    t = pl.program_id(1)

    @pl.when(t == 0)
    def _():
        s_scr[...] = jnp.zeros_like(s_scr)

    seg = _head_seg()
    p = p_ref[0]
    q = _xpos(p[:, 0:GROUP_WIDTH], cos_ref[...], sin_ref[...])
    k = _xpos(p[:, GROUP_WIDTH:2 * GROUP_WIDTH], cos_ref[...], sin_ref[...]) * HEAD_DIM ** -0.5
    v = p[:, 2 * GROUP_WIDTH:3 * GROUP_WIDTH]
    gt = p[:, 3 * GROUP_WIDTH:]
    e_last = elast_ref[...]
    n_ch = tt // c
    hsl = [slice(h * HEAD_DIM, (h + 1) * HEAD_DIM) for h in range(N_HEADS)]
    pairs = [(ch, h) for ch in range(n_ch) for h in range(N_HEADS)]
    rows = lambda x, ch, h: x[ch * c:(ch + 1) * c, hsl[h]]

    qg = [q[ch * c:(ch + 1) * c] * eg_ref[...] for ch in range(n_ch)]
    kr = [k[ch * c:(ch + 1) * c] * erem_ref[...] for ch in range(n_ch)]
    qk = [_dot_nt(rows(q, ch, h), rows(k, ch, h)) * dmask_ref[h] for ch, h in pairs]
    intra = [_dot(x, rows(v, ch, h)) for x, (ch, h) in zip(qk, pairs)]
    kv = [_dot_tn(kr[ch][:, hsl[h]], rows(v, ch, h)) for ch, h in pairs]

    state = [s_scr[h] for h in range(N_HEADS)]
    o_chunks = []
    for ch in range(n_ch):
        o_chunks.append(jnp.concatenate(
            [intra[ch * N_HEADS + h] + _dot(qg[ch][:, hsl[h]], state[h]) for h in range(N_HEADS)], axis=1))
        state = [state[h] * e_last[:, hsl[h]] + kv[ch * N_HEADS + h] for h in range(N_HEADS)]
    for h in range(N_HEADS):
        s_scr[h] = state[h]
    o = jnp.concatenate(o_chunks, axis=0)
    o = o * lax.rsqrt(_segsum(o * o, seg) * (1.0 / HEAD_DIM) + NORM_EPS)
    y_ref[0] = _silu(gt) * o
    s_out_ref[0] = s_scr[...]


def _ret_tables(seq, pos0):
    inv = 1.0 / (RET_THETA ** jnp.linspace(0.0, 1.0, HEAD_DIM // 2, dtype=f32))
    ang = (pos0 + jnp.arange(seq)).astype(f32)[:, None] * inv[None, :]
    cos = jnp.tile(jnp.repeat(jnp.cos(ang), 2, axis=1), (1, N_HEADS))
    sgn = jnp.tile(jnp.array([-1.0, 1.0], f32), HEAD_DIM // 2)
    sin = jnp.tile(jnp.repeat(jnp.sin(ang), 2, axis=1) * sgn[None, :], (1, N_HEADS))
    lg = np.log(1.0 - 2.0 ** (-5.0 - np.arange(N_HEADS, dtype=np.float64)))
    i = np.arange(CHUNK)
    diff = i[:, None] - i[None, :]
    dmask = np.where(diff[None] >= 0, np.exp(lg[:, None, None] * np.maximum(diff, 0)[None]), 0.0)
    lane = np.repeat(lg, HEAD_DIM)[None, :]
    e_g = np.exp((i[:, None] + 1) * lane)
    e_rem = np.exp((CHUNK - 1 - i[:, None]) * lane)
    e_last = np.exp(CHUNK * lane)
    e_one = np.exp(lane)
    return cos, sin, tuple(jnp.asarray(a, f32) for a in (dmask, e_g, e_rem, e_last, e_one))


def _ret_prompt(pb, cos, sin, tabs, tt):
    b, seq, _ = pb.shape
    dmask, e_g, e_rem, e_last, _ = tabs
    full = lambda a: pl.BlockSpec(a.shape, lambda bi, t: (0,) * a.ndim)
    return pl.pallas_call(
        functools.partial(_ret_body, tt=tt),
        grid=(b, seq // tt),
        in_specs=[pl.BlockSpec((1, tt, 4 * GROUP_WIDTH), lambda bi, t: (bi, t, 0)),
                  pl.BlockSpec((tt, GROUP_WIDTH), lambda bi, t: (t, 0)),
                  pl.BlockSpec((tt, GROUP_WIDTH), lambda bi, t: (t, 0)),
                  full(dmask), full(e_g), full(e_rem), full(e_last)],
        out_specs=[pl.BlockSpec((1, tt, GROUP_WIDTH), lambda bi, t: (bi, t, 0)),
                   pl.BlockSpec((1, N_HEADS, HEAD_DIM, HEAD_DIM), lambda bi, t: (bi, 0, 0, 0))],
        out_shape=[jax.ShapeDtypeStruct((b, seq, GROUP_WIDTH), f32),
                   jax.ShapeDtypeStruct((b, N_HEADS, HEAD_DIM, HEAD_DIM), f32)],
        scratch_shapes=[pltpu.VMEM((N_HEADS, HEAD_DIM, HEAD_DIM), f32)],
        compiler_params=_cparams("arbitrary", "arbitrary"),
        name="retention_prompt",
    )(pb, cos, sin, dmask, e_g, e_rem, e_last)


def _gdn_scalars(pab, alog_p, dtb_p, sel_a, sel_b):
    logd = -jnp.exp(alog_p) * _softplus(pab + dtb_p)
    beta = jax.nn.sigmoid(pab)
    return _dot(logd, sel_a, precision=HI), _dot(beta, sel_b, precision=HI)


def _gdn_qkv(conv, seg):
    cs = _silu(conv)
    q = cs[:, 0:GROUP_WIDTH]
    k = cs[:, GROUP_WIDTH:2 * GROUP_WIDTH]
    v = cs[:, 2 * GROUP_WIDTH:]
    q = q * lax.rsqrt(_segsum(q * q, seg) + 1e-6) * HEAD_DIM ** -0.5
    k = k * lax.rsqrt(_segsum(k * k, seg) + 1e-6)
    return q, k, v


def _gdn_body(pc_ref, pab_ref, cw_ref, alog_ref, dtb_ref, sela_ref, selb_ref, nw_ref,
              y_ref, s_out_ref, s_scr, ext_scr, *, tt):
    c = CHUNK
    t = pl.program_id(1)

    @pl.when(t == 0)
    def _():
        s_scr[...] = jnp.zeros_like(s_scr)
        ext_scr[pl.ds(0, 8), :] = jnp.zeros((8, C_QKV), f32)

    seg = _head_seg()
    pc = pc_ref[0]
    gate = pc[:, C_QKV:]
    ext_scr[pl.ds(8, tt), :] = pc[:, :C_QKV]
    cw = cw_ref[...]
    conv = sum(ext_scr[pl.ds(8 - (GDN_CONV - 1) + j, tt), :] * cw[j:j + 1, :] for j in range(GDN_CONV))
    ext_scr[pl.ds(0, 8), :] = pc[tt - 8:tt, :C_QKV]
    q, k, v = _gdn_qkv(conv, seg)
    logd, beta = _gdn_scalars(pab_ref[0], alog_ref[...], dtb_ref[...], sela_ref[...], selb_ref[...])
    kb = k * beta
    vb = v * beta

    ri, ci = _iota2((c, c), 0), _iota2((c, c), 1)
    incl, strict = ri >= ci, ri > ci
    lmat = incl.astype(f32)
    ones = jnp.ones((c, c), f32)
    eye4 = (_iota2((c, GROUP_WIDTH), 0) == _iota2((c, GROUP_WIDTH), 1) % HEAD_DIM).astype(f32)
    incl4 = _iota2((c, GROUP_WIDTH), 0) >= _iota2((c, GROUP_WIDTH), 1) % HEAD_DIM

    n_ch = tt // c
    hsl = [slice(h * HEAD_DIM, (h + 1) * HEAD_DIM) for h in range(N_HEADS)]
    pairs = [(ch, h) for ch in range(n_ch) for h in range(N_HEADS)]

    dmask, qg, kbg, kr, e_last = [], [], [], [], []
    for ch in range(n_ch):
        sl = slice(ch * c, (ch + 1) * c)
        gcum = _dot(lmat, logd[sl], precision=HI)
        gcum_t = _dot(ones, gcum * eye4, precision=HI)
        g_last = gcum[c - 1:c, :]
        e_g = jnp.exp(gcum)
        e_last.append(jnp.exp(g_last))
        dmask.append(jnp.exp(jnp.where(incl4, gcum - gcum_t, -jnp.inf)))
        qg.append(q[sl] * e_g)
        kbg.append(kb[sl] * e_g)
        kr.append(k[sl] * jnp.exp(g_last - gcum))
    rows = lambda x, ch, h: x[ch * c:(ch + 1) * c, hsl[h]]
    kq = [_dot_nt(jnp.concatenate([rows(kb, ch, h), rows(q, ch, h)], axis=0), rows(k, ch, h)) for ch, h in pairs]
    tmat = _unit_lower_inverse([jnp.where(strict, -x[:c] * dmask[ch][:, hsl[h]], 0.0) for x, (ch, h) in zip(kq, pairs)], c)
    aqk = [x[c:] * dmask[ch][:, hsl[h]] for x, (ch, h) in zip(kq, pairs)]
    uw = [_dot(tm, jnp.concatenate([rows(vb, ch, h), kbg[ch][:, hsl[h]]], axis=1)) for tm, (ch, h) in zip(tmat, pairs)]

    state = [s_scr[h] for h in range(N_HEADS)]
    o_chunks = []
    for ch in range(n_ch):
        idx = [ch * N_HEADS + h for h in range(N_HEADS)]
        vnew = [uw[i][:, :HEAD_DIM] - _dot(uw[i][:, HEAD_DIM:], state[h]) for h, i in enumerate(idx)]
        o_chunks.append(jnp.concatenate(
            [_dot(qg[ch][:, hsl[h]], state[h]) + _dot(aqk[i], vnew[h]) for h, i in enumerate(idx)], axis=1))
        state = [state[h] * e_last[ch][:, hsl[h]] + _dot_tn(kr[ch][:, hsl[h]], vnew[h]) for h in range(N_HEADS)]
    for h in range(N_HEADS):
        s_scr[h] = state[h]
    o = jnp.concatenate(o_chunks, axis=0)
    o = o * lax.rsqrt(_segsum(o * o, seg) * (1.0 / HEAD_DIM) + NORM_EPS) * nw_ref[...] * _silu(gate)
    y_ref[0] = o
    s_out_ref[0] = s_scr[...]


def _gdn_prompt(pc, pab, prm, tt):
    b, seq, _ = pc.shape
    vec = lambda a: pl.BlockSpec(a.shape, lambda bi, t: (0, 0))
    return pl.pallas_call(
        functools.partial(_gdn_body, tt=tt),
        grid=(b, seq // tt),
        in_specs=[pl.BlockSpec((1, tt, 4 * GROUP_WIDTH), lambda bi, t: (bi, t, 0)),
                  pl.BlockSpec((1, tt, 128), lambda bi, t: (bi, t, 0))] + [vec(a) for a in prm],
        out_specs=[pl.BlockSpec((1, tt, GROUP_WIDTH), lambda bi, t: (bi, t, 0)),
                   pl.BlockSpec((1, N_HEADS, HEAD_DIM, HEAD_DIM), lambda bi, t: (bi, 0, 0, 0))],
        out_shape=[jax.ShapeDtypeStruct((b, seq, GROUP_WIDTH), f32),
                   jax.ShapeDtypeStruct((b, N_HEADS, HEAD_DIM, HEAD_DIM), f32)],
        scratch_shapes=[pltpu.VMEM((N_HEADS, HEAD_DIM, HEAD_DIM), f32), pltpu.VMEM((tt + 8, C_QKV), f32)],
        compiler_params=_cparams("arbitrary", "arbitrary"),
        name="gdn_prompt",
    )(pc, pab, *prm)


def _top_blocks(gate, own, nb):
    blk = _iota2(gate.shape, 0).astype(f32)
    g = jnp.where(blk < own, gate, -jnp.inf)
    picks = []
    for _ in range(MOBA_TOPK):
        m = jnp.max(g, axis=0, keepdims=True)
        idx = jnp.min(jnp.where(g == m, blk, float(nb)), axis=0, keepdims=True)
        picks.append(jnp.where(m > -jnp.inf, idx, -1.0))
        g = jnp.where(blk == idx, -jnp.inf, g)
    return picks


def _moba_body(q_ref, k_ref, v_ref, o_ref, k_scr, vt_scr, mean_scr, *, nb):
    i = pl.program_id(1)
    blk_w = MOBA_BLOCK
    hsl = [slice(h * HEAD_DIM, (h + 1) * HEAD_DIM) for h in range(N_HEADS)]
    own_start = pl.multiple_of(i * blk_w, blk_w)

    @pl.when(i == 0)
    def _():
        mean_scr[...] = jnp.zeros_like(mean_scr)

    k_tile = k_ref[0]
    k_scr[pl.ds(own_start, blk_w), :] = k_tile.astype(bf16)
    mean_row = jnp.sum(k_tile, axis=0, keepdims=True) * (1.0 / blk_w)
    means = jnp.where(_iota2(mean_scr.shape, 0) == i, mean_row, mean_scr[...])
    mean_scr[...] = means
    v_t = v_ref[0].T.astype(bf16)
    for h in range(N_HEADS):
        vt_scr[h, pl.ds(0, HEAD_DIM), pl.ds(own_start, blk_w)] = v_t[hsl[h]]
        vt_scr[h, pl.ds(HEAD_DIM, SUM_ROWS), pl.ds(own_start, blk_w)] = jnp.ones((SUM_ROWS, blk_w), bf16)

    q_t = (q_ref[0] * HEAD_DIM ** -0.5).T
    qt = [q_t[hsl[h]] for h in range(N_HEADS)]
    qtb = [(x * LOG2E).astype(bf16) for x in qt]
    picks = [_top_blocks(_dot(means[:, hsl[h]], qt[h], precision=HI), i.astype(f32), nb) for h in range(N_HEADS)]

    k_own = k_scr[pl.ds(own_start, blk_w), :]
    causal = _iota2((blk_w, blk_w), 0) <= _iota2((blk_w, blk_w), 1)
    s_own = [jnp.where(causal, _dot(k_own[:, hsl[h]], qtb[h]), NEG) for h in range(N_HEADS)]
    m_own = [jnp.max(x, axis=0, keepdims=True) for x in s_own]
    p_own = [jnp.exp2(x - m).astype(bf16) for x, m in zip(s_own, m_own)]
    carry = []
    for h in range(N_HEADS):
        carry += [m_own[h], _dot(vt_scr[h, :, pl.ds(own_start, blk_w)], p_own[h])]

    def step(jj, carry):
        js = [2 * jj, 2 * jj + 1]
        starts = [pl.multiple_of(j * blk_w, blk_w) for j in js]
        kj = [k_scr[pl.ds(st, blk_w), :] for st in starts]
        units = [(h, e) for h in range(N_HEADS) for e in range(2)]
        s = [_dot(kj[e][:, hsl[h]], qtb[h]) for h, e in units]
        chosen = [functools.reduce(jnp.logical_or, [pk == js[e].astype(f32) for pk in picks[h]]) for h, e in units]
        m_blk = [jnp.where(c, jnp.max(x, axis=0, keepdims=True), NEG) for c, x in zip(chosen, s)]
        m_new = [jnp.maximum(carry[2 * h], jnp.maximum(m_blk[2 * h], m_blk[2 * h + 1])) for h in range(N_HEADS)]
        shift = [jnp.where(c, m_new[h], -NEG) for c, (h, e) in zip(chosen, units)]
        p = [jnp.exp2(x - sh).astype(bf16) for x, sh in zip(s, shift)]
        pv = [_dot(vt_scr[h, :, pl.ds(starts[e], blk_w)], x) for x, (h, e) in zip(p, units)]
        out = []
        for h in range(N_HEADS):
            out += [m_new[h], jnp.exp2(carry[2 * h] - m_new[h]) * carry[2 * h + 1] + (pv[2 * h] + pv[2 * h + 1])]
        return tuple(out)

    carry = lax.fori_loop(0, (i + 1) // 2, step, tuple(carry))
    o_t = jnp.concatenate([carry[2 * h + 1][:HEAD_DIM] / carry[2 * h + 1][HEAD_DIM:HEAD_DIM + 1]
                           for h in range(N_HEADS)], axis=0)
    o_ref[0] = o_t.T


def _moba_prompt(q, k, v):
    b, seq, _ = k.shape
    nb = seq // MOBA_BLOCK
    assert seq % MOBA_BLOCK == 0
    tile = pl.BlockSpec((1, MOBA_BLOCK, GROUP_WIDTH), lambda bi, i: (bi, i, 0))
    return pl.pallas_call(
        functools.partial(_moba_body, nb=nb),
        grid=(b, nb),
        in_specs=[tile, tile, tile],
        out_specs=tile,
        out_shape=jax.ShapeDtypeStruct((b, seq, GROUP_WIDTH), f32),
        scratch_shapes=[pltpu.VMEM((seq, GROUP_WIDTH), bf16),
                        pltpu.VMEM((N_HEADS, HEAD_DIM + SUM_ROWS, seq), bf16),
                        pltpu.VMEM((nb, GROUP_WIDTH), f32)],
        compiler_params=_cparams("arbitrary", "arbitrary"),
        name="moba_prompt",
    )(q, k, v)


def _row_to_col(row, eye):
    return jnp.sum(eye * row, axis=1, keepdims=True)


def _col_to_row(col, eye):
    return jnp.sum(eye * col, axis=0, keepdims=True)


def _sample_mix_body(pa_ref, shift_ref, pb_ref, pc_ref, pab_ref, conv_ref, srw_ref, srt_ref, sgd_ref,
                     mu_ref, w0_ref, w2_ref, a0_ref, a2_ref, g2_ref, kk_ref, ka_ref, rk_ref, lnw_ref, lnb_ref,
                     cos_ref, sin_ref, eone_ref, cw_ref, alog_ref, dtb_ref, sela_ref, selb_ref, nw_ref,
                     ya_ref, yb_ref, yc_ref, srw_out, srt_out, sgd_out):
    seg = _head_seg()
    eye = _eye(HEAD_DIM)
    hsl = [slice(h * HEAD_DIM, (h + 1) * HEAD_DIM) for h in range(N_HEADS)]

    r, logw, k2, v, kk, a, g = _rwkv_pre(pa_ref[0], shift_ref[0], mu_ref[...], w0_ref[...], w2_ref[...], a0_ref[...],
                                         a2_ref[...], g2_ref[...], kk_ref[...], ka_ref[...], seg)
    w = jnp.exp(logw)
    kka = kk * a
    ys = []
    for h, hs in enumerate(hsl):
        s_old = srw_ref[0, h]
        sa = jnp.sum(s_old * (-kk[:, hs]), axis=1, keepdims=True)
        s_new = s_old * w[:, hs] + sa * kka[:, hs] + _row_to_col(v[:, hs], eye) * k2[:, hs]
        srw_out[0, h] = s_new
        ys.append(_col_to_row(jnp.sum(s_new * r[:, hs], axis=1, keepdims=True), eye))
    ya_ref[0] = _rwkv_post(jnp.concatenate(ys, axis=1), r, k2, v, g, rk_ref[...], lnw_ref[...], lnb_ref[...], seg)

    pb = pb_ref[0]
    q = _xpos(pb[:, 0:GROUP_WIDTH], cos_ref[...], sin_ref[...])
    k = _xpos(pb[:, GROUP_WIDTH:2 * GROUP_WIDTH], cos_ref[...], sin_ref[...]) * HEAD_DIM ** -0.5
    v = pb[:, 2 * GROUP_WIDTH:3 * GROUP_WIDTH]
    e_one = eone_ref[...]
    os_ = []
    for h, hs in enumerate(hsl):
        s_new = srt_ref[0, h] * e_one[:, hs] + _row_to_col(k[:, hs], eye) * v[:, hs]
        srt_out[0, h] = s_new
        os_.append(jnp.sum(_row_to_col(q[:, hs], eye) * s_new, axis=0, keepdims=True))
    o = jnp.concatenate(os_, axis=1)
    o = o * lax.rsqrt(_segsum(o * o, seg) * (1.0 / HEAD_DIM) + NORM_EPS)
    yb_ref[0] = _silu(pb[:, 3 * GROUP_WIDTH:]) * o

    pc = pc_ref[0]
    cw = cw_ref[...]
    prev = conv_ref[0]
    conv = pc[:, :C_QKV] * cw[GDN_CONV - 1:GDN_CONV, :]
    for j in range(GDN_CONV - 1):
        conv = conv + prev[j:j + 1, :] * cw[j:j + 1, :]
    q, k, v = _gdn_qkv(conv, seg)
    logd, beta = _gdn_scalars(pab_ref[0], alog_ref[...], dtb_ref[...], sela_ref[...], selb_ref[...])
    alpha = jnp.exp(logd)
    os_ = []
    for h, hs in enumerate(hsl):
        s_old = sgd_ref[0, h]
        k_col = _row_to_col(k[:, hs], eye)
        ks = jnp.sum(k_col * s_old, axis=0, keepdims=True)
        s_new = alpha[:, hs] * s_old + k_col * (beta[:, hs] * (v[:, hs] - alpha[:, hs] * ks))
        sgd_out[0, h] = s_new
        os_.append(jnp.sum(_row_to_col(q[:, hs], eye) * s_new, axis=0, keepdims=True))
    o = jnp.concatenate(os_, axis=1)
    yc_ref[0] = o * lax.rsqrt(_segsum(o * o, seg) * (1.0 / HEAD_DIM) + NORM_EPS) * nw_ref[...] * _silu(pc[:, C_QKV:])


def _sample_mixers(pa, shift, pb, pc, pab, conv_prev, s_rwkv, s_ret, s_gdn, rwkv_prm, ret_prm, gdn_prm):
    nseq = pa.shape[0]
    row = lambda a: pl.BlockSpec((1, 1, a.shape[-1]), lambda b: (b, 0, 0))
    state = pl.BlockSpec((1, N_HEADS, HEAD_DIM, HEAD_DIM), lambda b: (b, 0, 0, 0))
    full = lambda a: pl.BlockSpec(a.shape, lambda b: (0,) * a.ndim)
    r3 = lambda a: a.reshape(nseq, 1, a.shape[-1])
    prm = tuple(rwkv_prm) + tuple(ret_prm) + tuple(gdn_prm)
    outs = pl.pallas_call(
        _sample_mix_body,
        grid=(nseq,),
        in_specs=[row(pa), row(shift), row(pb), row(pc), row(pab),
                  pl.BlockSpec((1, GDN_CONV - 1, C_QKV), lambda b: (b, 0, 0)), state, state, state]
                 + [full(a) for a in prm],
        out_specs=[row(pa[..., :GROUP_WIDTH])] * 3 + [state] * 3,
        out_shape=[jax.ShapeDtypeStruct((nseq, 1, GROUP_WIDTH), f32)] * 3
                  + [jax.ShapeDtypeStruct((nseq, N_HEADS, HEAD_DIM, HEAD_DIM), f32)] * 3,
        compiler_params=_cparams("arbitrary"),
        name="sample_mixers",
    )(r3(pa), r3(shift), r3(pb), r3(pc), r3(pab), conv_prev, s_rwkv, s_ret, s_gdn, *prm)
    return outs


def _moba_sample_body(pt_ref, q_ref, qb_ref, kn_ref, vn_ref, *refs, nb, bps):
    del pt_ref
    npg = bps * PAGES_PER_BLOCK
    k_refs, v_refs = refs[:npg], refs[npg:2 * npg]
    o_ref, gate_scr, m_scr, l_scr, acc_scr = refs[2 * npg:]
    j = pl.program_id(1)
    hmask = (_iota2((N_HEADS, GROUP_WIDTH), 1) // HEAD_DIM == _iota2((N_HEADS, GROUP_WIDTH), 0)).astype(f32)
    qb = qb_ref[0]
    qb = jnp.concatenate([qb] * (MOBA_BLOCK // 128), axis=1)
    block_t = lambda page_refs: jnp.concatenate(
        [r[...].reshape(GROUP_WIDTH, PAGE_SIZE) for r in page_refs], axis=1)
    blocks = [slice(e * PAGES_PER_BLOCK, (e + 1) * PAGES_PER_BLOCK) for e in range(bps)]
    kt = [block_t(k_refs[pg]) for pg in blocks]
    s = [jnp.sum((x * qb).reshape(N_HEADS, HEAD_DIM, MOBA_BLOCK), axis=1) for x in kt]
    gate = [jnp.sum(x, axis=1, keepdims=True) * (1.0 / MOBA_BLOCK) for x in s]
    m = [jnp.max(x, axis=1, keepdims=True) for x in s]
    p = [jnp.exp(x - mx) for x, mx in zip(s, m)]
    pv = [_dot_nt(x, block_t(v_refs[pg])) for x, pg in zip(p, blocks)]
    for e in range(bps):
        slot = j * bps + e
        gate_scr[slot] = jnp.broadcast_to(gate[e], (N_HEADS, 128))
        m_scr[slot] = jnp.broadcast_to(m[e], (N_HEADS, 128))
        l_scr[slot] = jnp.broadcast_to(jnp.sum(p[e], axis=1, keepdims=True), (N_HEADS, 128))
        acc_scr[slot] = pv[e]

    @pl.when(j == nb // bps - 1)
    def _():
        g = gate_scr[...]
        blk = _iota2(g.shape, 0).astype(f32)
        sel = jnp.zeros(g.shape, jnp.bool_)
        for _ in range(min(MOBA_TOPK, nb)):
            gm = jnp.max(g, axis=0, keepdims=True)
            idx = jnp.min(jnp.where(g == gm, blk, float(nb)), axis=0, keepdims=True)
            pick = blk == idx
            sel = jnp.logical_or(sel, pick)
            g = jnp.where(pick, -jnp.inf, g)
        q_own = hmask * (q_ref[0] * HEAD_DIM ** -0.5)
        s_own = jnp.broadcast_to(jnp.sum(q_own * kn_ref[0], axis=1, keepdims=True), (N_HEADS, 128))
        mb = jnp.where(sel, m_scr[...], NEG)
        m_all = jnp.maximum(jnp.max(mb, axis=0), s_own)
        wgt = jnp.where(sel, jnp.exp(mb - m_all[None]), 0.0)
        w_own = jnp.exp(s_own - m_all)
        denom = jnp.sum(wgt * l_scr[...], axis=0) + w_own
        wide = lambda a: jnp.concatenate([a, a], axis=-1)
        num = jnp.sum(wide(wgt) * acc_scr[...], axis=0) + wide(w_own) * vn_ref[0]
        o_ref[0] = jnp.sum(hmask * (num / wide(denom)), axis=0, keepdims=True)


def _moba_sample(q, k_new, v_new, cache_kt, cache_vt, page_table, layer, bps):
    nseq, n_pages = page_table.shape
    nb = n_pages // PAGES_PER_BLOCK
    assert n_pages % PAGES_PER_BLOCK == 0 and nb % bps == 0
    npg = bps * PAGES_PER_BLOCK
    row = pl.BlockSpec((1, 1, GROUP_WIDTH), lambda b, j, pt: (b, 0, 0))
    page = lambda e: pl.BlockSpec((None, None, N_HEADS, HEAD_DIM, PAGE_SIZE),
                                  lambda b, j, pt: (layer, pt[b, npg * j + e], 0, 0, 0))
    pages = [page(e) for e in range(npg)]
    r3 = lambda a: a.reshape(nseq, 1, GROUP_WIDTH)
    q_lanes = jnp.broadcast_to((q * HEAD_DIM ** -0.5)[:, :, None], (nseq, GROUP_WIDTH, 128))
    return pl.pallas_call(
        functools.partial(_moba_sample_body, nb=nb, bps=bps),
        grid_spec=pltpu.PrefetchScalarGridSpec(
            num_scalar_prefetch=1,
            grid=(nseq, nb // bps),
            in_specs=[row, pl.BlockSpec((1, GROUP_WIDTH, 128), lambda b, j, pt: (b, 0, 0)), row, row] + pages + pages,
            out_specs=row,
            scratch_shapes=[pltpu.VMEM((nb, N_HEADS, 128), f32)] * 3 + [pltpu.VMEM((nb, N_HEADS, GROUP_WIDTH), f32)]),
        out_shape=jax.ShapeDtypeStruct((nseq, 1, GROUP_WIDTH), f32),
        compiler_params=_cparams("arbitrary", "arbitrary"),
        name="moba_sample",
    )(page_table, r3(q), q_lanes, r3(k_new), r3(v_new), *([cache_kt] * npg), *([cache_vt] * npg)
      ).reshape(nseq, GROUP_WIDTH)


def _pad_rows(w, start, total):
    return jnp.zeros((total, w.shape[1]), w.dtype).at[start:start + w.shape[0]].set(w)


def _mixer_params(l, rwkv_mu, rwkv_w0, rwkv_w2, rwkv_a0, rwkv_a2, rwkv_g2, rwkv_k_k, rwkv_k_a, rwkv_r_k, rwkv_ln_w,
                  rwkv_ln_b, gdn_conv_w, gdn_a_log, gdn_dt_bias, gdn_norm_w):
    row = lambda a: a.reshape(1, -1)
    rwkv = (row(rwkv_mu[l]), row(rwkv_w0[l]), _pad_rows(rwkv_w2[l], 0, LORA_W), row(rwkv_a0[l]),
            _pad_rows(rwkv_a2[l], RWKV_W_LORA, LORA_W), _pad_rows(rwkv_g2[l], RWKV_W_LORA + RWKV_A_LORA, LORA_W),
            row(rwkv_k_k[l]), row(rwkv_k_a[l]), row(rwkv_r_k[l]), row(rwkv_ln_w[l]), row(rwkv_ln_b[l]))
    lane = np.arange(128)[:, None]
    col_head = (np.arange(GROUP_WIDTH) // HEAD_DIM)[None, :]
    sel_a = jnp.asarray(lane == col_head, f32)
    sel_b = jnp.asarray(lane == col_head + N_HEADS, f32)
    pad128 = lambda a: jnp.zeros((1, 128), f32).at[0, :N_HEADS].set(a)
    gdn = (gdn_conv_w[l], pad128(gdn_a_log[l]), pad128(gdn_dt_bias[l]), sel_a, sel_b,
           jnp.tile(gdn_norm_w[l], N_HEADS).reshape(1, GROUP_WIDTH))
    return rwkv, gdn


def kernel(x_prompt, x_sample, c_prompt, c_sample, cache_k, cache_v, page_table, state_rwkv, state_rwkv_shift,
           state_ret, state_gdn, state_gdn_conv, w_mod, b_mod, norm_pre, norm_post, ffn_w_gate, ffn_w_up, ffn_w_down,
           w_in, w_out, rwkv_mu, rwkv_w0, rwkv_w2, rwkv_a0, rwkv_a2, rwkv_g2, rwkv_k_k, rwkv_k_a, rwkv_r_k, rwkv_ln_w,
           rwkv_ln_b, gdn_conv_w, gdn_a_log, gdn_dt_bias, gdn_norm_w):
    depth = w_in.shape[0]
    bp, seq, d = x_prompt.shape
    bs = x_sample.shape[0]
    assert x_sample.shape[1] == 1
    past_len = page_table.shape[1] * PAGE_SIZE
    assert past_len % MOBA_BLOCK == 0
    cache_kt = jnp.transpose(cache_k, (0, 1, 3, 4, 2))
    cache_vt = jnp.transpose(cache_v, (0, 1, 3, 4, 2))

    n_c = bp + bs
    c_all = jnp.concatenate([c_prompt, c_sample, jnp.zeros((-n_c % 8, d), f32)], axis=0)
    mods = _modulation(c_all, w_mod, b_mod)

    wg, wu, wdn = ffn_w_gate.astype(bf16), ffn_w_up.astype(bf16), ffn_w_down.astype(bf16)
    w_in_b, w_out_b = w_in.astype(bf16), w_out.astype(bf16)
    a0, b0 = A_COLS, A_COLS + 4 * GROUP_WIDTH
    c0 = b0 + C_QKV + GROUP_WIDTH
    d0 = c0 + 2 * N_HEADS
    col_groups = [(0, a0), (a0, b0), (b0, c0), (d0, d0 + GROUP_WIDTH), (d0 + GROUP_WIDTH, d0 + 2 * GROUP_WIDTH),
                  (d0 + 2 * GROUP_WIDTH, d0 + 3 * GROUP_WIDTH)]

    ws = [w_in_b[:, :, lo:hi] for lo, hi in col_groups]
    ws.append(jnp.pad(w_in_b[:, :, c0:d0], ((0, 0), (0, 0), (0, 128 - 2 * N_HEADS))))
    cos_p, sin_p, ret_tabs = _ret_tables(seq, 0)
    cos_s, sin_s, _ = _ret_tables(1, past_len)
    tm = 512
    tm_ffn = 1024
    tt = 256

    xp, xs = x_prompt, x_sample.reshape(1, bs, d)
    news_p, news_s = [], []
    for l in range(depth):
        mp = [m.reshape(bp, 1, d) for m in jnp.split(mods[l, :bp], 9, axis=-1)]
        ms = [m.reshape(1, bs, d) for m in jnp.split(mods[l, bp:n_c], 9, axis=-1)]
        npre = [norm_pre[l, i].reshape(1, d) for i in range(3)]
        npost = [norm_post[l, i].reshape(1, d) for i in range(3)]
        rwkv_prm, gdn_prm = _mixer_params(l, rwkv_mu, rwkv_w0, rwkv_w2, rwkv_a0, rwkv_a2, rwkv_g2, rwkv_k_k, rwkv_k_a,
                                          rwkv_r_k, rwkv_ln_w, rwkv_ln_b, gdn_conv_w, gdn_a_log, gdn_dt_bias, gdn_norm_w)

        xp = _ffn(xp, mp[0], mp[1], mp[2], npre[0], npost[0], wg, wu, wdn, l, 0, tm_ffn, FFN_TF)
        pa, pb, pc, qd, kd, vd, pab = _inproj(xp, mp[3], mp[4], npre[1], ws, l, tm)
        ya, rw_new = _rwkv_prompt(pa, rwkv_prm, 2 * tt)
        yb, rt_new = _ret_prompt(pb, cos_p, sin_p, ret_tabs, tt)
        yc, gd_new = _gdn_prompt(pc, pab, gdn_prm, tt)
        yd = _moba_prompt(qd, kd, vd)
        xp = _outproj(xp, (ya, yb, yc, yd), mp[5], npost[1], w_out_b, l, tm)
        xp = _ffn(xp, mp[6], mp[7], mp[8], npre[2], npost[2], wg, wu, wdn, l, 1, tm_ffn, FFN_TF)
        hd = lambda a: a.reshape(a.shape[0], a.shape[1], N_HEADS, HEAD_DIM)
        news_p.append((hd(kd), hd(vd), rw_new, pa[:, -1], rt_new, gd_new, pc[:, seq - (GDN_CONV - 1):, :C_QKV]))

        xs = _ffn(xs, ms[0], ms[1], ms[2], npre[0], npost[0], wg, wu, wdn, l, 0, bs, FFN_TF)
        pa, pb, pc, qd, kd, vd, pab = (a[0] for a in _inproj(xs, ms[3], ms[4], npre[1], ws, l, bs))
        ya, yb, yc, rw_new, rt_new, gd_new = _sample_mixers(
            pa, state_rwkv_shift[l], pb, pc, pab, state_gdn_conv[l], state_rwkv[l], state_ret[l], state_gdn[l],
            rwkv_prm, (cos_s, sin_s, ret_tabs[4]), gdn_prm)
        yd = _moba_sample(qd, kd, vd, cache_kt, cache_vt, page_table, l, 8)
        ys = tuple(a.reshape(1, bs, GROUP_WIDTH) for a in (ya, yb, yc, yd))
        xs = _outproj(xs, ys, ms[5], npost[1], w_out_b, l, bs)
        xs = _ffn(xs, ms[6], ms[7], ms[8], npre[2], npost[2], wg, wu, wdn, l, 1, bs, FFN_TF)
        hs_ = lambda a: a.reshape(bs, 1, N_HEADS, HEAD_DIM)
        conv_new = jnp.concatenate([state_gdn_conv[l][:, 1:], pc[:, None, :C_QKV]], axis=1)
        news_s.append((hs_(kd), hs_(vd), rw_new, pa, rt_new, gd_new, conv_new))

    kp, vp, rwp, shp, rtp, gdp, cvp = [jnp.stack(t) for t in zip(*news_p)]
    ks, vs, rws, shs, rts, gds, cvs = [jnp.stack(t) for t in zip(*news_s)]
    return (xp, xs.reshape(bs, 1, d), kp, vp, ks, vs, rwp, rws, shp, shs, rtp, rts, gdp, gds, cvp, cvs)
```

```python
import functools
import math

import numpy as np
import jax
import jax.numpy as jnp
from jax import lax
from jax.experimental import pallas as pl
from jax.experimental.pallas import tpu as pltpu

f32 = jnp.float32
bf16 = jnp.bfloat16
HI = lax.Precision.HIGHEST

GROUP_WIDTH = 256
HEAD_DIM = 64
N_HEADS = 4
FFN_RES = 0.5
NORM_EPS = 1e-6
RWKV_W_LORA = 32
RWKV_A_LORA = 32
RWKV_G_LORA = 64
RWKV_LN_EPS = 64e-5
RET_THETA = 10000.0
GDN_CONV = 4
MOBA_BLOCK = 256
MOBA_TOPK = 3
SUM_ROWS = 16
PAGE_SIZE = 128
PAGES_PER_BLOCK = MOBA_BLOCK // PAGE_SIZE
CHUNK = 64
PIPE_GROUP = 4
FFN_TF = 256
INV_BLOCK = 16
LORA_W = RWKV_W_LORA + RWKV_A_LORA + RWKV_G_LORA
A_COLS = 3 * GROUP_WIDTH + LORA_W
C_QKV = 3 * GROUP_WIDTH
NEG = -1e30
LOG2E = math.log2(math.e)
VMEM_LIMIT = 56 * 1024 * 1024


def _cparams(*sem):
    return pltpu.CompilerParams(dimension_semantics=sem, vmem_limit_bytes=VMEM_LIMIT)


def _dot(a, b, precision=None):
    return jnp.dot(a, b, preferred_element_type=f32, precision=precision)


def _dot_nt(a, b, precision=None):
    return lax.dot_general(a, b, (((1,), (1,)), ((), ())), preferred_element_type=f32, precision=precision)


def _dot_tn(a, b, precision=None):
    return lax.dot_general(a, b, (((0,), (0,)), ((), ())), preferred_element_type=f32, precision=precision)


def _rms(x, g):
    return x * lax.rsqrt(jnp.mean(x * x, axis=-1, keepdims=True) + NORM_EPS) * g


def _silu(x):
    return x * jax.nn.sigmoid(x)


def _softplus(x):
    return jnp.maximum(x, 0.0) + jnp.log(1.0 + jnp.exp(-jnp.abs(x)))


def _iota2(shape, axis):
    return lax.broadcasted_iota(jnp.int32, shape, axis)


def _head_seg():
    return (_iota2((GROUP_WIDTH, GROUP_WIDTH), 0) // HEAD_DIM == _iota2((GROUP_WIDTH, GROUP_WIDTH), 1) // HEAD_DIM).astype(f32)


def _segsum(x, seg):
    return _dot(x, seg, precision=HI)


def _eye(n):
    return (_iota2((n, n), 0) == _iota2((n, n), 1)).astype(f32)


def _split(x):
    hi = x.astype(bf16)
    return hi, (x - hi.astype(f32)).astype(bf16)


def _dot3(a, b):
    return _dot(a[0], b[0]) + (_dot(a[0], b[1]) + _dot(a[1], b[0]))


def _bdot(a, b):
    return _dot(a.astype(bf16), b.astype(bf16))


def _bdot_nt(a, b):
    return _dot_nt(a.astype(bf16), b.astype(bf16))


def _bdot_tn(a, b):
    return _dot_tn(a.astype(bf16), b.astype(bf16))


def _nilpotent_inverse(ns, eye, index):
    ts = [eye + n for n in ns]
    ps = [_split(n) for n in ns]
    for _ in range(int(math.log2(index)) - 1):
        ps = [_split(_dot3(p, p)) for p in ps]
        tsp = [_split(t) for t in ts]
        ts = [t + _dot3(p, tp) for t, p, tp in zip(ts, ps, tsp)]
    return ts


def _unit_lower_inverse(ns, c):
    eye = _eye(c)
    same = _iota2((c, c), 0) // INV_BLOCK == _iota2((c, c), 1) // INV_BLOCK
    t_diag = _nilpotent_inverse([jnp.where(same, n, 0.0) for n in ns], eye, INV_BLOCK)
    tds = [t.astype(bf16) for t in t_diag]
    ps = [_dot(td, jnp.where(same, 0.0, n).astype(bf16)) for td, n in zip(tds, ns)]
    t_off = [eye + p for p in ps]
    pw = [p.astype(bf16) for p in ps]
    for _ in range(int(math.log2(c // INV_BLOCK)) - 1):
        pw = [_dot(p, p).astype(bf16) for p in pw]
        t_off = [t + _dot(p, t.astype(bf16)) for t, p in zip(t_off, pw)]
    return [_dot(to.astype(bf16), td) for to, td in zip(t_off, tds)]


def _mod_body(c_ref, w_ref, b_ref, o_ref):
    s = _silu(c_ref[...]).astype(bf16)
    o_ref[0] = _dot(s, w_ref[0].astype(bf16)) + b_ref[0]


def _modulation(c_all, w_mod, b_mod):
    depth, d, n9 = w_mod.shape
    rows = c_all.shape[0]
    tn = 1152
    return pl.pallas_call(
        _mod_body,
        grid=(depth, n9 // tn),
        in_specs=[pl.BlockSpec((rows, d), lambda l, j: (0, 0)),
                  pl.BlockSpec((1, d, tn), lambda l, j: (l, 0, j)),
                  pl.BlockSpec((1, 1, tn), lambda l, j: (l, 0, j))],
        out_specs=pl.BlockSpec((1, rows, tn), lambda l, j: (l, 0, j)),
        out_shape=jax.ShapeDtypeStruct((depth, rows, n9), f32),
        compiler_params=_cparams("arbitrary", "arbitrary"),
        name="modulation",
    )(c_all, w_mod, b_mod.reshape(depth, 1, n9))


def _ffn_body(x_ref, *refs, tf, mixed):
    if mixed:
        ya_ref, yb_ref, yc_ref, yd_ref, gm_ref, npm_ref, wo_ref = refs[:7]
        refs = refs[7:]
    sh_ref, sc_ref, g_ref, npre_ref, npost_ref, wg_ref, wu_ref, wd_ref, o_ref = refs
    x = x_ref[0]
    if mixed:
        y = jnp.concatenate([ya_ref[0], yb_ref[0], yc_ref[0], yd_ref[0]], axis=-1).astype(bf16)
        x = x + gm_ref[0] * _rms(_dot(y, wo_ref[...]), npm_ref[...])
    h = (_rms(x, npre_ref[...]) * (1.0 + sc_ref[0]) + sh_ref[0]).astype(bf16)
    acc = None
    for j in range(wg_ref.shape[1] // tf):
        cs = slice(j * tf, (j + 1) * tf)
        act = (_silu(_dot(h, wg_ref[:, cs])) * _dot(h, wu_ref[:, cs])).astype(bf16)
        part = _dot(act, wd_ref[cs, :])
        acc = part if acc is None else acc + part
    o_ref[0] = x + FFN_RES * g_ref[0] * _rms(acc, npost_ref[...])


def _ffn(x, sh, sc, g, npre, npost, wg, wu, wd, l, i, tm, tf, mix=None):
    b, seq, d = x.shape
    r = sh.shape[1]
    dff = wg.shape[-1]
    mod_spec = pl.BlockSpec((1, r, d), lambda bi, t: (bi, 0 if r == 1 else t, 0))
    vec_spec = pl.BlockSpec((1, d), lambda bi, t: (0, 0))
    once = pl.Buffered(1)
    mix_specs, mix_args = [], []
    if mix is not None:
        ys, gm, npm, w_out = mix
        y_spec = pl.BlockSpec((1, tm, GROUP_WIDTH), lambda bi, t: (bi, t, 0))
        mix_specs = [y_spec] * len(ys) + [mod_spec, vec_spec,
                                          pl.BlockSpec((None, d, d), lambda bi, t: (l, 0, 0), pipeline_mode=once)]
        mix_args = [*ys, gm, npm, w_out]
    return pl.pallas_call(
        functools.partial(_ffn_body, tf=tf, mixed=mix is not None),
        grid=(b, seq // tm),
        in_specs=[pl.BlockSpec((1, tm, d), lambda bi, t: (bi, t, 0))] + mix_specs
                 + [mod_spec, mod_spec, mod_spec, vec_spec, vec_spec,
                    pl.BlockSpec((None, None, d, dff), lambda bi, t: (l, i, 0, 0), pipeline_mode=once),
                    pl.BlockSpec((None, None, d, dff), lambda bi, t: (l, i, 0, 0), pipeline_mode=once),
                    pl.BlockSpec((None, None, dff, d), lambda bi, t: (l, i, 0, 0), pipeline_mode=once)],
        out_specs=pl.BlockSpec((1, tm, d), lambda bi, t: (bi, t, 0)),
        out_shape=jax.ShapeDtypeStruct(x.shape, f32),
        compiler_params=_cparams("arbitrary", "arbitrary"),
        name="ffn_halfstep",
    )(x, *mix_args, sh, sc, g, npre, npost, wg, wu, wd)


def _inproj_body(x_ref, sh_ref, sc_ref, npre_ref, *refs):
    n = len(refs) // 2
    h = (_rms(x_ref[0], npre_ref[...]) * (1.0 + sc_ref[0]) + sh_ref[0]).astype(bf16)
    for w_ref, o_ref in zip(refs[:n], refs[n:]):
        o_ref[0] = _dot(h, w_ref[...])


def _inproj(x, sh, sc, npre, ws, l, tm):
    b, seq, d = x.shape
    r = sh.shape[1]
    mod_spec = pl.BlockSpec((1, r, d), lambda bi, t: (bi, 0 if r == 1 else t, 0))
    return pl.pallas_call(
        _inproj_body,
        grid=(b, seq // tm),
        in_specs=[pl.BlockSpec((1, tm, d), lambda bi, t: (bi, t, 0)), mod_spec, mod_spec,
                  pl.BlockSpec((1, d), lambda bi, t: (0, 0))]
                 + [pl.BlockSpec((None,) + w.shape[1:], lambda bi, t: (l, 0, 0)) for w in ws],
        out_specs=[pl.BlockSpec((1, tm, w.shape[2]), lambda bi, t: (bi, t, 0)) for w in ws],
        out_shape=[jax.ShapeDtypeStruct((b, seq, w.shape[2]), f32) for w in ws],
        compiler_params=_cparams("arbitrary", "arbitrary"),
        name="mixer_inproj",
    )(x, sh, sc, npre, *ws)


def _rwkv_pre(p, prev, mu, w0, w2p, a0, a2p, g2p, k_k, k_a, seg):
    xs = p + (prev - p) * mu
    r = xs[:, 0:GROUP_WIDTH]
    k = xs[:, GROUP_WIDTH:2 * GROUP_WIDTH]
    v = xs[:, 2 * GROUP_WIDTH:3 * GROUP_WIDTH]
    lo = xs[:, 3 * GROUP_WIDTH:]
    w_log = -_softplus(-(w0 + _dot(jnp.tanh(lo), w2p))) - 0.5
    logw = -jnp.exp(w_log)
    a = jax.nn.sigmoid(a0 + _dot(lo, a2p))
    g = _dot(jax.nn.sigmoid(lo), g2p)
    kk = k * k_k
    kk = kk * lax.rsqrt(_segsum(kk * kk, seg) + 1e-6)
    k2 = k * (1.0 + (a - 1.0) * k_a)
    return r, logw, k2, v, kk, a, g


def _rwkv_post(y, r, k2, v, g, r_k, ln_w, ln_b, seg):
    mean = _segsum(y, seg) * (1.0 / HEAD_DIM)
    d = y - mean
    var = _segsum(d * d, seg) * (1.0 / HEAD_DIM)
    yn = d * lax.rsqrt(var + RWKV_LN_EPS) * ln_w + ln_b
    bonus = _segsum(r * k2 * r_k, seg) * v
    return (yn + bonus) * g


def _rwkv_body(p_ref, mu_ref, w0_ref, w2_ref, a0_ref, a2_ref, g2_ref, kk_ref, ka_ref, rk_ref, lnw_ref, lnb_ref,
               y_ref, s_out_ref, s_scr, carry_scr, *, tt):
    c = CHUNK
    t = pl.program_id(1)

    @pl.when(t == 0)
    def _():
        s_scr[...] = jnp.zeros_like(s_scr)
        carry_scr[...] = jnp.zeros_like(carry_scr)

    seg = _head_seg()
    p = p_ref[0]
    row = _iota2((tt, 1), 0)
    prev = jnp.where(row == 0, carry_scr[7:8, :], pltpu.roll(p, 1, axis=0))
    carry_scr[...] = p[tt - 8:tt, :]
    r, logw, k2, v, kk, a, g = _rwkv_pre(p, prev, mu_ref[...], w0_ref[...], w2_ref[...], a0_ref[...], a2_ref[...],
                                         g2_ref[...], kk_ref[...], ka_ref[...], seg)
    kka = kk * a
    ri, ci = _iota2((c, c), 0), _iota2((c, c), 1)
    incl, strict = ri >= ci, ri > ci
    lmat = incl.astype(f32)

    n_ch = tt // c
    hsl = [slice(h * HEAD_DIM, (h + 1) * HEAD_DIM) for h in range(N_HEADS)]
    vb = v.astype(bf16)

    def phase1(chs):
        pairs = [(ch, h) for ch in chs for h in range(N_HEADS)]
        at, rt, kt, bt, res = {}, {}, {}, {}, {}
        for ch in chs:
            sl = slice(ch * c, (ch + 1) * c)
            lw = logw[sl]
            gcum = _dot(lmat, lw, precision=HI)
            g_last = gcum[c - 1:c, :]
            e_ng = jnp.exp(-gcum)
            e_rem = jnp.exp(g_last - gcum)
            res["e_last", ch] = jnp.exp(g_last)
            rt[ch] = (r[sl] * jnp.exp(gcum)).astype(bf16)
            at[ch] = (-kk[sl] * jnp.exp(gcum - lw)).astype(bf16)
            kt[ch] = (k2[sl] * e_ng).astype(bf16)
            bt[ch] = (kka[sl] * e_ng).astype(bf16)
            res["kh", ch] = (k2[sl] * e_rem).astype(bf16)
            res["bh", ch] = (kka[sl] * e_rem).astype(bf16)
            res["rt", ch] = rt[ch]
        amat = [_dot_nt(jnp.concatenate([at[ch][:, hsl[h]], rt[ch][:, hsl[h]]], axis=0),
                        jnp.concatenate([kt[ch][:, hsl[h]], bt[ch][:, hsl[h]]], axis=0)) for ch, h in pairs]
        vh = [vb[ch * c:(ch + 1) * c, hsl[h]] for ch, h in pairs]
        tmat = _unit_lower_inverse([jnp.where(strict, m[:c, c:], 0.0) for m in amat], c)
        akv = [_dot(jnp.where(strict, m[:c, :c], 0.0).astype(bf16), x) for m, x in zip(amat, vh)]
        rkv = [_dot(jnp.where(incl, m[c:, :c], 0.0).astype(bf16), x) for m, x in zip(amat, vh)]
        tx = [_dot(tm.astype(bf16), jnp.concatenate([at[ch][:, hsl[h]], x.astype(bf16)], axis=1))
              for tm, x, (ch, h) in zip(tmat, akv, pairs)]
        for i, p in enumerate(pairs):
            res["vh", p], res["rkv", p], res["tx", p] = vh[i], rkv[i], tx[i]
            res["a_rb", p] = jnp.where(incl, amat[i][c:, c:], 0.0).astype(bf16)
        return res

    def phase2(chs, res, state):
        for ch in chs:
            heads = range(N_HEADS)
            sb = [x.astype(bf16) for x in state]
            u = [_dot_nt(res["tx", (ch, h)][:, :HEAD_DIM].astype(bf16), sb[h]) + res["tx", (ch, h)][:, HEAD_DIM:]
                 for h in heads]
            ub = [x.astype(bf16) for x in u]
            y_chunks.append(jnp.concatenate(
                [_dot_nt(res["rt", ch][:, hsl[h]], sb[h]) + res["rkv", (ch, h)] + _dot(res["a_rb", (ch, h)], ub[h])
                 for h in heads], axis=1))
            state = [state[h] * res["e_last", ch][:, hsl[h]]
                     + _dot_tn(jnp.concatenate([res["vh", (ch, h)], ub[h]], axis=0),
                               jnp.concatenate([res["kh", ch][:, hsl[h]], res["bh", ch][:, hsl[h]]], axis=0))
                     for h in heads]
        return state

    groups = [list(range(g, min(g + PIPE_GROUP, n_ch))) for g in range(0, n_ch, PIPE_GROUP)]
    state = [s_scr[h] for h in range(N_HEADS)]
    y_chunks = []
    pending = None
    for grp in groups:
        res = phase1(grp)
        if pending is not None:
            state = phase2(pending[0], pending[1], state)
        pending = (grp, res)
    state = phase2(pending[0], pending[1], state)
    for h in range(N_HEADS):
        s_scr[h] = state[h]
    y = jnp.concatenate(y_chunks, axis=0)
    y_ref[0] = _rwkv_post(y, r, k2, v, g, rk_ref[...], lnw_ref[...], lnb_ref[...], seg)
    s_out_ref[0] = s_scr[...]


def _rwkv_prompt(pa, prm, tt):
    b, seq, _ = pa.shape
    vec = lambda a: pl.BlockSpec(a.shape, lambda bi, t: (0, 0))
    return pl.pallas_call(
        functools.partial(_rwkv_body, tt=tt),
        grid=(b, seq // tt),
        in_specs=[pl.BlockSpec((1, tt, A_COLS), lambda bi, t: (bi, t, 0))] + [vec(a) for a in prm],
        out_specs=[pl.BlockSpec((1, tt, GROUP_WIDTH), lambda bi, t: (bi, t, 0)),
                   pl.BlockSpec((1, N_HEADS, HEAD_DIM, HEAD_DIM), lambda bi, t: (bi, 0, 0, 0))],
        out_shape=[jax.ShapeDtypeStruct((b, seq, GROUP_WIDTH), f32),
                   jax.ShapeDtypeStruct((b, N_HEADS, HEAD_DIM, HEAD_DIM), f32)],
        scratch_shapes=[pltpu.VMEM((N_HEADS, HEAD_DIM, HEAD_DIM), f32), pltpu.VMEM((8, A_COLS), f32)],
        compiler_params=_cparams("arbitrary", "arbitrary"),
        name="rwkv7_prompt",
    )(pa, *prm)


def _xpos(x, cos, sin_signed):
    n = x.shape[-1]
    even = (_iota2(x.shape, 1) % 2) == 0
    partner = jnp.where(even, pltpu.roll(x, n - 1, axis=1), pltpu.roll(x, 1, axis=1))
    return x * cos + partner * sin_signed


def _ret_body(p_ref, cos_ref, sin_ref, dmask_ref, eg_ref, erem_ref, elast_ref, y_ref, s_out_ref, s_scr, *, tt):
    c = CHUNK
    t = pl.program_id(1)

    @pl.when(t == 0)
    def _():
        s_scr[...] = jnp.zeros_like(s_scr)

    seg = _head_seg()
    p = p_ref[0]
    q = _xpos(p[:, 0:GROUP_WIDTH], cos_ref[...], sin_ref[...])
    k = _xpos(p[:, GROUP_WIDTH:2 * GROUP_WIDTH], cos_ref[...], sin_ref[...]) * HEAD_DIM ** -0.5
    v = p[:, 2 * GROUP_WIDTH:3 * GROUP_WIDTH]
    gt = p[:, 3 * GROUP_WIDTH:]
    e_last = elast_ref[...]
    n_ch = tt // c
    hsl = [slice(h * HEAD_DIM, (h + 1) * HEAD_DIM) for h in range(N_HEADS)]
    pairs = [(ch, h) for ch in range(n_ch) for h in range(N_HEADS)]
    rows = lambda x, ch, h: x[ch * c:(ch + 1) * c, hsl[h]]

    qg = [q[ch * c:(ch + 1) * c] * eg_ref[...] for ch in range(n_ch)]
    kr = [k[ch * c:(ch + 1) * c] * erem_ref[...] for ch in range(n_ch)]
    qk = [_dot_nt(rows(q, ch, h), rows(k, ch, h)) * dmask_ref[h] for ch, h in pairs]
    intra = [_dot(x, rows(v, ch, h)) for x, (ch, h) in zip(qk, pairs)]
    kv = [_dot_tn(kr[ch][:, hsl[h]], rows(v, ch, h)) for ch, h in pairs]

    state = [s_scr[h] for h in range(N_HEADS)]
    o_chunks = []
    for ch in range(n_ch):
        o_chunks.append(jnp.concatenate(
            [intra[ch * N_HEADS + h] + _dot(qg[ch][:, hsl[h]], state[h]) for h in range(N_HEADS)], axis=1))
        state = [state[h] * e_last[:, hsl[h]] + kv[ch * N_HEADS + h] for h in range(N_HEADS)]
    for h in range(N_HEADS):
        s_scr[h] = state[h]
    o = jnp.concatenate(o_chunks, axis=0)
    o = o * lax.rsqrt(_segsum(o * o, seg) * (1.0 / HEAD_DIM) + NORM_EPS)
    y_ref[0] = _silu(gt) * o
    s_out_ref[0] = s_scr[...]


def _ret_tables(seq, pos0):
    inv = 1.0 / (RET_THETA ** jnp.linspace(0.0, 1.0, HEAD_DIM // 2, dtype=f32))
    ang = (pos0 + jnp.arange(seq)).astype(f32)[:, None] * inv[None, :]
    cos = jnp.tile(jnp.repeat(jnp.cos(ang), 2, axis=1), (1, N_HEADS))
    sgn = jnp.tile(jnp.array([-1.0, 1.0], f32), HEAD_DIM // 2)
    sin = jnp.tile(jnp.repeat(jnp.sin(ang), 2, axis=1) * sgn[None, :], (1, N_HEADS))
    lg = np.log(1.0 - 2.0 ** (-5.0 - np.arange(N_HEADS, dtype=np.float64)))
    i = np.arange(CHUNK)
    diff = i[:, None] - i[None, :]
    dmask = np.where(diff[None] >= 0, np.exp(lg[:, None, None] * np.maximum(diff, 0)[None]), 0.0)
    lane = np.repeat(lg, HEAD_DIM)[None, :]
    e_g = np.exp((i[:, None] + 1) * lane)
    e_rem = np.exp((CHUNK - 1 - i[:, None]) * lane)
    e_last = np.exp(CHUNK * lane)
    e_one = np.exp(lane)
    return cos, sin, tuple(jnp.asarray(a, f32) for a in (dmask, e_g, e_rem, e_last, e_one))


def _ret_prompt(pb, cos, sin, tabs, tt):
    b, seq, _ = pb.shape
    dmask, e_g, e_rem, e_last, _ = tabs
    full = lambda a: pl.BlockSpec(a.shape, lambda bi, t: (0,) * a.ndim)
    return pl.pallas_call(
        functools.partial(_ret_body, tt=tt),
        grid=(b, seq // tt),
        in_specs=[pl.BlockSpec((1, tt, 4 * GROUP_WIDTH), lambda bi, t: (bi, t, 0)),
                  pl.BlockSpec((tt, GROUP_WIDTH), lambda bi, t: (t, 0)),
                  pl.BlockSpec((tt, GROUP_WIDTH), lambda bi, t: (t, 0)),
                  full(dmask), full(e_g), full(e_rem), full(e_last)],
        out_specs=[pl.BlockSpec((1, tt, GROUP_WIDTH), lambda bi, t: (bi, t, 0)),
                   pl.BlockSpec((1, N_HEADS, HEAD_DIM, HEAD_DIM), lambda bi, t: (bi, 0, 0, 0))],
        out_shape=[jax.ShapeDtypeStruct((b, seq, GROUP_WIDTH), f32),
                   jax.ShapeDtypeStruct((b, N_HEADS, HEAD_DIM, HEAD_DIM), f32)],
        scratch_shapes=[pltpu.VMEM((N_HEADS, HEAD_DIM, HEAD_DIM), f32)],
        compiler_params=_cparams("arbitrary", "arbitrary"),
        name="retention_prompt",
    )(pb, cos, sin, dmask, e_g, e_rem, e_last)


def _gdn_scalars(pab, alog_p, dtb_p, sel_a, sel_b):
    logd = -jnp.exp(alog_p) * _softplus(pab + dtb_p)
    beta = jax.nn.sigmoid(pab)
    return _dot(logd, sel_a, precision=HI), _dot(beta, sel_b, precision=HI)


def _gdn_qkv(conv, seg):
    cs = _silu(conv)
    q = cs[:, 0:GROUP_WIDTH]
    k = cs[:, GROUP_WIDTH:2 * GROUP_WIDTH]
    v = cs[:, 2 * GROUP_WIDTH:]
    q = q * lax.rsqrt(_segsum(q * q, seg) + 1e-6) * HEAD_DIM ** -0.5
    k = k * lax.rsqrt(_segsum(k * k, seg) + 1e-6)
    return q, k, v


def _gdn_body(pc_ref, pab_ref, cw_ref, alog_ref, dtb_ref, sela_ref, selb_ref, nw_ref,
              y_ref, s_out_ref, s_scr, ext_scr, *, tt):
    c = CHUNK
    t = pl.program_id(1)

    @pl.when(t == 0)
    def _():
        s_scr[...] = jnp.zeros_like(s_scr)
        ext_scr[pl.ds(0, 8), :] = jnp.zeros((8, C_QKV), f32)

    seg = _head_seg()
    pc = pc_ref[0]
    gate = pc[:, C_QKV:]
    ext_scr[pl.ds(8, tt), :] = pc[:, :C_QKV]
    cw = cw_ref[...]
    conv = sum(ext_scr[pl.ds(8 - (GDN_CONV - 1) + j, tt), :] * cw[j:j + 1, :] for j in range(GDN_CONV))
    ext_scr[pl.ds(0, 8), :] = pc[tt - 8:tt, :C_QKV]
    q, k, v = _gdn_qkv(conv, seg)
    logd, beta = _gdn_scalars(pab_ref[0], alog_ref[...], dtb_ref[...], sela_ref[...], selb_ref[...])
    kb = k * beta
    vb = v * beta

    ri, ci = _iota2((c, c), 0), _iota2((c, c), 1)
    incl, strict = ri >= ci, ri > ci
    lmat = incl.astype(f32)
    ones = jnp.ones((c, c), f32)
    eye4 = (_iota2((c, GROUP_WIDTH), 0) == _iota2((c, GROUP_WIDTH), 1) % HEAD_DIM).astype(f32)
    incl4 = _iota2((c, GROUP_WIDTH), 0) >= _iota2((c, GROUP_WIDTH), 1) % HEAD_DIM

    n_ch = tt // c
    hsl = [slice(h * HEAD_DIM, (h + 1) * HEAD_DIM) for h in range(N_HEADS)]
    pairs = [(ch, h) for ch in range(n_ch) for h in range(N_HEADS)]

    dmask, qg, kbg, kr, e_last = [], [], [], [], []
    for ch in range(n_ch):
        sl = slice(ch * c, (ch + 1) * c)
        gcum = _dot(lmat, logd[sl], precision=HI)
        gcum_t = _dot(ones, gcum * eye4, precision=HI)
        g_last = gcum[c - 1:c, :]
        e_g = jnp.exp(gcum)
        e_last.append(jnp.exp(g_last))
        dmask.append(jnp.exp(jnp.where(incl4, gcum - gcum_t, -jnp.inf)))
        qg.append(q[sl] * e_g)
        kbg.append(kb[sl] * e_g)
        kr.append(k[sl] * jnp.exp(g_last - gcum))
    rows = lambda x, ch, h: x[ch * c:(ch + 1) * c, hsl[h]]
    kq = [_dot_nt(jnp.concatenate([rows(kb, ch, h), rows(q, ch, h)], axis=0), rows(k, ch, h)) for ch, h in pairs]
    tmat = _unit_lower_inverse([jnp.where(strict, -x[:c] * dmask[ch][:, hsl[h]], 0.0) for x, (ch, h) in zip(kq, pairs)], c)
    aqk = [x[c:] * dmask[ch][:, hsl[h]] for x, (ch, h) in zip(kq, pairs)]
    uw = [_dot(tm, jnp.concatenate([rows(vb, ch, h), kbg[ch][:, hsl[h]]], axis=1)) for tm, (ch, h) in zip(tmat, pairs)]

    state = [s_scr[h] for h in range(N_HEADS)]
    o_chunks = []
    for ch in range(n_ch):
        idx = [ch * N_HEADS + h for h in range(N_HEADS)]
        vnew = [uw[i][:, :HEAD_DIM] - _dot(uw[i][:, HEAD_DIM:], state[h]) for h, i in enumerate(idx)]
        o_chunks.append(jnp.concatenate(
            [_dot(qg[ch][:, hsl[h]], state[h]) + _dot(aqk[i], vnew[h]) for h, i in enumerate(idx)], axis=1))
        state = [state[h] * e_last[ch][:, hsl[h]] + _dot_tn(kr[ch][:, hsl[h]], vnew[h]) for h in range(N_HEADS)]
    for h in range(N_HEADS):
        s_scr[h] = state[h]
    o = jnp.concatenate(o_chunks, axis=0)
    o = o * lax.rsqrt(_segsum(o * o, seg) * (1.0 / HEAD_DIM) + NORM_EPS) * nw_ref[...] * _silu(gate)
    y_ref[0] = o
    s_out_ref[0] = s_scr[...]


def _gdn_prompt(pc, pab, prm, tt):
    b, seq, _ = pc.shape
    vec = lambda a: pl.BlockSpec(a.shape, lambda bi, t: (0, 0))
    return pl.pallas_call(
        functools.partial(_gdn_body, tt=tt),
        grid=(b, seq // tt),
        in_specs=[pl.BlockSpec((1, tt, 4 * GROUP_WIDTH), lambda bi, t: (bi, t, 0)),
                  pl.BlockSpec((1, tt, 128), lambda bi, t: (bi, t, 0))] + [vec(a) for a in prm],
        out_specs=[pl.BlockSpec((1, tt, GROUP_WIDTH), lambda bi, t: (bi, t, 0)),
                   pl.BlockSpec((1, N_HEADS, HEAD_DIM, HEAD_DIM), lambda bi, t: (bi, 0, 0, 0))],
        out_shape=[jax.ShapeDtypeStruct((b, seq, GROUP_WIDTH), f32),
                   jax.ShapeDtypeStruct((b, N_HEADS, HEAD_DIM, HEAD_DIM), f32)],
        scratch_shapes=[pltpu.VMEM((N_HEADS, HEAD_DIM, HEAD_DIM), f32), pltpu.VMEM((tt + 8, C_QKV), f32)],
        compiler_params=_cparams("arbitrary", "arbitrary"),
        name="gdn_prompt",
    )(pc, pab, *prm)


def _top_blocks(gate, own, nb):
    blk = _iota2(gate.shape, 0).astype(f32)
    g = jnp.where(blk < own, gate, -jnp.inf)
    picks = []
    for _ in range(MOBA_TOPK):
        m = jnp.max(g, axis=0, keepdims=True)
        idx = jnp.min(jnp.where(g == m, blk, float(nb)), axis=0, keepdims=True)
        picks.append(jnp.where(m > -jnp.inf, idx, -1.0))
        g = jnp.where(blk == idx, -jnp.inf, g)
    return picks


def _moba_body(q_ref, k_ref, v_ref, o_ref, k_scr, vt_scr, mean_scr, *, nb):
    i = pl.program_id(1)
    blk_w = MOBA_BLOCK
    hsl = [slice(h * HEAD_DIM, (h + 1) * HEAD_DIM) for h in range(N_HEADS)]
    own_start = pl.multiple_of(i * blk_w, blk_w)

    @pl.when(i == 0)
    def _():
        mean_scr[...] = jnp.zeros_like(mean_scr)

    k_tile = k_ref[0]
    k_scr[pl.ds(own_start, blk_w), :] = k_tile.astype(bf16)
    mean_row = jnp.sum(k_tile, axis=0, keepdims=True) * (1.0 / blk_w)
    means = jnp.where(_iota2(mean_scr.shape, 0) == i, mean_row, mean_scr[...])
    mean_scr[...] = means
    v_t = v_ref[0].T.astype(bf16)
    for h in range(N_HEADS):
        vt_scr[h, pl.ds(0, HEAD_DIM), pl.ds(own_start, blk_w)] = v_t[hsl[h]]
        vt_scr[h, pl.ds(HEAD_DIM, SUM_ROWS), pl.ds(own_start, blk_w)] = jnp.ones((SUM_ROWS, blk_w), bf16)

    q_t = (q_ref[0] * HEAD_DIM ** -0.5).T
    qt = [q_t[hsl[h]] for h in range(N_HEADS)]
    qtb = [(x * LOG2E).astype(bf16) for x in qt]
    picks = [_top_blocks(_dot(means[:, hsl[h]], qt[h], precision=HI), i.astype(f32), nb) for h in range(N_HEADS)]

    k_own = k_scr[pl.ds(own_start, blk_w), :]
    causal = _iota2((blk_w, blk_w), 0) <= _iota2((blk_w, blk_w), 1)
    s_own = [jnp.where(causal, _dot(k_own[:, hsl[h]], qtb[h]), NEG) for h in range(N_HEADS)]
    m_own = [jnp.max(x, axis=0, keepdims=True) for x in s_own]
    p_own = [jnp.exp2(x - m).astype(bf16) for x, m in zip(s_own, m_own)]
    carry = []
    for h in range(N_HEADS):
        carry += [m_own[h], _dot(vt_scr[h, :, pl.ds(own_start, blk_w)], p_own[h])]

    def step(jj, carry):
        js = [2 * jj, 2 * jj + 1]
        starts = [pl.multiple_of(j * blk_w, blk_w) for j in js]
        kj = [k_scr[pl.ds(st, blk_w), :] for st in starts]
        units = [(h, e) for h in range(N_HEADS) for e in range(2)]
        s = [_dot(kj[e][:, hsl[h]], qtb[h]) for h, e in units]
        chosen = [functools.reduce(jnp.logical_or, [pk == js[e].astype(f32) for pk in picks[h]]) for h, e in units]
        m_blk = [jnp.where(c, jnp.max(x, axis=0, keepdims=True), NEG) for c, x in zip(chosen, s)]
        m_new = [jnp.maximum(carry[2 * h], jnp.maximum(m_blk[2 * h], m_blk[2 * h + 1])) for h in range(N_HEADS)]
        shift = [jnp.where(c, m_new[h], -NEG) for c, (h, e) in zip(chosen, units)]
        p = [jnp.exp2(x - sh).astype(bf16) for x, sh in zip(s, shift)]
        pv = [_dot(vt_scr[h, :, pl.ds(starts[e], blk_w)], x) for x, (h, e) in zip(p, units)]
        out = []
        for h in range(N_HEADS):
            out += [m_new[h], jnp.exp2(carry[2 * h] - m_new[h]) * carry[2 * h + 1] + (pv[2 * h] + pv[2 * h + 1])]
        return tuple(out)

    carry = lax.fori_loop(0, (i + 1) // 2, step, tuple(carry))
    o_t = jnp.concatenate([carry[2 * h + 1][:HEAD_DIM] / carry[2 * h + 1][HEAD_DIM:HEAD_DIM + 1]
                           for h in range(N_HEADS)], axis=0)
    o_ref[0] = o_t.T


def _moba_prompt(q, k, v):
    b, seq, _ = k.shape
    nb = seq // MOBA_BLOCK
    assert seq % MOBA_BLOCK == 0
    tile = pl.BlockSpec((1, MOBA_BLOCK, GROUP_WIDTH), lambda bi, i: (bi, i, 0))
    return pl.pallas_call(
        functools.partial(_moba_body, nb=nb),
        grid=(b, nb),
        in_specs=[tile, tile, tile],
        out_specs=tile,
        out_shape=jax.ShapeDtypeStruct((b, seq, GROUP_WIDTH), f32),
        scratch_shapes=[pltpu.VMEM((seq, GROUP_WIDTH), bf16),
                        pltpu.VMEM((N_HEADS, HEAD_DIM + SUM_ROWS, seq), bf16),
                        pltpu.VMEM((nb, GROUP_WIDTH), f32)],
        compiler_params=_cparams("arbitrary", "arbitrary"),
        name="moba_prompt",
    )(q, k, v)


def _row_to_col(row, eye):
    return jnp.sum(eye * row, axis=1, keepdims=True)


def _col_to_row(col, eye):
    return jnp.sum(eye * col, axis=0, keepdims=True)


def _sample_mix_body(pa_ref, shift_ref, pb_ref, pc_ref, pab_ref, conv_ref, srw_ref, srt_ref, sgd_ref,
                     mu_ref, w0_ref, w2_ref, a0_ref, a2_ref, g2_ref, kk_ref, ka_ref, rk_ref, lnw_ref, lnb_ref,
                     cos_ref, sin_ref, eone_ref, cw_ref, alog_ref, dtb_ref, sela_ref, selb_ref, nw_ref,
                     ya_ref, yb_ref, yc_ref, srw_out, srt_out, sgd_out):
    seg = _head_seg()
    eye = _eye(HEAD_DIM)
    hsl = [slice(h * HEAD_DIM, (h + 1) * HEAD_DIM) for h in range(N_HEADS)]

    r, logw, k2, v, kk, a, g = _rwkv_pre(pa_ref[0], shift_ref[0], mu_ref[...], w0_ref[...], w2_ref[...], a0_ref[...],
                                         a2_ref[...], g2_ref[...], kk_ref[...], ka_ref[...], seg)
    w = jnp.exp(logw)
    kka = kk * a
    ys = []
    for h, hs in enumerate(hsl):
        s_old = srw_ref[0, h]
        sa = jnp.sum(s_old * (-kk[:, hs]), axis=1, keepdims=True)
        s_new = s_old * w[:, hs] + sa * kka[:, hs] + _row_to_col(v[:, hs], eye) * k2[:, hs]
        srw_out[0, h] = s_new
        ys.append(_col_to_row(jnp.sum(s_new * r[:, hs], axis=1, keepdims=True), eye))
    ya_ref[0] = _rwkv_post(jnp.concatenate(ys, axis=1), r, k2, v, g, rk_ref[...], lnw_ref[...], lnb_ref[...], seg)

    pb = pb_ref[0]
    q = _xpos(pb[:, 0:GROUP_WIDTH], cos_ref[...], sin_ref[...])
    k = _xpos(pb[:, GROUP_WIDTH:2 * GROUP_WIDTH], cos_ref[...], sin_ref[...]) * HEAD_DIM ** -0.5
    v = pb[:, 2 * GROUP_WIDTH:3 * GROUP_WIDTH]
    e_one = eone_ref[...]
    os_ = []
    for h, hs in enumerate(hsl):
        s_new = srt_ref[0, h] * e_one[:, hs] + _row_to_col(k[:, hs], eye) * v[:, hs]
        srt_out[0, h] = s_new
        os_.append(jnp.sum(_row_to_col(q[:, hs], eye) * s_new, axis=0, keepdims=True))
    o = jnp.concatenate(os_, axis=1)
    o = o * lax.rsqrt(_segsum(o * o, seg) * (1.0 / HEAD_DIM) + NORM_EPS)
    yb_ref[0] = _silu(pb[:, 3 * GROUP_WIDTH:]) * o

    pc = pc_ref[0]
    cw = cw_ref[...]
    prev = conv_ref[0]
    conv = pc[:, :C_QKV] * cw[GDN_CONV - 1:GDN_CONV, :]
    for j in range(GDN_CONV - 1):
        conv = conv + prev[j:j + 1, :] * cw[j:j + 1, :]
    q, k, v = _gdn_qkv(conv, seg)
    logd, beta = _gdn_scalars(pab_ref[0], alog_ref[...], dtb_ref[...], sela_ref[...], selb_ref[...])
    alpha = jnp.exp(logd)
    os_ = []
    for h, hs in enumerate(hsl):
        s_old = sgd_ref[0, h]
        k_col = _row_to_col(k[:, hs], eye)
        ks = jnp.sum(k_col * s_old, axis=0, keepdims=True)
        s_new = alpha[:, hs] * s_old + k_col * (beta[:, hs] * (v[:, hs] - alpha[:, hs] * ks))
        sgd_out[0, h] = s_new
        os_.append(jnp.sum(_row_to_col(q[:, hs], eye) * s_new, axis=0, keepdims=True))
    o = jnp.concatenate(os_, axis=1)
    yc_ref[0] = o * lax.rsqrt(_segsum(o * o, seg) * (1.0 / HEAD_DIM) + NORM_EPS) * nw_ref[...] * _silu(pc[:, C_QKV:])


def _sample_mixers(pa, shift, pb, pc, pab, conv_prev, s_rwkv, s_ret, s_gdn, rwkv_prm, ret_prm, gdn_prm):
    nseq = pa.shape[0]
    row = lambda a: pl.BlockSpec((1, 1, a.shape[-1]), lambda b: (b, 0, 0))
    state = pl.BlockSpec((1, N_HEADS, HEAD_DIM, HEAD_DIM), lambda b: (b, 0, 0, 0))
    full = lambda a: pl.BlockSpec(a.shape, lambda b: (0,) * a.ndim)
    r3 = lambda a: a.reshape(nseq, 1, a.shape[-1])
    prm = tuple(rwkv_prm) + tuple(ret_prm) + tuple(gdn_prm)
    outs = pl.pallas_call(
        _sample_mix_body,
        grid=(nseq,),
        in_specs=[row(pa), row(shift), row(pb), row(pc), row(pab),
                  pl.BlockSpec((1, GDN_CONV - 1, C_QKV), lambda b: (b, 0, 0)), state, state, state]
                 + [full(a) for a in prm],
        out_specs=[row(pa[..., :GROUP_WIDTH])] * 3 + [state] * 3,
        out_shape=[jax.ShapeDtypeStruct((nseq, 1, GROUP_WIDTH), f32)] * 3
                  + [jax.ShapeDtypeStruct((nseq, N_HEADS, HEAD_DIM, HEAD_DIM), f32)] * 3,
        compiler_params=_cparams("arbitrary"),
        name="sample_mixers",
    )(r3(pa), r3(shift), r3(pb), r3(pc), r3(pab), conv_prev, s_rwkv, s_ret, s_gdn, *prm)
    return outs


def _moba_sample_body(pt_ref, q_ref, qb_ref, kn_ref, vn_ref, *refs, nb, bps):
    del pt_ref
    npg = bps * PAGES_PER_BLOCK
    k_refs, v_refs = refs[:npg], refs[npg:2 * npg]
    o_ref, gate_scr, m_scr, l_scr, acc_scr = refs[2 * npg:]
    j = pl.program_id(1)
    hmask = (_iota2((N_HEADS, GROUP_WIDTH), 1) // HEAD_DIM == _iota2((N_HEADS, GROUP_WIDTH), 0)).astype(f32)
    qb = qb_ref[0]
    qb = jnp.concatenate([qb] * (MOBA_BLOCK // 128), axis=1)
    block_t = lambda page_refs: jnp.concatenate(
        [r[...].reshape(GROUP_WIDTH, PAGE_SIZE) for r in page_refs], axis=1)
    blocks = [slice(e * PAGES_PER_BLOCK, (e + 1) * PAGES_PER_BLOCK) for e in range(bps)]
    kt = [block_t(k_refs[pg]) for pg in blocks]
    s = [jnp.sum((x * qb).reshape(N_HEADS, HEAD_DIM, MOBA_BLOCK), axis=1) for x in kt]
    gate = [jnp.sum(x, axis=1, keepdims=True) * (1.0 / MOBA_BLOCK) for x in s]
    m = [jnp.max(x, axis=1, keepdims=True) for x in s]
    p = [jnp.exp(x - mx) for x, mx in zip(s, m)]
    pv = [_dot_nt(x, block_t(v_refs[pg])) for x, pg in zip(p, blocks)]
    for e in range(bps):
        slot = j * bps + e
        gate_scr[slot] = jnp.broadcast_to(gate[e], (N_HEADS, 128))
        m_scr[slot] = jnp.broadcast_to(m[e], (N_HEADS, 128))
        l_scr[slot] = jnp.broadcast_to(jnp.sum(p[e], axis=1, keepdims=True), (N_HEADS, 128))
        acc_scr[slot] = pv[e]

    @pl.when(j == nb // bps - 1)
    def _():
        g = gate_scr[...]
        blk = _iota2(g.shape, 0).astype(f32)
        sel = jnp.zeros(g.shape, jnp.bool_)
        for _ in range(min(MOBA_TOPK, nb)):
            gm = jnp.max(g, axis=0, keepdims=True)
            idx = jnp.min(jnp.where(g == gm, blk, float(nb)), axis=0, keepdims=True)
            pick = blk == idx
            sel = jnp.logical_or(sel, pick)
            g = jnp.where(pick, -jnp.inf, g)
        q_own = hmask * (q_ref[0] * HEAD_DIM ** -0.5)
        s_own = jnp.broadcast_to(jnp.sum(q_own * kn_ref[0], axis=1, keepdims=True), (N_HEADS, 128))
        mb = jnp.where(sel, m_scr[...], NEG)
        m_all = jnp.maximum(jnp.max(mb, axis=0), s_own)
        wgt = jnp.where(sel, jnp.exp(mb - m_all[None]), 0.0)
        w_own = jnp.exp(s_own - m_all)
        denom = jnp.sum(wgt * l_scr[...], axis=0) + w_own
        wide = lambda a: jnp.concatenate([a, a], axis=-1)
        num = jnp.sum(wide(wgt) * acc_scr[...], axis=0) + wide(w_own) * vn_ref[0]
        o_ref[0] = jnp.sum(hmask * (num / wide(denom)), axis=0, keepdims=True)


def _moba_sample(q, k_new, v_new, cache_kt, cache_vt, page_table, layer, bps):
    nseq, n_pages = page_table.shape
    nb = n_pages // PAGES_PER_BLOCK
    assert n_pages % PAGES_PER_BLOCK == 0 and nb % bps == 0
    npg = bps * PAGES_PER_BLOCK
    row = pl.BlockSpec((1, 1, GROUP_WIDTH), lambda b, j, pt: (b, 0, 0))
    page = lambda e: pl.BlockSpec((None, None, N_HEADS, HEAD_DIM, PAGE_SIZE),
                                  lambda b, j, pt: (layer, pt[b, npg * j + e], 0, 0, 0))
    pages = [page(e) for e in range(npg)]
    r3 = lambda a: a.reshape(nseq, 1, GROUP_WIDTH)
    q_lanes = jnp.broadcast_to((q * HEAD_DIM ** -0.5)[:, :, None], (nseq, GROUP_WIDTH, 128))
    return pl.pallas_call(
        functools.partial(_moba_sample_body, nb=nb, bps=bps),
        grid_spec=pltpu.PrefetchScalarGridSpec(
            num_scalar_prefetch=1,
            grid=(nseq, nb // bps),
            in_specs=[row, pl.BlockSpec((1, GROUP_WIDTH, 128), lambda b, j, pt: (b, 0, 0)), row, row] + pages + pages,
            out_specs=row,
            scratch_shapes=[pltpu.VMEM((nb, N_HEADS, 128), f32)] * 3 + [pltpu.VMEM((nb, N_HEADS, GROUP_WIDTH), f32)]),
        out_shape=jax.ShapeDtypeStruct((nseq, 1, GROUP_WIDTH), f32),
        compiler_params=_cparams("arbitrary", "arbitrary"),
        name="moba_sample",
    )(page_table, r3(q), q_lanes, r3(k_new), r3(v_new), *([cache_kt] * npg), *([cache_vt] * npg)
      ).reshape(nseq, GROUP_WIDTH)


def _pad_rows(w, start, total):
    return jnp.zeros((total, w.shape[1]), w.dtype).at[start:start + w.shape[0]].set(w)


def _mixer_params(l, rwkv_mu, rwkv_w0, rwkv_w2, rwkv_a0, rwkv_a2, rwkv_g2, rwkv_k_k, rwkv_k_a, rwkv_r_k, rwkv_ln_w,
                  rwkv_ln_b, gdn_conv_w, gdn_a_log, gdn_dt_bias, gdn_norm_w):
    row = lambda a: a.reshape(1, -1)
    rwkv = (row(rwkv_mu[l]), row(rwkv_w0[l]), _pad_rows(rwkv_w2[l], 0, LORA_W), row(rwkv_a0[l]),
            _pad_rows(rwkv_a2[l], RWKV_W_LORA, LORA_W), _pad_rows(rwkv_g2[l], RWKV_W_LORA + RWKV_A_LORA, LORA_W),
            row(rwkv_k_k[l]), row(rwkv_k_a[l]), row(rwkv_r_k[l]), row(rwkv_ln_w[l]), row(rwkv_ln_b[l]))
    lane = np.arange(128)[:, None]
    col_head = (np.arange(GROUP_WIDTH) // HEAD_DIM)[None, :]
    sel_a = jnp.asarray(lane == col_head, f32)
    sel_b = jnp.asarray(lane == col_head + N_HEADS, f32)
    pad128 = lambda a: jnp.zeros((1, 128), f32).at[0, :N_HEADS].set(a)
    gdn = (gdn_conv_w[l], pad128(gdn_a_log[l]), pad128(gdn_dt_bias[l]), sel_a, sel_b,
           jnp.tile(gdn_norm_w[l], N_HEADS).reshape(1, GROUP_WIDTH))
    return rwkv, gdn


def kernel(x_prompt, x_sample, c_prompt, c_sample, cache_k, cache_v, page_table, state_rwkv, state_rwkv_shift,
           state_ret, state_gdn, state_gdn_conv, w_mod, b_mod, norm_pre, norm_post, ffn_w_gate, ffn_w_up, ffn_w_down,
           w_in, w_out, rwkv_mu, rwkv_w0, rwkv_w2, rwkv_a0, rwkv_a2, rwkv_g2, rwkv_k_k, rwkv_k_a, rwkv_r_k, rwkv_ln_w,
           rwkv_ln_b, gdn_conv_w, gdn_a_log, gdn_dt_bias, gdn_norm_w):
    depth = w_in.shape[0]
    bp, seq, d = x_prompt.shape
    bs = x_sample.shape[0]
    assert x_sample.shape[1] == 1
    past_len = page_table.shape[1] * PAGE_SIZE
    assert past_len % MOBA_BLOCK == 0
    cache_kt = jnp.transpose(cache_k, (0, 1, 3, 4, 2))
    cache_vt = jnp.transpose(cache_v, (0, 1, 3, 4, 2))

    n_c = bp + bs
    c_all = jnp.concatenate([c_prompt, c_sample, jnp.zeros((-n_c % 8, d), f32)], axis=0)
    mods = _modulation(c_all, w_mod, b_mod)

    wg, wu, wdn = ffn_w_gate.astype(bf16), ffn_w_up.astype(bf16), ffn_w_down.astype(bf16)
    w_in_b, w_out_b = w_in.astype(bf16), w_out.astype(bf16)
    a0, b0 = A_COLS, A_COLS + 4 * GROUP_WIDTH
    c0 = b0 + C_QKV + GROUP_WIDTH
    d0 = c0 + 2 * N_HEADS
    col_groups = [(0, a0), (a0, b0), (b0, c0), (d0, d0 + GROUP_WIDTH), (d0 + GROUP_WIDTH, d0 + 2 * GROUP_WIDTH),
                  (d0 + 2 * GROUP_WIDTH, d0 + 3 * GROUP_WIDTH)]

    ws = [w_in_b[:, :, lo:hi] for lo, hi in col_groups]
    ws.append(jnp.pad(w_in_b[:, :, c0:d0], ((0, 0), (0, 0), (0, 128 - 2 * N_HEADS))))
    cos_p, sin_p, ret_tabs = _ret_tables(seq, 0)
    cos_s, sin_s, _ = _ret_tables(1, past_len)
    tm = 512
    tm_ffn = 1024
    tt = 256

    xp, xs = x_prompt, x_sample.reshape(1, bs, d)
    news_p, news_s = [], []
    for l in range(depth):
        mp = [m.reshape(bp, 1, d) for m in jnp.split(mods[l, :bp], 9, axis=-1)]
        ms = [m.reshape(1, bs, d) for m in jnp.split(mods[l, bp:n_c], 9, axis=-1)]
        npre = [norm_pre[l, i].reshape(1, d) for i in range(3)]
        npost = [norm_post[l, i].reshape(1, d) for i in range(3)]
        rwkv_prm, gdn_prm = _mixer_params(l, rwkv_mu, rwkv_w0, rwkv_w2, rwkv_a0, rwkv_a2, rwkv_g2, rwkv_k_k, rwkv_k_a,
                                          rwkv_r_k, rwkv_ln_w, rwkv_ln_b, gdn_conv_w, gdn_a_log, gdn_dt_bias, gdn_norm_w)

        xp = _ffn(xp, mp[0], mp[1], mp[2], npre[0], npost[0], wg, wu, wdn, l, 0, tm_ffn, FFN_TF)
        pa, pb, pc, qd, kd, vd, pab = _inproj(xp, mp[3], mp[4], npre[1], ws, l, tm)
        ya, rw_new = _rwkv_prompt(pa, rwkv_prm, 2 * tt)
        yb, rt_new = _ret_prompt(pb, cos_p, sin_p, ret_tabs, tt)
        yc, gd_new = _gdn_prompt(pc, pab, gdn_prm, tt)
        yd = _moba_prompt(qd, kd, vd)
        xp = _ffn(xp, mp[6], mp[7], mp[8], npre[2], npost[2], wg, wu, wdn, l, 1, tm_ffn, FFN_TF,
                  mix=((ya, yb, yc, yd), mp[5], npost[1], w_out_b))
        hd = lambda a: a.reshape(a.shape[0], a.shape[1], N_HEADS, HEAD_DIM)
        news_p.append((hd(kd), hd(vd), rw_new, pa[:, -1], rt_new, gd_new, pc[:, seq - (GDN_CONV - 1):, :C_QKV]))

        xs = _ffn(xs, ms[0], ms[1], ms[2], npre[0], npost[0], wg, wu, wdn, l, 0, bs, FFN_TF)
        pa, pb, pc, qd, kd, vd, pab = (a[0] for a in _inproj(xs, ms[3], ms[4], npre[1], ws, l, bs))
        ya, yb, yc, rw_new, rt_new, gd_new = _sample_mixers(
            pa, state_rwkv_shift[l], pb, pc, pab, state_gdn_conv[l], state_rwkv[l], state_ret[l], state_gdn[l],
            rwkv_prm, (cos_s, sin_s, ret_tabs[4]), gdn_prm)
        yd = _moba_sample(qd, kd, vd, cache_kt, cache_vt, page_table, l, 16)
        ys = tuple(a.reshape(1, bs, GROUP_WIDTH) for a in (ya, yb, yc, yd))
        xs = _ffn(xs, ms[6], ms[7], ms[8], npre[2], npost[2], wg, wu, wdn, l, 1, bs, FFN_TF,
                  mix=(ys, ms[5], npost[1], w_out_b))
        hs_ = lambda a: a.reshape(bs, 1, N_HEADS, HEAD_DIM)
        conv_new = jnp.concatenate([state_gdn_conv[l][:, 1:], pc[:, None, :C_QKV]], axis=1)
        news_s.append((hs_(kd), hs_(vd), rw_new, pa, rt_new, gd_new, conv_new))

    kp, vp, rwp, shp, rtp, gdp, cvp = [jnp.stack(t) for t in zip(*news_p)]
    ks, vs, rws, shs, rts, gds, cvs = [jnp.stack(t) for t in zip(*news_s)]
    return (xp, xs.reshape(bs, 1, d), kp, vp, ks, vs, rwp, rws, shp, shs, rtp, rts, gdp, gds, cvp, cvs)
```

```python
import functools
import math

import numpy as np
import jax
import jax.numpy as jnp
from jax import lax
from jax.experimental import pallas as pl
from jax.experimental.pallas import tpu as pltpu

f32 = jnp.float32
bf16 = jnp.bfloat16
HI = lax.Precision.HIGHEST

GROUP_WIDTH = 256
HEAD_DIM = 64
N_HEADS = 4
FFN_RES = 0.5
NORM_EPS = 1e-6
RWKV_W_LORA = 32
RWKV_A_LORA = 32
RWKV_G_LORA = 64
RWKV_LN_EPS = 64e-5
RET_THETA = 10000.0
GDN_CONV = 4
MOBA_BLOCK = 256
MOBA_TOPK = 3
SUM_ROWS = 16
PAGE_SIZE = 128
PAGES_PER_BLOCK = MOBA_BLOCK // PAGE_SIZE
CHUNK = 64
PIPE_GROUP = 4
FFN_TF = 256
INV_BLOCK = 16
LORA_W = RWKV_W_LORA + RWKV_A_LORA + RWKV_G_LORA
A_COLS = 3 * GROUP_WIDTH + LORA_W
C_QKV = 3 * GROUP_WIDTH
NEG = -1e30
LOG2E = math.log2(math.e)
VMEM_LIMIT = 56 * 1024 * 1024


def _cparams(*sem):
    return pltpu.CompilerParams(dimension_semantics=sem, vmem_limit_bytes=VMEM_LIMIT)


def _dot(a, b, precision=None):
    return jnp.dot(a, b, preferred_element_type=f32, precision=precision)


def _dot_nt(a, b, precision=None):
    return lax.dot_general(a, b, (((1,), (1,)), ((), ())), preferred_element_type=f32, precision=precision)


def _dot_tn(a, b, precision=None):
    return lax.dot_general(a, b, (((0,), (0,)), ((), ())), preferred_element_type=f32, precision=precision)


def _rms(x, g):
    return x * lax.rsqrt(jnp.mean(x * x, axis=-1, keepdims=True) + NORM_EPS) * g


def _silu(x):
    return x * jax.nn.sigmoid(x)


def _softplus(x):
    return jnp.maximum(x, 0.0) + jnp.log(1.0 + jnp.exp(-jnp.abs(x)))


def _iota2(shape, axis):
    return lax.broadcasted_iota(jnp.int32, shape, axis)


def _head_seg():
    return (_iota2((GROUP_WIDTH, GROUP_WIDTH), 0) // HEAD_DIM == _iota2((GROUP_WIDTH, GROUP_WIDTH), 1) // HEAD_DIM).astype(f32)


def _split3(x):
    x1 = x.astype(bf16)
    r1 = x - x1.astype(f32)
    x2 = r1.astype(bf16)
    return x1, x2, (r1 - x2.astype(f32)).astype(bf16)


def _dot_exact_rhs(a, b01):
    bb = b01.astype(bf16)
    a1, a2, a3 = _split3(a)
    return (_dot(a1, bb) + _dot(a2, bb)) + _dot(a3, bb)


def _dot_exact_lhs(a01, b):
    ab = a01.astype(bf16)
    b1, b2, b3 = _split3(b)
    return (_dot(ab, b1) + _dot(ab, b2)) + _dot(ab, b3)


def _segsum(x, seg):
    return _dot_exact_rhs(x, seg)


def _eye(n):
    return (_iota2((n, n), 0) == _iota2((n, n), 1)).astype(f32)


def _split(x):
    hi = x.astype(bf16)
    return hi, (x - hi.astype(f32)).astype(bf16)


def _dot3(a, b):
    return _dot(a[0], b[0]) + (_dot(a[0], b[1]) + _dot(a[1], b[0]))


def _bdot(a, b):
    return _dot(a.astype(bf16), b.astype(bf16))


def _bdot_nt(a, b):
    return _dot_nt(a.astype(bf16), b.astype(bf16))


def _bdot_tn(a, b):
    return _dot_tn(a.astype(bf16), b.astype(bf16))


def _nilpotent_inverse(ns, eye, index):
    ts = [eye + n for n in ns]
    ps = [_split(n) for n in ns]
    for _ in range(int(math.log2(index)) - 1):
        ps = [_split(_dot3(p, p)) for p in ps]
        tsp = [_split(t) for t in ts]
        ts = [t + _dot3(p, tp) for t, p, tp in zip(ts, ps, tsp)]
    return ts


def _unit_lower_inverse(ns, c):
    eye = _eye(c)
    same = _iota2((c, c), 0) // INV_BLOCK == _iota2((c, c), 1) // INV_BLOCK
    t_diag = _nilpotent_inverse([jnp.where(same, n, 0.0) for n in ns], eye, INV_BLOCK)
    tds = [t.astype(bf16) for t in t_diag]
    ps = [_dot(td, jnp.where(same, 0.0, n).astype(bf16)) for td, n in zip(tds, ns)]
    t_off = [eye + p for p in ps]
    pw = [p.astype(bf16) for p in ps]
    for _ in range(int(math.log2(c // INV_BLOCK)) - 1):
        pw = [_dot(p, p).astype(bf16) for p in pw]
        t_off = [t + _dot(p, t.astype(bf16)) for t, p in zip(t_off, pw)]
    return [_dot(to.astype(bf16), td) for to, td in zip(t_off, tds)]


def _mod_body(c_ref, w_ref, b_ref, o_ref):
    s = _silu(c_ref[...]).astype(bf16)
    o_ref[0] = _dot(s, w_ref[0].astype(bf16)) + b_ref[0]


def _modulation(c_all, w_mod, b_mod):
    depth, d, n9 = w_mod.shape
    rows = c_all.shape[0]
    tn = 1152
    return pl.pallas_call(
        _mod_body,
        grid=(depth, n9 // tn),
        in_specs=[pl.BlockSpec((rows, d), lambda l, j: (0, 0)),
                  pl.BlockSpec((1, d, tn), lambda l, j: (l, 0, j)),
                  pl.BlockSpec((1, 1, tn), lambda l, j: (l, 0, j))],
        out_specs=pl.BlockSpec((1, rows, tn), lambda l, j: (l, 0, j)),
        out_shape=jax.ShapeDtypeStruct((depth, rows, n9), f32),
        compiler_params=_cparams("arbitrary", "arbitrary"),
        name="modulation",
    )(c_all, w_mod, b_mod.reshape(depth, 1, n9))


def _ffn_body(x_ref, *refs, tf, mixed):
    if mixed:
        ya_ref, yb_ref, yc_ref, yd_ref, gm_ref, npm_ref, wo_ref = refs[:7]
        refs = refs[7:]
    sh_ref, sc_ref, g_ref, npre_ref, npost_ref, wg_ref, wu_ref, wd_ref, o_ref = refs
    x = x_ref[0]
    if mixed:
        y = jnp.concatenate([ya_ref[0], yb_ref[0], yc_ref[0], yd_ref[0]], axis=-1).astype(bf16)
        x = x + gm_ref[0] * _rms(_dot(y, wo_ref[...]), npm_ref[...])
    h = (_rms(x, npre_ref[...]) * (1.0 + sc_ref[0]) + sh_ref[0]).astype(bf16)
    acc = None
    for j in range(wg_ref.shape[1] // tf):
        cs = slice(j * tf, (j + 1) * tf)
        act = (_silu(_dot(h, wg_ref[:, cs])) * _dot(h, wu_ref[:, cs])).astype(bf16)
        part = _dot(act, wd_ref[cs, :])
        acc = part if acc is None else acc + part
    o_ref[0] = x + FFN_RES * g_ref[0] * _rms(acc, npost_ref[...])


def _ffn(x, sh, sc, g, npre, npost, wg, wu, wd, l, i, tm, tf, mix=None):
    b, seq, d = x.shape
    r = sh.shape[1]
    dff = wg.shape[-1]
    mod_spec = pl.BlockSpec((1, r, d), lambda bi, t: (bi, 0 if r == 1 else t, 0))
    vec_spec = pl.BlockSpec((1, d), lambda bi, t: (0, 0))
    once = pl.Buffered(1)
    mix_specs, mix_args = [], []
    if mix is not None:
        ys, gm, npm, w_out = mix
        y_spec = pl.BlockSpec((1, tm, GROUP_WIDTH), lambda bi, t: (bi, t, 0))
        mix_specs = [y_spec] * len(ys) + [mod_spec, vec_spec,
                                          pl.BlockSpec((None, d, d), lambda bi, t: (l, 0, 0), pipeline_mode=once)]
        mix_args = [*ys, gm, npm, w_out]
    return pl.pallas_call(
        functools.partial(_ffn_body, tf=tf, mixed=mix is not None),
        grid=(b, seq // tm),
        in_specs=[pl.BlockSpec((1, tm, d), lambda bi, t: (bi, t, 0))] + mix_specs
                 + [mod_spec, mod_spec, mod_spec, vec_spec, vec_spec,
                    pl.BlockSpec((None, None, d, dff), lambda bi, t: (l, i, 0, 0), pipeline_mode=once),
                    pl.BlockSpec((None, None, d, dff), lambda bi, t: (l, i, 0, 0), pipeline_mode=once),
                    pl.BlockSpec((None, None, dff, d), lambda bi, t: (l, i, 0, 0), pipeline_mode=once)],
        out_specs=pl.BlockSpec((1, tm, d), lambda bi, t: (bi, t, 0)),
        out_shape=jax.ShapeDtypeStruct(x.shape, f32),
        compiler_params=_cparams("arbitrary", "arbitrary"),
        name="ffn_halfstep",
    )(x, *mix_args, sh, sc, g, npre, npost, wg, wu, wd)


def _inproj_body(x_ref, sh_ref, sc_ref, npre_ref, *refs):
    n = len(refs) // 2
    h = (_rms(x_ref[0], npre_ref[...]) * (1.0 + sc_ref[0]) + sh_ref[0]).astype(bf16)
    for w_ref, o_ref in zip(refs[:n], refs[n:]):
        o_ref[0] = _dot(h, w_ref[...])


def _inproj(x, sh, sc, npre, ws, l, tm):
    b, seq, d = x.shape
    r = sh.shape[1]
    mod_spec = pl.BlockSpec((1, r, d), lambda bi, t: (bi, 0 if r == 1 else t, 0))
    return pl.pallas_call(
        _inproj_body,
        grid=(b, seq // tm),
        in_specs=[pl.BlockSpec((1, tm, d), lambda bi, t: (bi, t, 0)), mod_spec, mod_spec,
                  pl.BlockSpec((1, d), lambda bi, t: (0, 0))]
                 + [pl.BlockSpec((None,) + w.shape[1:], lambda bi, t: (l, 0, 0)) for w in ws],
        out_specs=[pl.BlockSpec((1, tm, w.shape[2]), lambda bi, t: (bi, t, 0)) for w in ws],
        out_shape=[jax.ShapeDtypeStruct((b, seq, w.shape[2]), f32) for w in ws],
        compiler_params=_cparams("arbitrary", "arbitrary"),
        name="mixer_inproj",
    )(x, sh, sc, npre, *ws)


def _rwkv_pre(p, prev, mu, w0, w2p, a0, a2p, g2p, k_k, k_a, seg):
    xs = p + (prev - p) * mu
    r = xs[:, 0:GROUP_WIDTH]
    k = xs[:, GROUP_WIDTH:2 * GROUP_WIDTH]
    v = xs[:, 2 * GROUP_WIDTH:3 * GROUP_WIDTH]
    lo = xs[:, 3 * GROUP_WIDTH:]
    w_log = -_softplus(-(w0 + _dot(jnp.tanh(lo), w2p))) - 0.5
    logw = -jnp.exp(w_log)
    a = jax.nn.sigmoid(a0 + _dot(lo, a2p))
    g = _dot(jax.nn.sigmoid(lo), g2p)
    kk = k * k_k
    kk = kk * lax.rsqrt(_segsum(kk * kk, seg) + 1e-6)
    k2 = k * (1.0 + (a - 1.0) * k_a)
    return r, logw, k2, v, kk, a, g


def _rwkv_post(y, r, k2, v, g, r_k, ln_w, ln_b, seg):
    mean = _segsum(y, seg) * (1.0 / HEAD_DIM)
    d = y - mean
    var = _segsum(d * d, seg) * (1.0 / HEAD_DIM)
    yn = d * lax.rsqrt(var + RWKV_LN_EPS) * ln_w + ln_b
    bonus = _segsum(r * k2 * r_k, seg) * v
    return (yn + bonus) * g


def _rwkv_body(p_ref, mu_ref, w0_ref, w2_ref, a0_ref, a2_ref, g2_ref, kk_ref, ka_ref, rk_ref, lnw_ref, lnb_ref,
               y_ref, s_out_ref, s_scr, carry_scr, *, tt):
    c = CHUNK
    t = pl.program_id(1)

    @pl.when(t == 0)
    def _():
        s_scr[...] = jnp.zeros_like(s_scr)
        carry_scr[...] = jnp.zeros_like(carry_scr)

    seg = _head_seg()
    p = p_ref[0]
    row = _iota2((tt, 1), 0)
    prev = jnp.where(row == 0, carry_scr[7:8, :], pltpu.roll(p, 1, axis=0))
    carry_scr[...] = p[tt - 8:tt, :]
    r, logw, k2, v, kk, a, g = _rwkv_pre(p, prev, mu_ref[...], w0_ref[...], w2_ref[...], a0_ref[...], a2_ref[...],
                                         g2_ref[...], kk_ref[...], ka_ref[...], seg)
    kka = kk * a
    ri, ci = _iota2((c, c), 0), _iota2((c, c), 1)
    incl, strict = ri >= ci, ri > ci
    lmat = incl.astype(f32)

    n_ch = tt // c
    hsl = [slice(h * HEAD_DIM, (h + 1) * HEAD_DIM) for h in range(N_HEADS)]
    vb = v.astype(bf16)

    def phase1(chs):
        pairs = [(ch, h) for ch in chs for h in range(N_HEADS)]
        at, rt, kt, bt, res = {}, {}, {}, {}, {}
        for ch in chs:
            sl = slice(ch * c, (ch + 1) * c)
            lw = logw[sl]
            gcum = _dot_exact_lhs(lmat, lw)
            g_last = gcum[c - 1:c, :]
            e_ng = jnp.exp(-gcum)
            e_rem = jnp.exp(g_last - gcum)
            res["e_last", ch] = jnp.exp(g_last)
            rt[ch] = (r[sl] * jnp.exp(gcum)).astype(bf16)
            at[ch] = (-kk[sl] * jnp.exp(gcum - lw)).astype(bf16)
            kt[ch] = (k2[sl] * e_ng).astype(bf16)
            bt[ch] = (kka[sl] * e_ng).astype(bf16)
            res["kh", ch] = (k2[sl] * e_rem).astype(bf16)
            res["bh", ch] = (kka[sl] * e_rem).astype(bf16)
            res["rt", ch] = rt[ch]
        amat = [_dot_nt(jnp.concatenate([at[ch][:, hsl[h]], rt[ch][:, hsl[h]]], axis=0),
                        jnp.concatenate([kt[ch][:, hsl[h]], bt[ch][:, hsl[h]]], axis=0)) for ch, h in pairs]
        vh = [vb[ch * c:(ch + 1) * c, hsl[h]] for ch, h in pairs]
        tmat = _unit_lower_inverse([jnp.where(strict, m[:c, c:], 0.0) for m in amat], c)
        akv = [_dot(jnp.where(strict, m[:c, :c], 0.0).astype(bf16), x) for m, x in zip(amat, vh)]
        rkv = [_dot(jnp.where(incl, m[c:, :c], 0.0).astype(bf16), x) for m, x in zip(amat, vh)]
        tx = [_dot(tm.astype(bf16), jnp.concatenate([at[ch][:, hsl[h]], x.astype(bf16)], axis=1))
              for tm, x, (ch, h) in zip(tmat, akv, pairs)]
        for i, p in enumerate(pairs):
            res["vh", p], res["rkv", p], res["tx", p] = vh[i], rkv[i], tx[i]
            res["a_rb", p] = jnp.where(incl, amat[i][c:, c:], 0.0).astype(bf16)
        return res

    def phase2(chs, res, state):
        for ch in chs:
            heads = range(N_HEADS)
            sb = [x.astype(bf16) for x in state]
            u = [_dot_nt(res["tx", (ch, h)][:, :HEAD_DIM].astype(bf16), sb[h]) + res["tx", (ch, h)][:, HEAD_DIM:]
                 for h in heads]
            ub = [x.astype(bf16) for x in u]
            y_chunks.append(jnp.concatenate(
                [_dot_nt(res["rt", ch][:, hsl[h]], sb[h]) + res["rkv", (ch, h)] + _dot(res["a_rb", (ch, h)], ub[h])
                 for h in heads], axis=1))
            state = [state[h] * res["e_last", ch][:, hsl[h]]
                     + _dot_tn(jnp.concatenate([res["vh", (ch, h)], ub[h]], axis=0),
                               jnp.concatenate([res["kh", ch][:, hsl[h]], res["bh", ch][:, hsl[h]]], axis=0))
                     for h in heads]
        return state

    groups = [list(range(g, min(g + PIPE_GROUP, n_ch))) for g in range(0, n_ch, PIPE_GROUP)]
    state = [s_scr[h] for h in range(N_HEADS)]
    y_chunks = []
    pending = None
    for grp in groups:
        res = phase1(grp)
        if pending is not None:
            state = phase2(pending[0], pending[1], state)
        pending = (grp, res)
    state = phase2(pending[0], pending[1], state)
    for h in range(N_HEADS):
        s_scr[h] = state[h]
    y = jnp.concatenate(y_chunks, axis=0)
    y_ref[0] = _rwkv_post(y, r, k2, v, g, rk_ref[...], lnw_ref[...], lnb_ref[...], seg)
    s_out_ref[0] = s_scr[...]


def _rwkv_prompt(pa, prm, tt):
    b, seq, _ = pa.shape
    vec = lambda a: pl.BlockSpec(a.shape, lambda bi, t: (0, 0))
    return pl.pallas_call(
        functools.partial(_rwkv_body, tt=tt),
        grid=(b, seq // tt),
        in_specs=[pl.BlockSpec((1, tt, A_COLS), lambda bi, t: (bi, t, 0))] + [vec(a) for a in prm],
        out_specs=[pl.BlockSpec((1, tt, GROUP_WIDTH), lambda bi, t: (bi, t, 0)),
                   pl.BlockSpec((1, N_HEADS, HEAD_DIM, HEAD_DIM), lambda bi, t: (bi, 0, 0, 0))],
        out_shape=[jax.ShapeDtypeStruct((b, seq, GROUP_WIDTH), f32),
                   jax.ShapeDtypeStruct((b, N_HEADS, HEAD_DIM, HEAD_DIM), f32)],
        scratch_shapes=[pltpu.VMEM((N_HEADS, HEAD_DIM, HEAD_DIM), f32), pltpu.VMEM((8, A_COLS), f32)],
        compiler_params=_cparams("arbitrary", "arbitrary"),
        name="rwkv7_prompt",
    )(pa, *prm)


def _xpos(x, cos, sin_signed):
    n = x.shape[-1]
    even = (_iota2(x.shape, 1) % 2) == 0
    partner = jnp.where(even, pltpu.roll(x, n - 1, axis=1), pltpu.roll(x, 1, axis=1))
    return x * cos + partner * sin_signed


def _ret_body(p_ref, cos_ref, sin_ref, dmask_ref, eg_ref, erem_ref, elast_ref, y_ref, s_out_ref, s_scr, *, tt):
    c = CHUNK
    t = pl.program_id(1)

    @pl.when(t == 0)
    def _():
        s_scr[...] = jnp.zeros_like(s_scr)

    seg = _head_seg()
    p = p_ref[0]
    q = _xpos(p[:, 0:GROUP_WIDTH], cos_ref[...], sin_ref[...])
    k = _xpos(p[:, GROUP_WIDTH:2 * GROUP_WIDTH], cos_ref[...], sin_ref[...]) * HEAD_DIM ** -0.5
    v = p[:, 2 * GROUP_WIDTH:3 * GROUP_WIDTH]
    gt = p[:, 3 * GROUP_WIDTH:]
    e_last = elast_ref[...]
    n_ch = tt // c
    hsl = [slice(h * HEAD_DIM, (h + 1) * HEAD_DIM) for h in range(N_HEADS)]
    pairs = [(ch, h) for ch in range(n_ch) for h in range(N_HEADS)]
    rows = lambda x, ch, h: x[ch * c:(ch + 1) * c, hsl[h]]

    qg = [q[ch * c:(ch + 1) * c] * eg_ref[...] for ch in range(n_ch)]
    kr = [k[ch * c:(ch + 1) * c] * erem_ref[...] for ch in range(n_ch)]
    qk = [_dot_nt(rows(q, ch, h), rows(k, ch, h)) * dmask_ref[h] for ch, h in pairs]
    intra = [_dot(x, rows(v, ch, h)) for x, (ch, h) in zip(qk, pairs)]
    kv = [_dot_tn(kr[ch][:, hsl[h]], rows(v, ch, h)) for ch, h in pairs]

    state = [s_scr[h] for h in range(N_HEADS)]
    o_chunks = []
    for ch in range(n_ch):
        o_chunks.append(jnp.concatenate(
            [intra[ch * N_HEADS + h] + _dot(qg[ch][:, hsl[h]], state[h]) for h in range(N_HEADS)], axis=1))
        state = [state[h] * e_last[:, hsl[h]] + kv[ch * N_HEADS + h] for h in range(N_HEADS)]
    for h in range(N_HEADS):
        s_scr[h] = state[h]
    o = jnp.concatenate(o_chunks, axis=0)
    o = o * lax.rsqrt(_segsum(o * o, seg) * (1.0 / HEAD_DIM) + NORM_EPS)
    y_ref[0] = _silu(gt) * o
    s_out_ref[0] = s_scr[...]


def _ret_tables(seq, pos0):
    inv = 1.0 / (RET_THETA ** jnp.linspace(0.0, 1.0, HEAD_DIM // 2, dtype=f32))
    ang = (pos0 + jnp.arange(seq)).astype(f32)[:, None] * inv[None, :]
    cos = jnp.tile(jnp.repeat(jnp.cos(ang), 2, axis=1), (1, N_HEADS))
    sgn = jnp.tile(jnp.array([-1.0, 1.0], f32), HEAD_DIM // 2)
    sin = jnp.tile(jnp.repeat(jnp.sin(ang), 2, axis=1) * sgn[None, :], (1, N_HEADS))
    lg = np.log(1.0 - 2.0 ** (-5.0 - np.arange(N_HEADS, dtype=np.float64)))
    i = np.arange(CHUNK)
    diff = i[:, None] - i[None, :]
    dmask = np.where(diff[None] >= 0, np.exp(lg[:, None, None] * np.maximum(diff, 0)[None]), 0.0)
    lane = np.repeat(lg, HEAD_DIM)[None, :]
    e_g = np.exp((i[:, None] + 1) * lane)
    e_rem = np.exp((CHUNK - 1 - i[:, None]) * lane)
    e_last = np.exp(CHUNK * lane)
    e_one = np.exp(lane)
    return cos, sin, tuple(jnp.asarray(a, f32) for a in (dmask, e_g, e_rem, e_last, e_one))


def _ret_prompt(pb, cos, sin, tabs, tt):
    b, seq, _ = pb.shape
    dmask, e_g, e_rem, e_last, _ = tabs
    full = lambda a: pl.BlockSpec(a.shape, lambda bi, t: (0,) * a.ndim)
    return pl.pallas_call(
        functools.partial(_ret_body, tt=tt),
        grid=(b, seq // tt),
        in_specs=[pl.BlockSpec((1, tt, 4 * GROUP_WIDTH), lambda bi, t: (bi, t, 0)),
                  pl.BlockSpec((tt, GROUP_WIDTH), lambda bi, t: (t, 0)),
                  pl.BlockSpec((tt, GROUP_WIDTH), lambda bi, t: (t, 0)),
                  full(dmask), full(e_g), full(e_rem), full(e_last)],
        out_specs=[pl.BlockSpec((1, tt, GROUP_WIDTH), lambda bi, t: (bi, t, 0)),
                   pl.BlockSpec((1, N_HEADS, HEAD_DIM, HEAD_DIM), lambda bi, t: (bi, 0, 0, 0))],
        out_shape=[jax.ShapeDtypeStruct((b, seq, GROUP_WIDTH), f32),
                   jax.ShapeDtypeStruct((b, N_HEADS, HEAD_DIM, HEAD_DIM), f32)],
        scratch_shapes=[pltpu.VMEM((N_HEADS, HEAD_DIM, HEAD_DIM), f32)],
        compiler_params=_cparams("arbitrary", "arbitrary"),
        name="retention_prompt",
    )(pb, cos, sin, dmask, e_g, e_rem, e_last)


def _gdn_scalars(pab, alog_p, dtb_p, sel_a, sel_b):
    logd = -jnp.exp(alog_p) * _softplus(pab + dtb_p)
    beta = jax.nn.sigmoid(pab)
    return _dot_exact_rhs(logd, sel_a), _dot_exact_rhs(beta, sel_b)


def _gdn_qkv(conv, seg):
    cs = _silu(conv)
    q = cs[:, 0:GROUP_WIDTH]
    k = cs[:, GROUP_WIDTH:2 * GROUP_WIDTH]
    v = cs[:, 2 * GROUP_WIDTH:]
    q = q * lax.rsqrt(_segsum(q * q, seg) + 1e-6) * HEAD_DIM ** -0.5
    k = k * lax.rsqrt(_segsum(k * k, seg) + 1e-6)
    return q, k, v


def _gdn_body(pc_ref, pab_ref, cw_ref, alog_ref, dtb_ref, sela_ref, selb_ref, nw_ref,
              y_ref, s_out_ref, s_scr, ext_scr, *, tt):
    c = CHUNK
    t = pl.program_id(1)

    @pl.when(t == 0)
    def _():
        s_scr[...] = jnp.zeros_like(s_scr)
        ext_scr[pl.ds(0, 8), :] = jnp.zeros((8, C_QKV), f32)

    seg = _head_seg()
    pc = pc_ref[0]
    gate = pc[:, C_QKV:]
    ext_scr[pl.ds(8, tt), :] = pc[:, :C_QKV]
    cw = cw_ref[...]
    conv = sum(ext_scr[pl.ds(8 - (GDN_CONV - 1) + j, tt), :] * cw[j:j + 1, :] for j in range(GDN_CONV))
    ext_scr[pl.ds(0, 8), :] = pc[tt - 8:tt, :C_QKV]
    q, k, v = _gdn_qkv(conv, seg)
    logd, beta = _gdn_scalars(pab_ref[0], alog_ref[...], dtb_ref[...], sela_ref[...], selb_ref[...])
    kb = k * beta
    vb = v * beta

    ri, ci = _iota2((c, c), 0), _iota2((c, c), 1)
    incl, strict = ri >= ci, ri > ci
    lmat = incl.astype(f32)
    ones = jnp.ones((c, c), f32)
    eye4 = (_iota2((c, GROUP_WIDTH), 0) == _iota2((c, GROUP_WIDTH), 1) % HEAD_DIM).astype(f32)
    incl4 = _iota2((c, GROUP_WIDTH), 0) >= _iota2((c, GROUP_WIDTH), 1) % HEAD_DIM

    n_ch = tt // c
    hsl = [slice(h * HEAD_DIM, (h + 1) * HEAD_DIM) for h in range(N_HEADS)]
    pairs = [(ch, h) for ch in range(n_ch) for h in range(N_HEADS)]

    dmask, qg, kbg, kr, e_last = [], [], [], [], []
    for ch in range(n_ch):
        sl = slice(ch * c, (ch + 1) * c)
        gcum = _dot_exact_lhs(lmat, logd[sl])
        gcum_t = _dot_exact_lhs(ones, gcum * eye4)
        g_last = gcum[c - 1:c, :]
        e_g = jnp.exp(gcum)
        e_last.append(jnp.exp(g_last))
        dmask.append(jnp.exp(jnp.where(incl4, gcum - gcum_t, -jnp.inf)))
        qg.append(q[sl] * e_g)
        kbg.append(kb[sl] * e_g)
        kr.append(k[sl] * jnp.exp(g_last - gcum))
    rows = lambda x, ch, h: x[ch * c:(ch + 1) * c, hsl[h]]
    kq = [_dot_nt(jnp.concatenate([rows(kb, ch, h), rows(q, ch, h)], axis=0), rows(k, ch, h)) for ch, h in pairs]
    tmat = _unit_lower_inverse([jnp.where(strict, -x[:c] * dmask[ch][:, hsl[h]], 0.0) for x, (ch, h) in zip(kq, pairs)], c)
    aqk = [x[c:] * dmask[ch][:, hsl[h]] for x, (ch, h) in zip(kq, pairs)]
    uw = [_dot(tm, jnp.concatenate([rows(vb, ch, h), kbg[ch][:, hsl[h]]], axis=1)) for tm, (ch, h) in zip(tmat, pairs)]

    state = [s_scr[h] for h in range(N_HEADS)]
    o_chunks = []
    for ch in range(n_ch):
        idx = [ch * N_HEADS + h for h in range(N_HEADS)]
        vnew = [uw[i][:, :HEAD_DIM] - _dot(uw[i][:, HEAD_DIM:], state[h]) for h, i in enumerate(idx)]
        o_chunks.append(jnp.concatenate(
            [_dot(qg[ch][:, hsl[h]], state[h]) + _dot(aqk[i], vnew[h]) for h, i in enumerate(idx)], axis=1))
        state = [state[h] * e_last[ch][:, hsl[h]] + _dot_tn(kr[ch][:, hsl[h]], vnew[h]) for h in range(N_HEADS)]
    for h in range(N_HEADS):
        s_scr[h] = state[h]
    o = jnp.concatenate(o_chunks, axis=0)
    o = o * lax.rsqrt(_segsum(o * o, seg) * (1.0 / HEAD_DIM) + NORM_EPS) * nw_ref[...] * _silu(gate)
    y_ref[0] = o
    s_out_ref[0] = s_scr[...]


def _gdn_prompt(pc, pab, prm, tt):
    b, seq, _ = pc.shape
    vec = lambda a: pl.BlockSpec(a.shape, lambda bi, t: (0, 0))
    return pl.pallas_call(
        functools.partial(_gdn_body, tt=tt),
        grid=(b, seq // tt),
        in_specs=[pl.BlockSpec((1, tt, 4 * GROUP_WIDTH), lambda bi, t: (bi, t, 0)),
                  pl.BlockSpec((1, tt, 128), lambda bi, t: (bi, t, 0))] + [vec(a) for a in prm],
        out_specs=[pl.BlockSpec((1, tt, GROUP_WIDTH), lambda bi, t: (bi, t, 0)),
                   pl.BlockSpec((1, N_HEADS, HEAD_DIM, HEAD_DIM), lambda bi, t: (bi, 0, 0, 0))],
        out_shape=[jax.ShapeDtypeStruct((b, seq, GROUP_WIDTH), f32),
                   jax.ShapeDtypeStruct((b, N_HEADS, HEAD_DIM, HEAD_DIM), f32)],
        scratch_shapes=[pltpu.VMEM((N_HEADS, HEAD_DIM, HEAD_DIM), f32), pltpu.VMEM((tt + 8, C_QKV), f32)],
        compiler_params=_cparams("arbitrary", "arbitrary"),
        name="gdn_prompt",
    )(pc, pab, *prm)


def _top_blocks(gate, own, nb):
    blk = _iota2(gate.shape, 0).astype(f32)
    g = jnp.where(blk < own, gate, -jnp.inf)
    picks = []
    for _ in range(MOBA_TOPK):
        m = jnp.max(g, axis=0, keepdims=True)
        idx = jnp.min(jnp.where(g == m, blk, float(nb)), axis=0, keepdims=True)
        picks.append(jnp.where(m > -jnp.inf, idx, -1.0))
        g = jnp.where(blk == idx, -jnp.inf, g)
    return picks


def _moba_body(q_ref, k_ref, v_ref, o_ref, k_scr, vt_scr, mean_scr, *, nb):
    i = pl.program_id(1)
    blk_w = MOBA_BLOCK
    hsl = [slice(h * HEAD_DIM, (h + 1) * HEAD_DIM) for h in range(N_HEADS)]
    own_start = pl.multiple_of(i * blk_w, blk_w)

    @pl.when(i == 0)
    def _():
        mean_scr[...] = jnp.zeros_like(mean_scr)

    k_tile = k_ref[0]
    k_scr[pl.ds(own_start, blk_w), :] = k_tile.astype(bf16)
    mean_row = jnp.sum(k_tile, axis=0, keepdims=True) * (1.0 / blk_w)
    means = jnp.where(_iota2(mean_scr.shape, 0) == i, mean_row, mean_scr[...])
    mean_scr[...] = means
    v_t = v_ref[0].T.astype(bf16)
    for h in range(N_HEADS):
        vt_scr[h, pl.ds(0, HEAD_DIM), pl.ds(own_start, blk_w)] = v_t[hsl[h]]
        vt_scr[h, pl.ds(HEAD_DIM, SUM_ROWS), pl.ds(own_start, blk_w)] = jnp.ones((SUM_ROWS, blk_w), bf16)

    q_t = (q_ref[0] * HEAD_DIM ** -0.5).T
    qt = [q_t[hsl[h]] for h in range(N_HEADS)]
    qtb = [(x * LOG2E).astype(bf16) for x in qt]
    picks = [_top_blocks(_dot(means[:, hsl[h]], qt[h], precision=HI), i.astype(f32), nb) for h in range(N_HEADS)]

    k_own = k_scr[pl.ds(own_start, blk_w), :]
    causal = _iota2((blk_w, blk_w), 0) <= _iota2((blk_w, blk_w), 1)
    s_own = [jnp.where(causal, _dot(k_own[:, hsl[h]], qtb[h]), NEG) for h in range(N_HEADS)]
    m_own = [jnp.max(x, axis=0, keepdims=True) for x in s_own]
    p_own = [jnp.exp2(x - m).astype(bf16) for x, m in zip(s_own, m_own)]
    carry = []
    for h in range(N_HEADS):
        carry += [m_own[h], _dot(vt_scr[h, :, pl.ds(own_start, blk_w)], p_own[h])]

    def step(jj, carry):
        js = [2 * jj, 2 * jj + 1]
        starts = [pl.multiple_of(j * blk_w, blk_w) for j in js]
        kj = [k_scr[pl.ds(st, blk_w), :] for st in starts]
        units = [(h, e) for h in range(N_HEADS) for e in range(2)]
        s = [_dot(kj[e][:, hsl[h]], qtb[h]) for h, e in units]
        chosen = [functools.reduce(jnp.logical_or, [pk == js[e].astype(f32) for pk in picks[h]]) for h, e in units]
        m_blk = [jnp.where(c, jnp.max(x, axis=0, keepdims=True), NEG) for c, x in zip(chosen, s)]
        m_new = [jnp.maximum(carry[2 * h], jnp.maximum(m_blk[2 * h], m_blk[2 * h + 1])) for h in range(N_HEADS)]
        shift = [jnp.where(c, m_new[h], -NEG) for c, (h, e) in zip(chosen, units)]
        p = [jnp.exp2(x - sh).astype(bf16) for x, sh in zip(s, shift)]
        pv = [_dot(vt_scr[h, :, pl.ds(starts[e], blk_w)], x) for x, (h, e) in zip(p, units)]
        out = []
        for h in range(N_HEADS):
            out += [m_new[h], jnp.exp2(carry[2 * h] - m_new[h]) * carry[2 * h + 1] + (pv[2 * h] + pv[2 * h + 1])]
        return tuple(out)

    carry = lax.fori_loop(0, (i + 1) // 2, step, tuple(carry))
    o_t = jnp.concatenate([carry[2 * h + 1][:HEAD_DIM] / carry[2 * h + 1][HEAD_DIM:HEAD_DIM + 1]
                           for h in range(N_HEADS)], axis=0)
    o_ref[0] = o_t.T


def _moba_prompt(q, k, v):
    b, seq, _ = k.shape
    nb = seq // MOBA_BLOCK
    assert seq % MOBA_BLOCK == 0
    tile = pl.BlockSpec((1, MOBA_BLOCK, GROUP_WIDTH), lambda bi, i: (bi, i, 0))
    return pl.pallas_call(
        functools.partial(_moba_body, nb=nb),
        grid=(b, nb),
        in_specs=[tile, tile, tile],
        out_specs=tile,
        out_shape=jax.ShapeDtypeStruct((b, seq, GROUP_WIDTH), f32),
        scratch_shapes=[pltpu.VMEM((seq, GROUP_WIDTH), bf16),
                        pltpu.VMEM((N_HEADS, HEAD_DIM + SUM_ROWS, seq), bf16),
                        pltpu.VMEM((nb, GROUP_WIDTH), f32)],
        compiler_params=_cparams("arbitrary", "arbitrary"),
        name="moba_prompt",
    )(q, k, v)


def _row_to_col(row, eye):
    return jnp.sum(eye * row, axis=1, keepdims=True)


def _col_to_row(col, eye):
    return jnp.sum(eye * col, axis=0, keepdims=True)


def _sample_mix_body(pa_ref, shift_ref, pb_ref, pc_ref, pab_ref, conv_ref, srw_ref, srt_ref, sgd_ref,
                     mu_ref, w0_ref, w2_ref, a0_ref, a2_ref, g2_ref, kk_ref, ka_ref, rk_ref, lnw_ref, lnb_ref,
                     cos_ref, sin_ref, eone_ref, cw_ref, alog_ref, dtb_ref, sela_ref, selb_ref, nw_ref,
                     ya_ref, yb_ref, yc_ref, srw_out, srt_out, sgd_out):
    seg = _head_seg()
    eye = _eye(HEAD_DIM)
    hsl = [slice(h * HEAD_DIM, (h + 1) * HEAD_DIM) for h in range(N_HEADS)]

    r, logw, k2, v, kk, a, g = _rwkv_pre(pa_ref[0], shift_ref[0], mu_ref[...], w0_ref[...], w2_ref[...], a0_ref[...],
                                         a2_ref[...], g2_ref[...], kk_ref[...], ka_ref[...], seg)
    w = jnp.exp(logw)
    kka = kk * a
    ys = []
    for h, hs in enumerate(hsl):
        s_old = srw_ref[0, h]
        sa = jnp.sum(s_old * (-kk[:, hs]), axis=1, keepdims=True)
        s_new = s_old * w[:, hs] + sa * kka[:, hs] + _row_to_col(v[:, hs], eye) * k2[:, hs]
        srw_out[0, h] = s_new
        ys.append(_col_to_row(jnp.sum(s_new * r[:, hs], axis=1, keepdims=True), eye))
    ya_ref[0] = _rwkv_post(jnp.concatenate(ys, axis=1), r, k2, v, g, rk_ref[...], lnw_ref[...], lnb_ref[...], seg)

    pb = pb_ref[0]
    q = _xpos(pb[:, 0:GROUP_WIDTH], cos_ref[...], sin_ref[...])
    k = _xpos(pb[:, GROUP_WIDTH:2 * GROUP_WIDTH], cos_ref[...], sin_ref[...]) * HEAD_DIM ** -0.5
    v = pb[:, 2 * GROUP_WIDTH:3 * GROUP_WIDTH]
    e_one = eone_ref[...]
    os_ = []
    for h, hs in enumerate(hsl):
        s_new = srt_ref[0, h] * e_one[:, hs] + _row_to_col(k[:, hs], eye) * v[:, hs]
        srt_out[0, h] = s_new
        os_.append(jnp.sum(_row_to_col(q[:, hs], eye) * s_new, axis=0, keepdims=True))
    o = jnp.concatenate(os_, axis=1)
    o = o * lax.rsqrt(_segsum(o * o, seg) * (1.0 / HEAD_DIM) + NORM_EPS)
    yb_ref[0] = _silu(pb[:, 3 * GROUP_WIDTH:]) * o

    pc = pc_ref[0]
    cw = cw_ref[...]
    prev = conv_ref[0]
    conv = pc[:, :C_QKV] * cw[GDN_CONV - 1:GDN_CONV, :]
    for j in range(GDN_CONV - 1):
        conv = conv + prev[j:j + 1, :] * cw[j:j + 1, :]
    q, k, v = _gdn_qkv(conv, seg)
    logd, beta = _gdn_scalars(pab_ref[0], alog_ref[...], dtb_ref[...], sela_ref[...], selb_ref[...])
    alpha = jnp.exp(logd)
    os_ = []
    for h, hs in enumerate(hsl):
        s_old = sgd_ref[0, h]
        k_col = _row_to_col(k[:, hs], eye)
        ks = jnp.sum(k_col * s_old, axis=0, keepdims=True)
        s_new = alpha[:, hs] * s_old + k_col * (beta[:, hs] * (v[:, hs] - alpha[:, hs] * ks))
        sgd_out[0, h] = s_new
        os_.append(jnp.sum(_row_to_col(q[:, hs], eye) * s_new, axis=0, keepdims=True))
    o = jnp.concatenate(os_, axis=1)
    yc_ref[0] = o * lax.rsqrt(_segsum(o * o, seg) * (1.0 / HEAD_DIM) + NORM_EPS) * nw_ref[...] * _silu(pc[:, C_QKV:])


def _sample_mixers(pa, shift, pb, pc, pab, conv_prev, s_rwkv, s_ret, s_gdn, rwkv_prm, ret_prm, gdn_prm):
    nseq = pa.shape[0]
    row = lambda a: pl.BlockSpec((1, 1, a.shape[-1]), lambda b: (b, 0, 0))
    state = pl.BlockSpec((1, N_HEADS, HEAD_DIM, HEAD_DIM), lambda b: (b, 0, 0, 0))
    full = lambda a: pl.BlockSpec(a.shape, lambda b: (0,) * a.ndim)
    r3 = lambda a: a.reshape(nseq, 1, a.shape[-1])
    prm = tuple(rwkv_prm) + tuple(ret_prm) + tuple(gdn_prm)
    outs = pl.pallas_call(
        _sample_mix_body,
        grid=(nseq,),
        in_specs=[row(pa), row(shift), row(pb), row(pc), row(pab),
                  pl.BlockSpec((1, GDN_CONV - 1, C_QKV), lambda b: (b, 0, 0)), state, state, state]
                 + [full(a) for a in prm],
        out_specs=[row(pa[..., :GROUP_WIDTH])] * 3 + [state] * 3,
        out_shape=[jax.ShapeDtypeStruct((nseq, 1, GROUP_WIDTH), f32)] * 3
                  + [jax.ShapeDtypeStruct((nseq, N_HEADS, HEAD_DIM, HEAD_DIM), f32)] * 3,
        compiler_params=_cparams("arbitrary"),
        name="sample_mixers",
    )(r3(pa), r3(shift), r3(pb), r3(pc), r3(pab), conv_prev, s_rwkv, s_ret, s_gdn, *prm)
    return outs


def _moba_sample_body(pt_ref, q_ref, qb_ref, kn_ref, vn_ref, *refs, nb, bps):
    del pt_ref
    npg = bps * PAGES_PER_BLOCK
    k_refs, v_refs = refs[:npg], refs[npg:2 * npg]
    o_ref, gate_scr, m_scr, l_scr, acc_scr = refs[2 * npg:]
    j = pl.program_id(1)
    hmask = (_iota2((N_HEADS, GROUP_WIDTH), 1) // HEAD_DIM == _iota2((N_HEADS, GROUP_WIDTH), 0)).astype(f32)
    qb = qb_ref[0]
    qb = jnp.concatenate([qb] * (MOBA_BLOCK // 128), axis=1)
    block_t = lambda page_refs: jnp.concatenate(
        [r[...].reshape(GROUP_WIDTH, PAGE_SIZE) for r in page_refs], axis=1)
    blocks = [slice(e * PAGES_PER_BLOCK, (e + 1) * PAGES_PER_BLOCK) for e in range(bps)]
    kt = [block_t(k_refs[pg]) for pg in blocks]
    s = [jnp.sum((x * qb).reshape(N_HEADS, HEAD_DIM, MOBA_BLOCK), axis=1) for x in kt]
    gate = [jnp.sum(x, axis=1, keepdims=True) * (1.0 / MOBA_BLOCK) for x in s]
    m = [jnp.max(x, axis=1, keepdims=True) for x in s]
    p = [jnp.exp(x - mx) for x, mx in zip(s, m)]
    pv = [_dot_nt(x, block_t(v_refs[pg])) for x, pg in zip(p, blocks)]
    for e in range(bps):
        slot = j * bps + e
        gate_scr[slot] = jnp.broadcast_to(gate[e], (N_HEADS, 128))
        m_scr[slot] = jnp.broadcast_to(m[e], (N_HEADS, 128))
        l_scr[slot] = jnp.broadcast_to(jnp.sum(p[e], axis=1, keepdims=True), (N_HEADS, 128))
        acc_scr[slot] = pv[e]

    @pl.when(j == nb // bps - 1)
    def _():
        g = gate_scr[...]
        blk = _iota2(g.shape, 0).astype(f32)
        sel = jnp.zeros(g.shape, jnp.bool_)
        for _ in range(min(MOBA_TOPK, nb)):
            gm = jnp.max(g, axis=0, keepdims=True)
            idx = jnp.min(jnp.where(g == gm, blk, float(nb)), axis=0, keepdims=True)
            pick = blk == idx
            sel = jnp.logical_or(sel, pick)
            g = jnp.where(pick, -jnp.inf, g)
        q_own = hmask * (q_ref[0] * HEAD_DIM ** -0.5)
        s_own = jnp.broadcast_to(jnp.sum(q_own * kn_ref[0], axis=1, keepdims=True), (N_HEADS, 128))
        mb = jnp.where(sel, m_scr[...], NEG)
        m_all = jnp.maximum(jnp.max(mb, axis=0), s_own)
        wgt = jnp.where(sel, jnp.exp(mb - m_all[None]), 0.0)
        w_own = jnp.exp(s_own - m_all)
        denom = jnp.sum(wgt * l_scr[...], axis=0) + w_own
        wide = lambda a: jnp.concatenate([a, a], axis=-1)
        num = jnp.sum(wide(wgt) * acc_scr[...], axis=0) + wide(w_own) * vn_ref[0]
        o_ref[0] = jnp.sum(hmask * (num / wide(denom)), axis=0, keepdims=True)


def _moba_sample(q, k_new, v_new, cache_kt, cache_vt, page_table, layer, bps):
    nseq, n_pages = page_table.shape
    nb = n_pages // PAGES_PER_BLOCK
    assert n_pages % PAGES_PER_BLOCK == 0 and nb % bps == 0
    npg = bps * PAGES_PER_BLOCK
    row = pl.BlockSpec((1, 1, GROUP_WIDTH), lambda b, j, pt: (b, 0, 0))
    page = lambda e: pl.BlockSpec((None, None, N_HEADS, HEAD_DIM, PAGE_SIZE),
                                  lambda b, j, pt: (layer, pt[b, npg * j + e], 0, 0, 0))
    pages = [page(e) for e in range(npg)]
    r3 = lambda a: a.reshape(nseq, 1, GROUP_WIDTH)
    q_lanes = jnp.broadcast_to((q * HEAD_DIM ** -0.5)[:, :, None], (nseq, GROUP_WIDTH, 128))
    return pl.pallas_call(
        functools.partial(_moba_sample_body, nb=nb, bps=bps),
        grid_spec=pltpu.PrefetchScalarGridSpec(
            num_scalar_prefetch=1,
            grid=(nseq, nb // bps),
            in_specs=[row, pl.BlockSpec((1, GROUP_WIDTH, 128), lambda b, j, pt: (b, 0, 0)), row, row] + pages + pages,
            out_specs=row,
            scratch_shapes=[pltpu.VMEM((nb, N_HEADS, 128), f32)] * 3 + [pltpu.VMEM((nb, N_HEADS, GROUP_WIDTH), f32)]),
        out_shape=jax.ShapeDtypeStruct((nseq, 1, GROUP_WIDTH), f32),
        compiler_params=_cparams("arbitrary", "arbitrary"),
        name="moba_sample",
    )(page_table, r3(q), q_lanes, r3(k_new), r3(v_new), *([cache_kt] * npg), *([cache_vt] * npg)
      ).reshape(nseq, GROUP_WIDTH)


def _pad_rows(w, start, total):
    return jnp.zeros((total, w.shape[1]), w.dtype).at[start:start + w.shape[0]].set(w)


def _mixer_params(l, rwkv_mu, rwkv_w0, rwkv_w2, rwkv_a0, rwkv_a2, rwkv_g2, rwkv_k_k, rwkv_k_a, rwkv_r_k, rwkv_ln_w,
                  rwkv_ln_b, gdn_conv_w, gdn_a_log, gdn_dt_bias, gdn_norm_w):
    row = lambda a: a.reshape(1, -1)
    rwkv = (row(rwkv_mu[l]), row(rwkv_w0[l]), _pad_rows(rwkv_w2[l], 0, LORA_W), row(rwkv_a0[l]),
            _pad_rows(rwkv_a2[l], RWKV_W_LORA, LORA_W), _pad_rows(rwkv_g2[l], RWKV_W_LORA + RWKV_A_LORA, LORA_W),
            row(rwkv_k_k[l]), row(rwkv_k_a[l]), row(rwkv_r_k[l]), row(rwkv_ln_w[l]), row(rwkv_ln_b[l]))
    lane = np.arange(128)[:, None]
    col_head = (np.arange(GROUP_WIDTH) // HEAD_DIM)[None, :]
    sel_a = jnp.asarray(lane == col_head, f32)
    sel_b = jnp.asarray(lane == col_head + N_HEADS, f32)
    pad128 = lambda a: jnp.zeros((1, 128), f32).at[0, :N_HEADS].set(a)
    gdn = (gdn_conv_w[l], pad128(gdn_a_log[l]), pad128(gdn_dt_bias[l]), sel_a, sel_b,
           jnp.tile(gdn_norm_w[l], N_HEADS).reshape(1, GROUP_WIDTH))
    return rwkv, gdn


def kernel(x_prompt, x_sample, c_prompt, c_sample, cache_k, cache_v, page_table, state_rwkv, state_rwkv_shift,
           state_ret, state_gdn, state_gdn_conv, w_mod, b_mod, norm_pre, norm_post, ffn_w_gate, ffn_w_up, ffn_w_down,
           w_in, w_out, rwkv_mu, rwkv_w0, rwkv_w2, rwkv_a0, rwkv_a2, rwkv_g2, rwkv_k_k, rwkv_k_a, rwkv_r_k, rwkv_ln_w,
           rwkv_ln_b, gdn_conv_w, gdn_a_log, gdn_dt_bias, gdn_norm_w):
    depth = w_in.shape[0]
    bp, seq, d = x_prompt.shape
    bs = x_sample.shape[0]
    assert x_sample.shape[1] == 1
    past_len = page_table.shape[1] * PAGE_SIZE
    assert past_len % MOBA_BLOCK == 0
    cache_kt = jnp.transpose(cache_k, (0, 1, 3, 4, 2))
    cache_vt = jnp.transpose(cache_v, (0, 1, 3, 4, 2))

    n_c = bp + bs
    c_all = jnp.concatenate([c_prompt, c_sample, jnp.zeros((-n_c % 8, d), f32)], axis=0)
    mods = _modulation(c_all, w_mod, b_mod)

    wg, wu, wdn = ffn_w_gate.astype(bf16), ffn_w_up.astype(bf16), ffn_w_down.astype(bf16)
    w_in_b, w_out_b = w_in.astype(bf16), w_out.astype(bf16)
    a0, b0 = A_COLS, A_COLS + 4 * GROUP_WIDTH
    c0 = b0 + C_QKV + GROUP_WIDTH
    d0 = c0 + 2 * N_HEADS
    col_groups = [(0, a0), (a0, b0), (b0, c0), (d0, d0 + GROUP_WIDTH), (d0 + GROUP_WIDTH, d0 + 2 * GROUP_WIDTH),
                  (d0 + 2 * GROUP_WIDTH, d0 + 3 * GROUP_WIDTH)]

    ws = [w_in_b[:, :, lo:hi] for lo, hi in col_groups]
    ws.append(jnp.pad(w_in_b[:, :, c0:d0], ((0, 0), (0, 0), (0, 128 - 2 * N_HEADS))))
    cos_p, sin_p, ret_tabs = _ret_tables(seq, 0)
    cos_s, sin_s, _ = _ret_tables(1, past_len)
    tm = 512
    tm_ffn = 1024
    tt = 256

    xp, xs = x_prompt, x_sample.reshape(1, bs, d)
    news_p, news_s = [], []
    for l in range(depth):
        mp = [m.reshape(bp, 1, d) for m in jnp.split(mods[l, :bp], 9, axis=-1)]
        ms = [m.reshape(1, bs, d) for m in jnp.split(mods[l, bp:n_c], 9, axis=-1)]
        npre = [norm_pre[l, i].reshape(1, d) for i in range(3)]
        npost = [norm_post[l, i].reshape(1, d) for i in range(3)]
        rwkv_prm, gdn_prm = _mixer_params(l, rwkv_mu, rwkv_w0, rwkv_w2, rwkv_a0, rwkv_a2, rwkv_g2, rwkv_k_k, rwkv_k_a,
                                          rwkv_r_k, rwkv_ln_w, rwkv_ln_b, gdn_conv_w, gdn_a_log, gdn_dt_bias, gdn_norm_w)

        xp = _ffn(xp, mp[0], mp[1], mp[2], npre[0], npost[0], wg, wu, wdn, l, 0, tm_ffn, FFN_TF)
        pa, pb, pc, qd, kd, vd, pab = _inproj(xp, mp[3], mp[4], npre[1], ws, l, tm)
        ya, rw_new = _rwkv_prompt(pa, rwkv_prm, 2 * tt)
        yb, rt_new = _ret_prompt(pb, cos_p, sin_p, ret_tabs, tt)
        yc, gd_new = _gdn_prompt(pc, pab, gdn_prm, tt)
        yd = _moba_prompt(qd, kd, vd)
        xp = _ffn(xp, mp[6], mp[7], mp[8], npre[2], npost[2], wg, wu, wdn, l, 1, tm_ffn, FFN_TF,
                  mix=((ya, yb, yc, yd), mp[5], npost[1], w_out_b))
        hd = lambda a: a.reshape(a.shape[0], a.shape[1], N_HEADS, HEAD_DIM)
        news_p.append((hd(kd), hd(vd), rw_new, pa[:, -1], rt_new, gd_new, pc[:, seq - (GDN_CONV - 1):, :C_QKV]))

        xs = _ffn(xs, ms[0], ms[1], ms[2], npre[0], npost[0], wg, wu, wdn, l, 0, bs, FFN_TF)
        pa, pb, pc, qd, kd, vd, pab = (a[0] for a in _inproj(xs, ms[3], ms[4], npre[1], ws, l, bs))
        ya, yb, yc, rw_new, rt_new, gd_new = _sample_mixers(
            pa, state_rwkv_shift[l], pb, pc, pab, state_gdn_conv[l], state_rwkv[l], state_ret[l], state_gdn[l],
            rwkv_prm, (cos_s, sin_s, ret_tabs[4]), gdn_prm)
        yd = _moba_sample(qd, kd, vd, cache_kt, cache_vt, page_table, l, 16)
        ys = tuple(a.reshape(1, bs, GROUP_WIDTH) for a in (ya, yb, yc, yd))
        xs = _ffn(xs, ms[6], ms[7], ms[8], npre[2], npost[2], wg, wu, wdn, l, 1, bs, FFN_TF,
                  mix=(ys, ms[5], npost[1], w_out_b))
        hs_ = lambda a: a.reshape(bs, 1, N_HEADS, HEAD_DIM)
        conv_new = jnp.concatenate([state_gdn_conv[l][:, 1:], pc[:, None, :C_QKV]], axis=1)
        news_s.append((hs_(kd), hs_(vd), rw_new, pa, rt_new, gd_new, conv_new))

    kp, vp, rwp, shp, rtp, gdp, cvp = [jnp.stack(t) for t in zip(*news_p)]
    ks, vs, rws, shs, rts, gds, cvs = [jnp.stack(t) for t in zip(*news_s)]
    return (xp, xs.reshape(bs, 1, d), kp, vp, ks, vs, rwp, rws, shp, shs, rtp, rts, gdp, gds, cvp, cvs)
```
